```python
import math
import jax, jax.numpy as jnp
from jax import lax
import numpy as np

D_MODEL = 2048
BATCH = 16
SEQ = 256
DEPTH = 4
DEC_BATCH = 2
DEC_SEQ = 1024
PAST_LEN = 256

GRID_W = 64
EPS = 1e-6
MLA_HEADS = 16
QK_NOPE = 128
QK_ROPE = 64
V_HEAD = 128
Q_LORA = 512
KV_LORA = 256
MLA_WIDTH = MLA_HEADS * V_HEAD
ROPE_BASE = 10000.0
Q_BLOCK = 128
GLA_HEADS = 4
GLA_DK = 256
GLA_DV = 512
GLA_KEY = GLA_HEADS * GLA_DK
GLA_WIDTH = GLA_HEADS * GLA_DV
GLA_GATE_RANK = 16
GLA_GATE_NORM = 16.0
GLA_CHUNK = 32
SSM_HEADS = 64
SSM_HEADDIM = 64
SSM_WIDTH = SSM_HEADS * SSM_HEADDIM
SSM_GROUPS = 8
SSM_STATE = 128
SSM_CONV = 5
SSM_CHUNK = 64
SSM_BC = SSM_GROUPS * SSM_STATE
SSM_CONV_CH = SSM_WIDTH + 2 * SSM_BC
IN_SIZES = (Q_LORA, KV_LORA, QK_ROPE, MLA_WIDTH,
            GLA_KEY, GLA_KEY, GLA_WIDTH, GLA_GATE_RANK, GLA_GATE_RANK, GLA_WIDTH,
            SSM_WIDTH, SSM_CONV_CH, SSM_HEADS, SSM_HEADS,
            D_MODEL, D_MODEL, D_MODEL)
IN_TOTAL = sum(IN_SIZES)

kernel_name = "hybrid_mla_gla_ssd_prefix_diffusion_step"


def rms_norm(x, w):
    xf = x.astype(jnp.float32)
    y = xf * lax.rsqrt(jnp.mean(xf * xf, axis=-1, keepdims=True) + EPS)
    return (y * w.astype(jnp.float32)).astype(x.dtype)


def axial_rope(n_tok):
    rows_count = n_tok // GRID_W
    rows = jnp.repeat(jnp.arange(rows_count, dtype=jnp.float32), GRID_W)
    cols = jnp.tile(jnp.arange(GRID_W, dtype=jnp.float32), rows_count)
    n_freq = QK_ROPE // 4
    inv = ROPE_BASE ** (-jnp.arange(n_freq, dtype=jnp.float32) / n_freq)
    ang = jnp.concatenate([rows[:, None] * inv, cols[:, None] * inv], axis=-1)
    return jnp.cos(ang), jnp.sin(ang)


def apply_rope(x, cos, sin):
    xf = x.astype(jnp.float32)
    x1, x2 = xf[..., 0::2], xf[..., 1::2]
    out = jnp.stack([x1 * cos - x2 * sin, x1 * sin + x2 * cos], axis=-1).reshape(x.shape)
    return out.astype(x.dtype)


def mla_expand(ckv, krope, w_kv_b):
    B, T, _ = ckv.shape
    kv = (ckv @ w_kv_b).reshape(B, T, MLA_HEADS, QK_NOPE + V_HEAD)
    kr = jnp.broadcast_to(krope[:, :, None, :].astype(kv.dtype), (B, T, MLA_HEADS, QK_ROPE))
    k = jnp.concatenate([kv[..., :QK_NOPE], kr], axis=-1)
    return k, kv[..., QK_NOPE:]


def mla_attend(q, k, v):
    B, S, H, Dq = q.shape
    nb = S // Q_BLOCK
    qb = jnp.moveaxis(q.reshape(B, nb, Q_BLOCK, H, Dq), 1, 0)
    scale = Dq ** -0.5

    def one(qblk):
        s = jnp.einsum('bqhd,bkhd->bhqk', qblk, k).astype(jnp.float32) * scale
        p = jax.nn.softmax(s, axis=-1).astype(v.dtype)
        return jnp.einsum('bhqk,bkhd->bqhd', p, v)

    o = lax.map(one, qb)
    return jnp.moveaxis(o, 0, 1).reshape(B, S, H, v.shape[-1])


def to_chunks(t, c):
    B, T = t.shape[:2]
    return jnp.moveaxis(t.reshape(B, T // c, c, *t.shape[2:]), 1, 0)


def from_chunks(t):
    n, B, c = t.shape[:3]
    return jnp.moveaxis(t, 0, 1).reshape(B, n * c, *t.shape[3:])


def gla_scan(q, k, v, log_a, s0):
    C = GLA_CHUNK
    mask = jnp.tril(jnp.ones((C, C), dtype=bool))[None, :, :, None, None]

    def step(s, inp):
        qc, kc, vc, gc = inp
        b = jnp.cumsum(gc, axis=1)
        o_inter = jnp.einsum('bchk,bhkv->bchv', qc * jnp.exp(b), s)
        rel = jnp.where(mask, b[:, :, None] - b[:, None, :], -jnp.inf)
        att = jnp.einsum('bqhk,bshk,bqshk->bhqs', qc, kc, jnp.exp(rel))
        o_intra = jnp.einsum('bhqs,bshv->bqhv', att, vc)
        b_last = b[:, -1]
        k_dec = kc * jnp.exp(b_last[:, None] - b)
        s_new = s * jnp.exp(b_last)[..., None] + jnp.einsum('bchk,bchv->bhkv', k_dec, vc)
        return s_new.astype(s.dtype), (o_inter + o_intra).astype(vc.dtype)

    s_fin, o = lax.scan(step, s0, (to_chunks(q, C), to_chunks(k, C), to_chunks(v, C), to_chunks(log_a, C)))
    return from_chunks(o), s_fin


def ssd_scan(x, dt, A, Bm, Cm, s0):
    C = SSM_CHUNK
    Bsz, _, H, P = x.shape
    G, N = Bm.shape[2], Bm.shape[3]
    Hg = H // G
    mask = jnp.tril(jnp.ones((C, C), dtype=bool))[None, :, :, None, None]

    def step(s, inp):
        xc, dtc, bc, cc = inp
        cum = jnp.cumsum(dtc * A, axis=1).reshape(Bsz, C, G, Hg)
        xg = (xc * dtc[..., None]).reshape(Bsz, C, G, Hg, P)
        sg = s.reshape(Bsz, G, Hg, P, N)
        y_inter = jnp.einsum('bcgn,bghpn->bcghp', cc, sg) * jnp.exp(cum)[..., None]
        cb = jnp.einsum('bqgn,bsgn->bgqs', cc, bc)
        dec = jnp.exp(jnp.where(mask, cum[:, :, None] - cum[:, None, :], -jnp.inf))
        y_intra = jnp.einsum('bgqs,bqsgh,bsghp->bqghp', cb, dec, xg)
        last = cum[:, -1]
        w = jnp.exp(last[:, None] - cum)
        s_new = sg * jnp.exp(last)[..., None, None] + jnp.einsum('bsgn,bsgh,bsghp->bghpn', bc, w, xg)
        y = (y_inter + y_intra).reshape(Bsz, C, H, P)
        return s_new.reshape(Bsz, H, P, N).astype(s.dtype), y.astype(xc.dtype)

    s_fin, y = lax.scan(step, s0, (to_chunks(x, C), to_chunks(dt, C), to_chunks(Bm, C), to_chunks(Cm, C)))
    return from_chunks(y), s_fin


def dwconv_centred(x, w, b):
    K = w.shape[0]
    y = lax.conv_general_dilated(x, w[:, None, :].astype(x.dtype), window_strides=(1,),
                                 padding=[((K - 1) // 2, K // 2)],
                                 dimension_numbers=('NWC', 'WIO', 'NWC'),
                                 feature_group_count=x.shape[-1])
    return y + b.astype(x.dtype)


def mixer_layer(x, mod, ctx, norm_w, w_in, q_a_norm, w_q_b, kv_a_norm, w_kv_b,
                gla_w_gate2, gla_b_gate, gla_norm, ssm_conv_w, ssm_conv_b, ssm_dt_bias,
                ssm_a_log, ssm_d, ssm_norm, w_br_mla, w_br_gla, w_br_ssm, w_out):
    B, S, _ = x.shape
    f32 = jnp.float32
    latent = ctx is not None
    shift, scale, gate = jnp.split(mod[:, None, :], 3, axis=-1)
    h = rms_norm(x, norm_w) * (1.0 + scale) + shift
    offsets = np.cumsum(IN_SIZES)[:-1].tolist()
    (q_a, kv_a, k_r, g_mla, q_l, k_l, v_l, ga_f, ga_b, g_gla, z, xbc, dt_f, dt_b,
     m_mla, m_gla, m_ssm) = jnp.split(h @ w_in, offsets, axis=-1)

    q = (rms_norm(q_a, q_a_norm) @ w_q_b).reshape(B, S, MLA_HEADS, QK_NOPE + QK_ROPE)
    q_nope, q_rope = q[..., :QK_NOPE], q[..., QK_NOPE:]
    ckv = rms_norm(kv_a, kv_a_norm)
    if latent:
        cos, sin = axial_rope(S)
        q_rope = apply_rope(q_rope, cos[:, None, :], sin[:, None, :])
        k_r = apply_rope(k_r, cos, sin)
    q = jnp.concatenate([q_nope, q_rope], axis=-1)
    k, v = mla_expand(ckv, k_r, w_kv_b)
    if latent:
        k_c, v_c = mla_expand(ctx[0], ctx[1], w_kv_b)
        k = jnp.concatenate([k_c.astype(k.dtype), k], axis=1)
        v = jnp.concatenate([v_c.astype(v.dtype), v], axis=1)
    o_mla = mla_attend(q, k, v).reshape(B, S, MLA_WIDTH) * jax.nn.silu(g_mla)

    qg = q_l.reshape(B, S, GLA_HEADS, GLA_DK) * (GLA_DK ** -0.5)
    kg = k_l.reshape(B, S, GLA_HEADS, GLA_DK)
    vg = v_l.reshape(B, S, GLA_HEADS, GLA_DV)
    la_f = (jax.nn.log_sigmoid((ga_f @ gla_w_gate2[0] + gla_b_gate[0]).astype(f32)) / GLA_GATE_NORM).reshape(B, S, GLA_HEADS, GLA_DK)
    la_b = (jax.nn.log_sigmoid((ga_b @ gla_w_gate2[1] + gla_b_gate[1]).astype(f32)) / GLA_GATE_NORM).reshape(B, S, GLA_HEADS, GLA_DK)
    if latent:
        s0f, s0b = ctx[2].astype(x.dtype), ctx[3].astype(x.dtype)
    else:
        s0f = jnp.zeros((B, GLA_HEADS, GLA_DK, GLA_DV), x.dtype)
        s0b = s0f
    o_f, gla_sf = gla_scan(qg, kg, vg, la_f, s0f)
    o_b, gla_sb = gla_scan(jnp.flip(qg, 1), jnp.flip(kg, 1), jnp.flip(vg, 1), jnp.flip(la_b, 1), s0b)
    o_gla = rms_norm(o_f + jnp.flip(o_b, 1), gla_norm).reshape(B, S, GLA_WIDTH) * jax.nn.silu(g_gla)

    xbc = jax.nn.silu(dwconv_centred(xbc, ssm_conv_w, ssm_conv_b))
    xs, Bm, Cm = jnp.split(xbc, [SSM_WIDTH, SSM_WIDTH + SSM_BC], axis=-1)
    xs = xs.reshape(B, S, SSM_HEADS, SSM_HEADDIM)
    Bm = Bm.reshape(B, S, SSM_GROUPS, SSM_STATE)
    Cm = Cm.reshape(B, S, SSM_GROUPS, SSM_STATE)
    A_f = -jnp.exp(ssm_a_log[0].astype(f32))
    A_b = -jnp.exp(ssm_a_log[1].astype(f32))
    dtf = jax.nn.softplus(dt_f.astype(f32) + ssm_dt_bias[0].astype(f32))
    dtb = jax.nn.softplus(dt_b.astype(f32) + ssm_dt_bias[1].astype(f32))
    if latent:
        h0f, h0b = ctx[4].astype(x.dtype), ctx[5].astype(x.dtype)
    else:
        h0f = jnp.zeros((B, SSM_HEADS, SSM_HEADDIM, SSM_STATE), x.dtype)
        h0b = h0f
    y_f, ssm_sf = ssd_scan(xs, dtf, A_f, Bm, Cm, h0f)
    y_b, ssm_sb = ssd_scan(jnp.flip(xs, 1), jnp.flip(dtb, 1), A_b, jnp.flip(Bm, 1), jnp.flip(Cm, 1), h0b)
    y = y_f + jnp.flip(y_b, 1) + xs * ssm_d[:, None].astype(xs.dtype)
    o_ssm = rms_norm(y.reshape(B, S, SSM_WIDTH) * jax.nn.silu(z), ssm_norm)

    merged = (jax.nn.sigmoid(m_mla) * (o_mla @ w_br_mla)
              + jax.nn.sigmoid(m_gla) * (o_gla @ w_br_gla)
              + jax.nn.sigmoid(m_ssm) * (o_ssm @ w_br_ssm))
    out = (x + gate * (merged @ w_out)).astype(x.dtype)
    return out, (ckv, k_r, gla_sf, gla_sb, ssm_sf, ssm_sb)


def setup_inputs(seed: int = 0) -> dict:
    key = jax.random.key(seed)
    ks = iter(jax.random.split(key, 48))
    f32 = jnp.float32

    def nrm(shape, s):
        return jax.random.normal(next(ks), shape, f32) * s

    def gain(shape):
        return 1.0 + nrm(shape, 0.02)

    u = jax.random.uniform(next(ks), (DEPTH, 2, SSM_HEADS), f32)
    dt0 = jnp.exp(u * (math.log(0.1) - math.log(0.001)) + math.log(0.001))
    return {
        "x_prompt": nrm((BATCH, SEQ, D_MODEL), 1.0),
        "x_sample": nrm((DEC_BATCH, DEC_SEQ, D_MODEL), 1.0),
        "cache_mla_ckv": nrm((DEC_BATCH, DEPTH, PAST_LEN, KV_LORA), 1.0),
        "cache_mla_krope": nrm((DEC_BATCH, DEPTH, PAST_LEN, QK_ROPE), 1.0),
        "state_gla_fwd": nrm((DEC_BATCH, DEPTH, GLA_HEADS, GLA_DK, GLA_DV), 1.0),
        "state_gla_bwd": nrm((DEC_BATCH, DEPTH, GLA_HEADS, GLA_DK, GLA_DV), 1.0),
        "state_ssm_fwd": nrm((DEC_BATCH, DEPTH, SSM_HEADS, SSM_HEADDIM, SSM_STATE), 1.0),
        "state_ssm_bwd": nrm((DEC_BATCH, DEPTH, SSM_HEADS, SSM_HEADDIM, SSM_STATE), 1.0),
        "c": nrm((DEC_BATCH, D_MODEL), 1.0),
        "c_ctx": nrm((D_MODEL,), 1.0),
        "w_mod": nrm((DEPTH, D_MODEL, 3 * D_MODEL), 0.5 * D_MODEL ** -0.5),
        "b_mod": nrm((DEPTH, 3 * D_MODEL), 0.02),
        "norm_w": gain((DEPTH, D_MODEL)),
        "w_in": nrm((DEPTH, D_MODEL, IN_TOTAL), D_MODEL ** -0.5),
        "q_a_norm": gain((DEPTH, Q_LORA)),
        "w_q_b": nrm((DEPTH, Q_LORA, MLA_HEADS * (QK_NOPE + QK_ROPE)), Q_LORA ** -0.5),
        "kv_a_norm": gain((DEPTH, KV_LORA)),
        "w_kv_b": nrm((DEPTH, KV_LORA, MLA_HEADS * (QK_NOPE + V_HEAD)), KV_LORA ** -0.5),
        "gla_w_gate2": nrm((DEPTH, 2, GLA_GATE_RANK, GLA_KEY), GLA_GATE_RANK ** -0.5),
        "gla_b_gate": nrm((DEPTH, 2, GLA_KEY), 0.1),
        "gla_norm": gain((DEPTH, GLA_DV)),
        "ssm_conv_w": nrm((DEPTH, SSM_CONV, SSM_CONV_CH), SSM_CONV ** -0.5),
        "ssm_conv_b": nrm((DEPTH, SSM_CONV_CH), 0.02),
        "ssm_dt_bias": dt0 + jnp.log(-jnp.expm1(-dt0)),
        "ssm_a_log": jnp.log(jax.random.uniform(next(ks), (DEPTH, 2, SSM_HEADS), f32, 1.0, 16.0)),
        "ssm_d": gain((DEPTH, SSM_HEADS)),
        "ssm_norm": gain((DEPTH, SSM_WIDTH)),
        "w_br_mla": nrm((DEPTH, MLA_WIDTH, D_MODEL), MLA_WIDTH ** -0.5),
        "w_br_gla": nrm((DEPTH, GLA_WIDTH, D_MODEL), GLA_WIDTH ** -0.5),
        "w_br_ssm": nrm((DEPTH, SSM_WIDTH, D_MODEL), SSM_WIDTH ** -0.5),
        "w_out": nrm((DEPTH, D_MODEL, D_MODEL), D_MODEL ** -0.5),
        "final_norm": gain((D_MODEL,)),
    }


def reference(x_prompt, x_sample, cache_mla_ckv, cache_mla_krope, state_gla_fwd, state_gla_bwd,
              state_ssm_fwd, state_ssm_bwd, c, c_ctx, w_mod, b_mod, norm_w, w_in, q_a_norm, w_q_b,
              kv_a_norm, w_kv_b, gla_w_gate2, gla_b_gate, gla_norm, ssm_conv_w, ssm_conv_b,
              ssm_dt_bias, ssm_a_log, ssm_d, ssm_norm, w_br_mla, w_br_gla, w_br_ssm, w_out, final_norm):
    xp, xs = x_prompt, x_sample
    ckv_l, kr_l, gf_l, gb_l, sf_l, sb_l = [], [], [], [], [], []
    for l in range(DEPTH):
        mod_ctx = jax.nn.silu(c_ctx)[None, :] @ w_mod[l] + b_mod[l]
        mod_lat = jax.nn.silu(c) @ w_mod[l] + b_mod[l]
        lw = (norm_w[l], w_in[l], q_a_norm[l], w_q_b[l], kv_a_norm[l], w_kv_b[l],
              gla_w_gate2[l], gla_b_gate[l], gla_norm[l], ssm_conv_w[l], ssm_conv_b[l],
              ssm_dt_bias[l], ssm_a_log[l], ssm_d[l], ssm_norm[l], w_br_mla[l], w_br_gla[l],
              w_br_ssm[l], w_out[l])
        xp, st = mixer_layer(xp, mod_ctx, None, *lw)
        ckv_l.append(st[0]); kr_l.append(st[1]); gf_l.append(st[2])
        gb_l.append(st[3]); sf_l.append(st[4]); sb_l.append(st[5])
        ctx = (cache_mla_ckv[:, l], cache_mla_krope[:, l], state_gla_fwd[:, l], state_gla_bwd[:, l],
               state_ssm_fwd[:, l], state_ssm_bwd[:, l])
        xs, _ = mixer_layer(xs, mod_lat, ctx, *lw)
    y_prompt = rms_norm(xp, final_norm)
    y_sample = rms_norm(xs, final_norm)
    new_mla_ckv = jnp.stack(ckv_l, axis=1)
    new_mla_krope = jnp.stack(kr_l, axis=1)
    new_gla_fwd = jnp.stack(gf_l, axis=1)
    new_gla_bwd = jnp.stack(gb_l, axis=1)
    new_ssm_fwd = jnp.stack(sf_l, axis=1)
    new_ssm_bwd = jnp.stack(sb_l, axis=1)
    return (y_prompt, y_sample, new_mla_ckv, new_mla_krope, new_gla_fwd, new_gla_bwd, new_ssm_fwd, new_ssm_bwd)
```

```python
import functools

import jax
import jax.numpy as jnp
import numpy as np
from jax import lax
from jax.experimental import pallas as pl
from jax.experimental.pallas import tpu as pltpu

F32 = jnp.float32
BF16 = jnp.bfloat16

D_MODEL = 2048
DEPTH = 4
GRID_W = 64
EPS = 1e-6
MLA_HEADS = 16
QK_NOPE = 128
QK_ROPE = 64
V_HEAD = 128
Q_LORA = 512
KV_LORA = 256
MLA_WIDTH = MLA_HEADS * V_HEAD
ROPE_BASE = 10000.0
GLA_HEADS = 4
GLA_DK = 256
GLA_DV = 512
GLA_KEY = GLA_HEADS * GLA_DK
GLA_WIDTH = GLA_HEADS * GLA_DV
GLA_GATE_RANK = 16
GLA_GATE_NORM = 16.0
SSM_HEADS = 64
SSM_HEADDIM = 64
SSM_WIDTH = SSM_HEADS * SSM_HEADDIM
SSM_GROUPS = 8
SSM_HPG = SSM_HEADS // SSM_GROUPS
SSM_GW = SSM_HPG * SSM_HEADDIM
SSM_STATE = 128
SSM_CONV = 5
SSM_BC = SSM_GROUPS * SSM_STATE
SSM_CONV_CH = SSM_WIDTH + 2 * SSM_BC

IN_SIZES = (Q_LORA, KV_LORA, QK_ROPE, MLA_WIDTH,
            GLA_KEY, GLA_KEY, GLA_WIDTH, GLA_GATE_RANK, GLA_GATE_RANK, GLA_WIDTH,
            SSM_WIDTH, SSM_CONV_CH, SSM_HEADS, SSM_HEADS,
            D_MODEL, D_MODEL, D_MODEL)
IN_NAMES = ("q_a", "kv_a", "k_r", "g_mla", "q_l", "k_l", "v_l", "ga_f", "ga_b", "g_gla",
            "z", "xbc", "dt_f", "dt_b", "m_mla", "m_gla", "m_ssm")
_IN_OFF = dict(zip(IN_NAMES, np.cumsum((0,) + IN_SIZES[:-1]).tolist()))
_IN_SIZE = dict(zip(IN_NAMES, IN_SIZES))

PACK_ORDER = ("xbc", "z", "g_mla", "v_l", "g_gla", "m_mla", "m_gla", "m_ssm", "q_l", "k_l", "q_a", "kv_a")
SMALL_ORDER = ("k_r", "ga_f", "ga_b", "dt_f", "dt_b")
SMALL_W = 256
PK = {}
_o = 0
for _n in PACK_ORDER:
    PK[_n] = _o
    _o += _IN_SIZE[_n]
PK["small"] = _o
SM = {}
_s = 0
for _n in SMALL_ORDER:
    SM[_n] = _s
    _s += _IN_SIZE[_n]
PACK_TOTAL = _o + SMALL_W

SEQ_BLOCK = 256
GLA_CHUNK = 128
SSM_CHUNK = 128
V7X_VMEM_LIMIT = 56 * 1024 * 1024


def _cparams(sem):
    return pltpu.CompilerParams(dimension_semantics=sem, vmem_limit_bytes=V7X_VMEM_LIMIT)


def _silu(x):
    return x * (1.0 / (1.0 + jnp.exp(-x)))


def _sigmoid(x):
    return 1.0 / (1.0 + jnp.exp(-x))


def _softplus(x):
    return jnp.maximum(x, 0.0) + jnp.log(1.0 + jnp.exp(-jnp.abs(x)))


def _dot(a, b):
    return jnp.dot(a, b, preferred_element_type=F32)


def _dot_nt(a, b):
    return lax.dot_general(a, b, (((1,), (1,)), ((), ())), preferred_element_type=F32)


def _split3(x):
    hi = x.astype(BF16)
    r1 = x - hi.astype(F32)
    mid = r1.astype(BF16)
    lo = (r1 - mid.astype(F32)).astype(BF16)
    return hi, mid, lo


def _tri_left(tri, x):
    hi, mid, lo = _split3(x)
    return _dot(tri, hi) + _dot(tri, mid) + _dot(tri, lo)


def _tri_right(x, tri):
    hi, mid, lo = _split3(x)
    return _dot(hi, tri) + _dot(mid, tri) + _dot(lo, tri)


def _mm_kernel(*refs, rms, ep, cast_w):
    it = iter(refs)
    x_ref = next(it)
    w_ref = next(it)
    gain_ref = next(it) if rms else None
    m_ref = next(it) if ep in ("sig", "sigadd") else None
    prev_ref = next(it) if ep == "sigadd" else None
    res_ref = next(it) if ep == "resid" else None
    gate_ref = next(it) if ep == "resid" else None
    o_ref = next(it)
    wbf_ref = next(it) if cast_w else None

    if cast_w:
        @pl.when(pl.program_id(1) == 0)
        def _():
            wbf_ref[...] = w_ref[...].astype(BF16)
        w = wbf_ref[...]
    else:
        w = w_ref[...]
    x = x_ref[...]
    if rms:
        xf = x.astype(F32)
        ms = jnp.mean(xf * xf, axis=-1, keepdims=True)
        x = xf * lax.rsqrt(ms + EPS) * gain_ref[...]
    acc = _dot(x.astype(BF16), w)
    if ep == "sig":
        acc = _sigmoid(m_ref[...]) * acc
    elif ep == "sigadd":
        acc = prev_ref[...].astype(F32) + _sigmoid(m_ref[...]) * acc
    elif ep == "resid":
        acc = res_ref[...] + gate_ref[...] * acc
    o_ref[...] = acc.astype(o_ref.dtype)


def _matmul(x, w, *, name, layer=None, n_out, tm, tn, out_dtype, w_col0=0, gain=None, x_col0=0, k=None,
            ep=None, m=None, m_col0=0, prev=None, res=None, gate_rows=None):
    M = x.shape[0]
    K = x.shape[1] if k is None else k
    assert M % tm == 0 and n_out % tn == 0 and w_col0 % tn == 0 and m_col0 % tn == 0 and x_col0 % K == 0
    xj = x_col0 // K
    cast_w = w.dtype != BF16
    rms = gain is not None
    wj = w_col0 // tn
    mj = m_col0 // tn
    if layer is None:
        w_spec = pl.BlockSpec((K, tn), lambda j, i: (0, j + wj))
    else:
        w_spec = pl.BlockSpec((None, K, tn), lambda j, i: (layer, 0, j + wj))
    in_specs = [pl.BlockSpec((tm, K), lambda j, i: (i, xj)), w_spec]
    args = [x, w]
    if rms:
        in_specs.append(pl.BlockSpec((1, K), lambda j, i: (0, 0)))
        args.append(gain.reshape(1, K))
    if ep in ("sig", "sigadd"):
        in_specs.append(pl.BlockSpec((tm, tn), lambda j, i: (i, j + mj)))
        args.append(m)
    if ep == "sigadd":
        in_specs.append(pl.BlockSpec((tm, tn), lambda j, i: (i, j)))
        args.append(prev)
    if ep == "resid":
        assert tm == SEQ_BLOCK
        in_specs.append(pl.BlockSpec((tm, tn), lambda j, i: (i, j)))
        args.append(res)
        in_specs.append(pl.BlockSpec((None, 1, tn), lambda j, i: (i, 0, j)))
        args.append(gate_rows)
    scratch = [pltpu.VMEM((K, tn), BF16)] if cast_w else []
    return pl.pallas_call(
        functools.partial(_mm_kernel, rms=rms, ep=ep, cast_w=cast_w),
        grid=(n_out // tn, M // tm),
        in_specs=in_specs,
        out_specs=pl.BlockSpec((tm, tn), lambda j, i: (i, j)),
        out_shape=jax.ShapeDtypeStruct((M, n_out), out_dtype),
        scratch_shapes=scratch,
        compiler_params=_cparams(("arbitrary", "arbitrary")),
        name=name,
    )(*args)


def _prenorm_kernel(x_ref, nw_ref, mod_ref, o_ref):
    x = x_ref[...]
    ms = jnp.mean(x * x, axis=-1, keepdims=True)
    y = x * lax.rsqrt(ms + EPS) * nw_ref[...]
    shift = mod_ref[:, 0:D_MODEL]
    scale = mod_ref[:, D_MODEL:2 * D_MODEL]
    o_ref[...] = (y * (1.0 + scale) + shift).astype(o_ref.dtype)


def _prenorm(x, norm_w, mod_rows):
    M = x.shape[0]
    return pl.pallas_call(
        _prenorm_kernel,
        grid=(M // SEQ_BLOCK,),
        in_specs=[pl.BlockSpec((SEQ_BLOCK, D_MODEL), lambda i: (i, 0)),
                  pl.BlockSpec((1, D_MODEL), lambda i: (0, 0)),
                  pl.BlockSpec((None, 1, 3 * D_MODEL), lambda i: (i, 0, 0))],
        out_specs=pl.BlockSpec((SEQ_BLOCK, D_MODEL), lambda i: (i, 0)),
        out_shape=jax.ShapeDtypeStruct((M, D_MODEL), BF16),
        compiler_params=_cparams(("arbitrary",)),
        name="prenorm",
    )(x, norm_w.reshape(1, D_MODEL), mod_rows)


def _rmsnorm_kernel(x_ref, w_ref, o_ref):
    x = x_ref[...]
    ms = jnp.mean(x * x, axis=-1, keepdims=True)
    o_ref[...] = (x * lax.rsqrt(ms + EPS) * w_ref[...]).astype(o_ref.dtype)


def _rmsnorm(x, w, *, x_col0=0, width=None, tm=512):
    M = x.shape[0]
    width = x.shape[1] if width is None else width
    assert x_col0 % width == 0 and M % tm == 0
    cj = x_col0 // width
    return pl.pallas_call(
        _rmsnorm_kernel,
        grid=(M // tm,),
        in_specs=[pl.BlockSpec((tm, width), lambda i: (i, cj)),
                  pl.BlockSpec((1, width), lambda i: (0, 0))],
        out_specs=pl.BlockSpec((tm, width), lambda i: (i, 0)),
        out_shape=jax.ShapeDtypeStruct((M, width), F32),
        compiler_params=_cparams(("arbitrary",)),
        name="rmsnorm",
    )(x, w.reshape(1, width))


def _swap_pairs(x):
    n = x.shape[-1]
    nxt = pltpu.roll(x, n - 1, axis=1)
    prv = pltpu.roll(x, 1, axis=1)
    lane = lax.broadcasted_iota(jnp.int32, x.shape, 1)
    return jnp.where((lane & 1) == 0, nxt, prv)


def _mla_kernel(*refs, latent, tq):
    it = iter(refs)
    qn_ref = next(it)
    qr_ref = next(it)
    g_ref = next(it)
    kv_ref = next(it)
    sm_ref = next(it)
    if latent:
        kvc_ref = next(it)
        krc_ref = next(it)
        cq_ref = next(it)
        sq_ref = next(it)
        ck_ref = next(it)
        sk_ref = next(it)
    o_ref = next(it)

    scale = float(QK_NOPE + QK_ROPE) ** -0.5
    qr = qr_ref[...]
    sm = sm_ref[...]
    if latent:
        qr = qr * cq_ref[...] + _swap_pairs(qr) * sq_ref[...]
        sm = sm * ck_ref[...] + _swap_pairs(sm) * sk_ref[...]
    kr = sm[:, 0:QK_ROPE].astype(BF16)
    if latent:
        krc = krc_ref[...].astype(BF16)

    for h in range(MLA_HEADS):
        qn_h = qn_ref[:, h * QK_NOPE:(h + 1) * QK_NOPE]
        qr_h = qr[:, h * QK_ROPE:(h + 1) * QK_ROPE].astype(BF16)
        c0 = h * (QK_NOPE + V_HEAD)
        kn_h = kv_ref[:, c0:c0 + QK_NOPE]
        v_h = kv_ref[:, c0 + QK_NOPE:c0 + QK_NOPE + V_HEAD]
        s_own = (_dot_nt(qn_h, kn_h) + _dot_nt(qr_h, kr)) * scale
        mx = jnp.max(s_own, axis=-1, keepdims=True)
        if latent:
            knc_h = kvc_ref[:, c0:c0 + QK_NOPE]
            vc_h = kvc_ref[:, c0 + QK_NOPE:c0 + QK_NOPE + V_HEAD]
            s_ctx = (_dot_nt(qn_h, knc_h) + _dot_nt(qr_h, krc)) * scale
            mx = jnp.maximum(mx, jnp.max(s_ctx, axis=-1, keepdims=True))
            p_ctx = jnp.exp(s_ctx - mx)
        p_own = jnp.exp(s_own - mx)
        den = jnp.sum(p_own, axis=-1, keepdims=True)
        if latent:
            den = den + jnp.sum(p_ctx, axis=-1, keepdims=True)
        inv = 1.0 / den
        o_h = _dot((p_own * inv).astype(BF16), v_h)
        if latent:
            o_h = o_h + _dot((p_ctx * inv).astype(BF16), vc_h)
        g_h = g_ref[:, h * V_HEAD:(h + 1) * V_HEAD]
        o_ref[:, h * V_HEAD:(h + 1) * V_HEAD] = (o_h * _silu(g_h)).astype(o_ref.dtype)


def _mla(qn, qr, proj, kv, *, n_batch, S, row0, latent, kv_ctx_row0=0, krope_ctx=None, tables=None):
    tq = min(S, 256)
    nq = S // tq
    rb = row0 // tq
    sb = row0 // S
    HW = MLA_HEADS * (QK_NOPE + V_HEAD)
    gj = PK["g_mla"] // MLA_WIDTH
    smj = PK["small"] // SMALL_W
    in_specs = [
        pl.BlockSpec((tq, MLA_HEADS * QK_NOPE), lambda b, i: (rb + b * nq + i, 0)),
        pl.BlockSpec((tq, MLA_HEADS * QK_ROPE), lambda b, i: (rb + b * nq + i, 0)),
        pl.BlockSpec((tq, MLA_WIDTH), lambda b, i: (rb + b * nq + i, gj)),
        pl.BlockSpec((S, HW), lambda b, i: (sb + b, 0)),
        pl.BlockSpec((S, SMALL_W), lambda b, i: (sb + b, smj)),
    ]
    args = [qn, qr, proj, kv, proj]
    if latent:
        Tc = krope_ctx.shape[1]
        cb = kv_ctx_row0 // Tc
        cos_q, sin_q, cos_k, sin_k = tables
        in_specs += [
            pl.BlockSpec((Tc, HW), lambda b, i: (cb + b, 0)),
            pl.BlockSpec((None, Tc, QK_ROPE), lambda b, i: (b, 0, 0)),
            pl.BlockSpec((tq, MLA_HEADS * QK_ROPE), lambda b, i: (i, 0)),
            pl.BlockSpec((tq, MLA_HEADS * QK_ROPE), lambda b, i: (i, 0)),
            pl.BlockSpec((S, SMALL_W), lambda b, i: (0, 0)),
            pl.BlockSpec((S, SMALL_W), lambda b, i: (0, 0)),
        ]
        args += [kv, krope_ctx, cos_q, sin_q, cos_k, sin_k]
    return pl.pallas_call(
        functools.partial(_mla_kernel, latent=latent, tq=tq),
        grid=(n_batch, nq),
        in_specs=in_specs,
        out_specs=pl.BlockSpec((tq, MLA_WIDTH), lambda b, i: (b * nq + i, 0)),
        out_shape=jax.ShapeDtypeStruct((n_batch * S, MLA_WIDTH), BF16),
        compiler_params=_cparams(("arbitrary", "arbitrary")),
        name="mla_latent" if latent else "mla_context",
    )(*args)


def _rope_tables(S):
    rows = jnp.repeat(jnp.arange(S // GRID_W, dtype=F32), GRID_W)
    cols = jnp.tile(jnp.arange(GRID_W, dtype=F32), S // GRID_W)
    n_freq = QK_ROPE // 4
    inv = ROPE_BASE ** (-jnp.arange(n_freq, dtype=F32) / n_freq)
    ang = jnp.concatenate([rows[:, None] * inv, cols[:, None] * inv], axis=-1)
    cos = jnp.repeat(jnp.cos(ang), 2, axis=-1)
    sin = jnp.repeat(jnp.sin(ang), 2, axis=-1)
    sign = jnp.tile(jnp.array([-1.0, 1.0], F32), QK_ROPE // 2)
    sin = sin * sign
    cos_q = jnp.tile(cos, (1, MLA_HEADS))
    sin_q = jnp.tile(sin, (1, MLA_HEADS))
    pad = SMALL_W - QK_ROPE
    cos_k = jnp.concatenate([cos, jnp.ones((S, pad), F32)], axis=-1)
    sin_k = jnp.concatenate([sin, jnp.zeros((S, pad), F32)], axis=-1)
    return cos_q, sin_q, cos_k, sin_k


def _gla_kernel(*refs, S, C, has_state, emit_state):
    it = iter(refs)
    q_ref = next(it)
    k_ref = next(it)
    v_ref = next(it)
    sm_ref = next(it)
    gg_ref = next(it)
    w2_ref = next(it)
    b2_ref = next(it)
    nw_ref = next(it)
    s0_refs = (next(it), next(it)) if has_state else None
    o_ref = next(it)
    so_refs = (next(it), next(it)) if emit_state else None
    vt_ref = next(it)
    la_ref = next(it)
    acc_ref = next(it)
    st_ref = next(it)

    n = S // C
    scale = float(GLA_DK) ** -0.5
    sm = sm_ref[...]
    for d, name in enumerate(("ga_f", "ga_b")):
        ga = sm[:, SM[name]:SM[name] + GLA_GATE_RANK].astype(BF16)
        xg = _dot(ga, w2_ref[d].astype(BF16)) + b2_ref[d]
        la_ref[d] = -_softplus(-xg) * (1.0 / GLA_GATE_NORM)
    for c in range(n):
        vt_ref[c] = v_ref[c * C:(c + 1) * C, :].T.astype(BF16)
    for d in range(2):
        if has_state:
            st_ref[d] = s0_refs[d][...].T
        else:
            st_ref[d] = jnp.zeros((GLA_DV, GLA_DK), F32)

    row = lax.broadcasted_iota(jnp.int32, (C, C), 0)
    col = lax.broadcasted_iota(jnp.int32, (C, C), 1)
    lower = row >= col
    tri = (jnp.where(lower, 1.0, 0.0).astype(BF16), jnp.where(row <= col, 1.0, 0.0).astype(BF16))
    mid = C // 2

    def chunk(c, d):
        r0 = pl.multiple_of(c * C, C)
        q = q_ref[pl.ds(r0, C), :] * scale
        k = k_ref[pl.ds(r0, C), :]
        v = v_ref[pl.ds(r0, C), :].astype(BF16)
        g = la_ref[d, pl.ds(r0, C), :]
        b = _tri_left(tri[d], g)
        tot = b[C - 1:C, :] if d == 0 else b[0:1, :]
        bm = b[mid:mid + 1, :]
        st = st_ref[d]
        o = _dot_nt((q * jnp.exp(b)).astype(BF16), st.astype(BF16))
        qa = (q * jnp.exp(b - bm)).astype(BF16)
        ka = (k * jnp.exp(bm - b)).astype(BF16)
        att = _dot_nt(qa, ka)
        att = jnp.where(lower if d == 0 else row <= col, att, 0.0)
        o = o + _dot(att.astype(BF16), v)
        kd = (k * jnp.exp(tot - b)).astype(BF16)
        st_ref[d] = st * jnp.exp(tot) + _dot(vt_ref[c], kd)
        return r0, o

    def fwd(c, carry):
        r0, o = chunk(c, 0)
        acc_ref[pl.ds(r0, C), :] = o
        return carry

    def bwd(i, carry):
        r0, o = chunk(n - 1 - i, 1)
        acc_ref[pl.ds(r0, C), :] += o
        return carry

    lax.fori_loop(0, n, fwd, 0)
    lax.fori_loop(0, n, bwd, 0)

    o = acc_ref[...]
    ms = jnp.mean(o * o, axis=-1, keepdims=True)
    y = o * lax.rsqrt(ms + EPS) * nw_ref[...]
    o_ref[...] = (y * _silu(gg_ref[...])).astype(o_ref.dtype)
    if emit_state:
        for d in range(2):
            so_refs[d][...] = st_ref[d].T


def _gla(proj, w2, b2, nw, *, layer, n_batch, S, row0, states_in=None, states_out=None):
    C = GLA_CHUNK
    sb = row0 // S
    qj, kj, vj, gj = (PK["q_l"] // GLA_DK, PK["k_l"] // GLA_DK, PK["v_l"] // GLA_DV, PK["g_gla"] // GLA_DV)
    smj = PK["small"] // SMALL_W
    in_specs = [
        pl.BlockSpec((S, GLA_DK), lambda b, h: (sb + b, qj + h)),
        pl.BlockSpec((S, GLA_DK), lambda b, h: (sb + b, kj + h)),
        pl.BlockSpec((S, GLA_DV), lambda b, h: (sb + b, vj + h)),
        pl.BlockSpec((S, SMALL_W), lambda b, h: (sb + b, smj)),
        pl.BlockSpec((S, GLA_DV), lambda b, h: (sb + b, gj + h)),
        pl.BlockSpec((None, 2, GLA_GATE_RANK, GLA_DK), lambda b, h: (layer, 0, 0, h)),
        pl.BlockSpec((None, 2, 1, GLA_DK), lambda b, h: (layer, 0, 0, h)),
        pl.BlockSpec((None, 1, GLA_DV), lambda b, h: (layer, 0, 0)),
    ]
    args = [proj, proj, proj, proj, proj, w2, b2.reshape(DEPTH, 2, 1, GLA_KEY), nw.reshape(DEPTH, 1, GLA_DV)]
    has_state = states_in is not None
    if has_state:
        for s in states_in:
            in_specs.append(pl.BlockSpec((None, None, None, GLA_DK, GLA_DV), lambda b, h: (b, layer, h, 0, 0)))
            args.append(s)
    out_specs = [pl.BlockSpec((S, GLA_DV), lambda b, h: (b, h))]
    out_shape = [jax.ShapeDtypeStruct((n_batch * S, GLA_WIDTH), BF16)]
    aliases = {}
    emit_state = states_out is not None
    if emit_state:
        for s in states_out:
            out_specs.append(pl.BlockSpec((None, None, None, GLA_DK, GLA_DV), lambda b, h: (b, layer, h, 0, 0)))
            out_shape.append(jax.ShapeDtypeStruct((n_batch, DEPTH, GLA_HEADS, GLA_DK, GLA_DV), F32))
            if s is not None:
                aliases[len(args)] = len(out_shape) - 1
                in_specs.append(pl.BlockSpec(memory_space=pl.ANY))
                args.append(s)
    n = S // C
    res = pl.pallas_call(
        functools.partial(_gla_kernel_aliased, S=S, C=C, has_state=has_state, emit_state=emit_state,
                          n_alias=len(aliases)),
        grid=(n_batch, GLA_HEADS),
        in_specs=in_specs,
        out_specs=out_specs,
        out_shape=out_shape,
        input_output_aliases=aliases,
        scratch_shapes=[pltpu.VMEM((n, GLA_DV, C), BF16), pltpu.VMEM((2, S, GLA_DK), F32),
                        pltpu.VMEM((S, GLA_DV), F32), pltpu.VMEM((2, GLA_DV, GLA_DK), F32)],
        compiler_params=_cparams(("arbitrary", "arbitrary")),
        name="gla_latent" if has_state else "gla_context",
    )(*args)
    return res


def _drop_alias_refs(refs, n_in, n_alias):
    refs = list(refs)
    return refs[:n_in - n_alias] + refs[n_in:]


def _gla_kernel_aliased(*refs, S, C, has_state, emit_state, n_alias):
    n_in = 8 + (2 if has_state else 0) + n_alias
    _gla_kernel(*_drop_alias_refs(refs, n_in, n_alias), S=S, C=C, has_state=has_state, emit_state=emit_state)


CONV_TW = 512
CONV_PAD = 8


def _conv_kernel(x_ref, w_ref, b_ref, o_ref, xp_ref, *, S):
    left = (SSM_CONV - 1) // 2
    xp_ref[0:CONV_PAD, :] = jnp.zeros((CONV_PAD, CONV_TW), F32)
    xp_ref[CONV_PAD + S:CONV_PAD + S + CONV_PAD, :] = jnp.zeros((CONV_PAD, CONV_TW), F32)
    xp_ref[CONV_PAD:CONV_PAD + S, :] = x_ref[...]
    acc = jnp.zeros((S, CONV_TW), F32) + b_ref[...]
    for j in range(SSM_CONV):
        acc = acc + xp_ref[CONV_PAD - left + j:CONV_PAD - left + j + S, :] * w_ref[j:j + 1, :]
    o_ref[...] = _silu(acc)


def _conv(proj, cw, cb, *, layer, n_batch, S, row0):
    sb = row0 // S
    xj = PK["xbc"] // CONV_TW
    return pl.pallas_call(
        functools.partial(_conv_kernel, S=S),
        grid=(n_batch, SSM_CONV_CH // CONV_TW),
        in_specs=[pl.BlockSpec((S, CONV_TW), lambda b, j: (sb + b, xj + j)),
                  pl.BlockSpec((None, SSM_CONV, CONV_TW), lambda b, j: (layer, 0, j)),
                  pl.BlockSpec((None, 1, CONV_TW), lambda b, j: (layer, 0, j))],
        out_specs=pl.BlockSpec((S, CONV_TW), lambda b, j: (b, j)),
        out_shape=jax.ShapeDtypeStruct((n_batch * S, SSM_CONV_CH), F32),
        scratch_shapes=[pltpu.VMEM((S + 2 * CONV_PAD, CONV_TW), F32)],
        compiler_params=_cparams(("arbitrary", "arbitrary")),
        name="ssm_conv",
    )(proj, cw, cb.reshape(DEPTH, 1, SSM_CONV_CH))


def _expand_heads(cols, width):
    C = cols.shape[0]
    lane_head = lax.broadcasted_iota(jnp.int32, (C, width), 1) // SSM_HEADDIM
    out = jnp.zeros((C, width), F32)
    for j in range(SSM_HPG):
        out = jnp.where(lane_head == j, cols[:, j:j + 1], out)
    return out


def _ssd_kernel(*refs, S, C, has_state, emit_state):
    it = iter(refs)
    x_ref = next(it)
    b_ref = next(it)
    c_ref = next(it)
    z_ref = next(it)
    dtc_ref = next(it)
    dtr_ref = next(it)
    pc_ref = next(it)
    pr_ref = next(it)
    d_ref = next(it)
    s0_refs = (next(it), next(it)) if has_state else None
    y_ref = next(it)
    so_refs = (next(it), next(it)) if emit_state else None
    bt_ref = next(it)
    acc_ref = next(it)
    st_ref = next(it)

    n = S // C
    for c in range(n):
        bt_ref[c] = b_ref[c * C:(c + 1) * C, :].T.astype(BF16)
    for d in range(2):
        if has_state:
            s0 = jnp.concatenate([s0_refs[d][j] for j in range(SSM_HPG)], axis=0)
            st_ref[d] = s0.T
        else:
            st_ref[d] = jnp.zeros((SSM_STATE, SSM_GW), F32)

    row = lax.broadcasted_iota(jnp.int32, (C, C), 0)
    col = lax.broadcasted_iota(jnp.int32, (C, C), 1)
    lower = row >= col
    upper = row <= col
    tri_l = jnp.where(lower, 1.0, 0.0).astype(BF16)
    tri_u = jnp.where(upper, 1.0, 0.0).astype(BF16)

    def chunk(c, d):
        r0 = pl.multiple_of(c * C, C)
        x = x_ref[pl.ds(r0, C), :]
        bm = b_ref[pl.ds(r0, C), :].astype(BF16)
        cm = c_ref[pl.ds(r0, C), :].astype(BF16)
        a_c = -jnp.exp(pc_ref[d, 1])
        a_r = -jnp.exp(pr_ref[d, 1])
        dt_c = _softplus(dtc_ref[d, pl.ds(r0, C), :] + pc_ref[d, 0])
        dt_r = _softplus(dtr_ref[d, c] + pr_ref[d, 0])
        if d == 0:
            cum_c = _tri_left(tri_l, dt_c * a_c)
            cum_r = _tri_right(dt_r * a_r, tri_u)
        else:
            cum_c = _tri_left(tri_u, dt_c * a_c)
            cum_r = _tri_right(dt_r * a_r, tri_l)
        mask = lower if d == 0 else upper
        cum_e = _expand_heads(cum_c, SSM_GW)
        dt_e = _expand_heads(dt_c, SSM_GW)
        xg = x * dt_e
        st = st_ref[d]
        y = _dot(cm, st.astype(BF16)) * jnp.exp(cum_e)
        cb = _dot_nt(cm, bm)
        parts = []
        for j in range(SSM_HPG):
            dec = jnp.where(mask, jnp.exp(cum_c[:, j:j + 1] - cum_r[j:j + 1, :]), 0.0)
            parts.append(_dot((cb * dec).astype(BF16), xg[:, j * SSM_HEADDIM:(j + 1) * SSM_HEADDIM].astype(BF16)))
        y = y + jnp.concatenate(parts, axis=-1)
        last = cum_e[C - 1:C, :] if d == 0 else cum_e[0:1, :]
        xw = (xg * jnp.exp(last - cum_e)).astype(BF16)
        st_ref[d] = st * jnp.exp(last) + _dot(bt_ref[c], xw)
        return r0, y

    def fwd(c, carry):
        r0, y = chunk(c, 0)
        acc_ref[pl.ds(r0, C), :] = y
        return carry

    def bwd(i, carry):
        r0, y = chunk(n - 1 - i, 1)
        acc_ref[pl.ds(r0, C), :] += y
        return carry

    lax.fori_loop(0, n, fwd, 0)
    lax.fori_loop(0, n, bwd, 0)

    y = acc_ref[...] + x_ref[...] * d_ref[...]
    y_ref[...] = y * _silu(z_ref[...])
    if emit_state:
        for d in range(2):
            s_t = st_ref[d].T
            for j in range(SSM_HPG):
                so_refs[d][j] = s_t[j * SSM_HEADDIM:(j + 1) * SSM_HEADDIM, :]


def _ssd_kernel_aliased(*refs, S, C, has_state, emit_state, n_alias):
    n_in = 9 + (2 if has_state else 0) + n_alias
    _ssd_kernel(*_drop_alias_refs(refs, n_in, n_alias), S=S, C=C, has_state=has_state, emit_state=emit_state)


def _ssd(xbc, proj, dt_col, dt_row, p_col, p_row, d_lane, *, layer, n_batch, S, row0,
         states_in=None, states_out=None):
    C = SSM_CHUNK
    sb = row0 // S
    zj = PK["z"] // SSM_GW
    bj = SSM_WIDTH // SSM_STATE
    cj = (SSM_WIDTH + SSM_BC) // SSM_STATE
    in_specs = [
        pl.BlockSpec((S, SSM_GW), lambda b, g: (b, g)),
        pl.BlockSpec((S, SSM_STATE), lambda b, g: (b, bj + g)),
        pl.BlockSpec((S, SSM_STATE), lambda b, g: (b, cj + g)),
        pl.BlockSpec((S, SSM_GW), lambda b, g: (sb + b, zj + g)),
        pl.BlockSpec((2, None, S, SSM_HPG), lambda b, g: (0, g, sb + b, 0)),
        pl.BlockSpec((2, None, S // C, SSM_HPG, C), lambda b, g: (0, g, sb + b, 0, 0)),
        pl.BlockSpec((None, 2, 2, None, 1, SSM_HPG), lambda b, g: (layer, 0, 0, g, 0, 0)),
        pl.BlockSpec((None, 2, 2, None, SSM_HPG, 1), lambda b, g: (layer, 0, 0, g, 0, 0)),
        pl.BlockSpec((None, 1, SSM_GW), lambda b, g: (layer, 0, g)),
    ]
    args = [xbc, xbc, xbc, proj, dt_col, dt_row, p_col, p_row, d_lane]
    has_state = states_in is not None
    st_block = (None, None, SSM_HPG, SSM_HEADDIM, SSM_STATE)
    if has_state:
        for s in states_in:
            in_specs.append(pl.BlockSpec(st_block, lambda b, g: (b, layer, g, 0, 0)))
            args.append(s)
    out_specs = [pl.BlockSpec((S, SSM_GW), lambda b, g: (b, g))]
    out_shape = [jax.ShapeDtypeStruct((n_batch * S, SSM_WIDTH), F32)]
    aliases = {}
    emit_state = states_out is not None
    if emit_state:
        for s in states_out:
            out_specs.append(pl.BlockSpec(st_block, lambda b, g: (b, layer, g, 0, 0)))
            out_shape.append(jax.ShapeDtypeStruct((n_batch, DEPTH, SSM_HEADS, SSM_HEADDIM, SSM_STATE), F32))
            if s is not None:
                aliases[len(args)] = len(out_shape) - 1
                in_specs.append(pl.BlockSpec(memory_space=pl.ANY))
                args.append(s)
    n = S // C
    return pl.pallas_call(
        functools.partial(_ssd_kernel_aliased, S=S, C=C, has_state=has_state, emit_state=emit_state,
                          n_alias=len(aliases)),
        grid=(n_batch, SSM_GROUPS),
        in_specs=in_specs,
        out_specs=out_specs,
        out_shape=out_shape,
        input_output_aliases=aliases,
        scratch_shapes=[pltpu.VMEM((n, SSM_STATE, C), BF16), pltpu.VMEM((S, SSM_GW), F32),
                        pltpu.VMEM((2, SSM_STATE, SSM_GW), F32)],
        compiler_params=_cparams(("arbitrary", "arbitrary")),
        name="ssd_latent" if has_state else "ssd_context",
    )(*args)


def _pack_w_in(w_in):
    cols = [w_in[:, :, _IN_OFF[n]:_IN_OFF[n] + _IN_SIZE[n]] for n in PACK_ORDER + SMALL_ORDER]
    used = sum(_IN_SIZE[n] for n in SMALL_ORDER)
    cols.append(jnp.zeros((DEPTH, D_MODEL, SMALL_W - used), w_in.dtype))
    return jnp.concatenate(cols, axis=-1).astype(BF16)


def kernel(x_prompt, x_sample, cache_mla_ckv, cache_mla_krope, state_gla_fwd, state_gla_bwd,
           state_ssm_fwd, state_ssm_bwd, c, c_ctx, w_mod, b_mod, norm_w, w_in, q_a_norm, w_q_b,
           kv_a_norm, w_kv_b, gla_w_gate2, gla_b_gate, gla_norm, ssm_conv_w, ssm_conv_b,
           ssm_dt_bias, ssm_a_log, ssm_d, ssm_norm, w_br_mla, w_br_gla, w_br_ssm, w_out, final_norm):
    nb_p, s_p, _ = x_prompt.shape
    nb_l, s_l, _ = x_sample.shape
    n_p = nb_p * s_p
    n_l = nb_l * s_l
    n_tok = n_p + n_l
    assert s_p == SEQ_BLOCK and s_l % SEQ_BLOCK == 0 and n_p % s_l == 0
    t_c = cache_mla_ckv.shape[2]
    assert t_c == SEQ_BLOCK

    x = jnp.concatenate([x_prompt.reshape(n_p, D_MODEL), x_sample.reshape(n_l, D_MODEL)], axis=0)

    w_in_p = _pack_w_in(w_in)
    wq = w_q_b.reshape(DEPTH, Q_LORA, MLA_HEADS, QK_NOPE + QK_ROPE)
    wq_nope = wq[..., :QK_NOPE].reshape(DEPTH, Q_LORA, MLA_HEADS * QK_NOPE)
    wq_rope = wq[..., QK_NOPE:].reshape(DEPTH, Q_LORA, MLA_HEADS * QK_ROPE)
    tables = _rope_tables(s_l)
    p_ssm = jnp.stack([ssm_dt_bias, ssm_a_log], axis=2).reshape(DEPTH, 2, 2, SSM_GROUPS, SSM_HPG)
    p_col = p_ssm[:, :, :, :, None, :]
    p_row = p_ssm[:, :, :, :, :, None]
    d_lane = jnp.repeat(ssm_d, SSM_HEADDIM, axis=-1).reshape(DEPTH, 1, SSM_WIDTH)
    blk_mod = np.concatenate([np.zeros(n_p // SEQ_BLOCK, np.int32),
                              1 + np.repeat(np.arange(nb_l, dtype=np.int32), s_l // SEQ_BLOCK)])
    cond = jnp.concatenate([c_ctx[None, :], c], axis=0)
    cond = _silu(cond)
    cond = jnp.concatenate([cond, jnp.zeros((8 - cond.shape[0] % 8, D_MODEL), F32)], axis=0)

    gla_f = gla_b = ssm_f = ssm_b = None
    ckv_l, kr_l = [], []
    for l in range(DEPTH):
        mod = _matmul(cond, w_mod, name="mm_mod", layer=l, n_out=3 * D_MODEL, tm=cond.shape[0], tn=1536, out_dtype=F32)
        mod = mod[:1 + nb_l] + b_mod[l][None, :]
        mod_rows = mod[blk_mod][:, None, :]
        gate_rows = mod_rows[:, :, 2 * D_MODEL:]

        h = _prenorm(x, norm_w[l], mod_rows)
        proj = _matmul(h, w_in_p, name="mm_in", layer=l, n_out=PACK_TOTAL, tm=512, tn=1024, out_dtype=F32)
        small = proj[:, PK["small"]:PK["small"] + SMALL_W]

        qn = _matmul(proj, wq_nope, name="mm_q_nope", x_col0=PK["q_a"], k=Q_LORA, layer=l, n_out=MLA_HEADS * QK_NOPE,
                     tm=512, tn=512, out_dtype=BF16, gain=q_a_norm[l])
        qr = _matmul(proj, wq_rope, name="mm_q_rope", x_col0=PK["q_a"], k=Q_LORA, layer=l, n_out=MLA_HEADS * QK_ROPE,
                     tm=512, tn=512, out_dtype=F32, gain=q_a_norm[l])
        ckv = _rmsnorm(proj, kv_a_norm[l], x_col0=PK["kv_a"], width=KV_LORA)
        ckv_all = jnp.concatenate([ckv, cache_mla_ckv[:, l].reshape(nb_l * t_c, KV_LORA)], axis=0)
        kv = _matmul(ckv_all, w_kv_b, name="mm_kv", layer=l, n_out=MLA_HEADS * (QK_NOPE + V_HEAD), tm=512, tn=512,
                     out_dtype=BF16)
        o_mla_p = _mla(qn, qr, proj, kv, n_batch=nb_p, S=s_p, row0=0, latent=False)
        o_mla_l = _mla(qn, qr, proj, kv, n_batch=nb_l, S=s_l, row0=n_p, latent=True,
                       kv_ctx_row0=n_tok, krope_ctx=cache_mla_krope[:, l], tables=tables)
        o_mla = jnp.concatenate([o_mla_p, o_mla_l], axis=0)
        ckv_l.append(ckv[:n_p].reshape(nb_p, s_p, KV_LORA))
        kr_l.append(small[:n_p, SM["k_r"]:SM["k_r"] + QK_ROPE].reshape(nb_p, s_p, QK_ROPE))

        o_gla_p, gla_f, gla_b = _gla(proj, gla_w_gate2, gla_b_gate, gla_norm, layer=l, n_batch=nb_p, S=s_p,
                                     row0=0, states_out=(gla_f, gla_b))
        (o_gla_l,) = _gla(proj, gla_w_gate2, gla_b_gate, gla_norm, layer=l, n_batch=nb_l, S=s_l, row0=n_p,
                          states_in=(state_gla_fwd, state_gla_bwd))
        o_gla = jnp.concatenate([o_gla_p, o_gla_l], axis=0)

        dt = small[:, SM["dt_f"]:SM["dt_f"] + 2 * SSM_HEADS].reshape(n_tok, 2, SSM_GROUPS, SSM_HPG)
        dt_col = jnp.transpose(dt, (1, 2, 0, 3))
        dt_row = jnp.transpose(dt.reshape(n_tok // SSM_CHUNK, SSM_CHUNK, 2, SSM_GROUPS, SSM_HPG),
                               (2, 3, 0, 4, 1))
        xbc_p = _conv(proj, ssm_conv_w, ssm_conv_b, layer=l, n_batch=nb_p, S=s_p, row0=0)
        xbc_l = _conv(proj, ssm_conv_w, ssm_conv_b, layer=l, n_batch=nb_l, S=s_l, row0=n_p)
        yz_p, ssm_f, ssm_b = _ssd(xbc_p, proj, dt_col, dt_row, p_col, p_row, d_lane, layer=l, n_batch=nb_p,
                                  S=s_p, row0=0, states_out=(ssm_f, ssm_b))
        (yz_l,) = _ssd(xbc_l, proj, dt_col, dt_row, p_col, p_row, d_lane, layer=l, n_batch=nb_l, S=s_l,
                       row0=n_p, states_in=(state_ssm_fwd, state_ssm_bwd))
        yz = jnp.concatenate([yz_p, yz_l], axis=0)

        mg = _matmul(o_mla, w_br_mla, name="mm_br_mla", layer=l, n_out=D_MODEL, tm=512, tn=512, out_dtype=F32,
                     ep="sig", m=proj, m_col0=PK["m_mla"])
        mg = _matmul(o_gla, w_br_gla, name="mm_br_gla", layer=l, n_out=D_MODEL, tm=512, tn=512, out_dtype=F32,
                     ep="sigadd", m=proj, m_col0=PK["m_gla"], prev=mg)
        mg = _matmul(yz, w_br_ssm, name="mm_br_ssm", layer=l, n_out=D_MODEL, tm=256, tn=512, out_dtype=BF16,
                     gain=ssm_norm[l], ep="sigadd", m=proj, m_col0=PK["m_ssm"], prev=mg)
        x = _matmul(mg, w_out, name="mm_out", layer=l, n_out=D_MODEL, tm=SEQ_BLOCK, tn=512, out_dtype=F32,
                    ep="resid", res=x, gate_rows=gate_rows)

    y = _rmsnorm(x, final_norm)
    y_prompt = y[:n_p].reshape(nb_p, s_p, D_MODEL)
    y_sample = y[n_p:].reshape(nb_l, s_l, D_MODEL)
    return (y_prompt, y_sample, jnp.stack(ckv_l, axis=1), jnp.stack(kr_l, axis=1),
            gla_f, gla_b, ssm_f, ssm_b)
```

```python
import functools

import jax
import jax.numpy as jnp
import numpy as np
from jax import lax
from jax.experimental import pallas as pl
from jax.experimental.pallas import tpu as pltpu

F32 = jnp.float32
BF16 = jnp.bfloat16

D_MODEL = 2048
DEPTH = 4
GRID_W = 64
EPS = 1e-6
MLA_HEADS = 16
QK_NOPE = 128
QK_ROPE = 64
V_HEAD = 128
Q_LORA = 512
KV_LORA = 256
MLA_WIDTH = MLA_HEADS * V_HEAD
ROPE_BASE = 10000.0
GLA_HEADS = 4
GLA_DK = 256
GLA_DV = 512
GLA_KEY = GLA_HEADS * GLA_DK
GLA_WIDTH = GLA_HEADS * GLA_DV
GLA_GATE_RANK = 16
GLA_GATE_NORM = 16.0
SSM_HEADS = 64
SSM_HEADDIM = 64
SSM_WIDTH = SSM_HEADS * SSM_HEADDIM
SSM_GROUPS = 8
SSM_HPG = SSM_HEADS // SSM_GROUPS
SSM_GW = SSM_HPG * SSM_HEADDIM
SSM_STATE = 128
SSM_CONV = 5
SSM_BC = SSM_GROUPS * SSM_STATE
SSM_CONV_CH = SSM_WIDTH + 2 * SSM_BC

IN_SIZES = (Q_LORA, KV_LORA, QK_ROPE, MLA_WIDTH,
            GLA_KEY, GLA_KEY, GLA_WIDTH, GLA_GATE_RANK, GLA_GATE_RANK, GLA_WIDTH,
            SSM_WIDTH, SSM_CONV_CH, SSM_HEADS, SSM_HEADS,
            D_MODEL, D_MODEL, D_MODEL)
IN_NAMES = ("q_a", "kv_a", "k_r", "g_mla", "q_l", "k_l", "v_l", "ga_f", "ga_b", "g_gla",
            "z", "xbc", "dt_f", "dt_b", "m_mla", "m_gla", "m_ssm")
_IN_OFF = dict(zip(IN_NAMES, np.cumsum((0,) + IN_SIZES[:-1]).tolist()))
_IN_SIZE = dict(zip(IN_NAMES, IN_SIZES))

REGIONS = {
    "a": ("g_mla", "q_l", "k_l", "v_l"),
    "b": ("g_gla", "z", "xbc"),
    "c": ("m_mla", "m_gla", "m_ssm"),
    "s": ("q_a", "kv_a", "k_r", "ga_f", "ga_b", "dt_f", "dt_b"),
}
IN_TN = 1024
SEG = {}
REGION_W = {}
for _r, _names in REGIONS.items():
    _o = 0
    for _n in _names:
        SEG[_n] = (_r, _o)
        _o += _IN_SIZE[_n]
    REGION_W[_r] = -(-_o // IN_TN) * IN_TN
    if _r != "s":
        assert _o % IN_TN == 0
        assert all(_IN_OFF[_names[i]] + _IN_SIZE[_names[i]] == _IN_OFF[_names[i + 1]]
                   for i in range(len(_names) - 1))
SMALL_W = 256
SMALL_COL0 = SEG["k_r"][1]
assert SMALL_COL0 % SMALL_W == 0 and SEG["dt_b"][1] + _IN_SIZE["dt_b"] <= SMALL_COL0 + SMALL_W
SM = {n: SEG[n][1] - SMALL_COL0 for n in ("k_r", "ga_f", "ga_b", "dt_f", "dt_b")}

LANES = 128
SEQ_BLOCK = 256
GLA_CHUNK = 128
SSM_CHUNK = 128
V7X_VMEM_LIMIT = 56 * 1024 * 1024


def _cparams(sem):
    return pltpu.CompilerParams(dimension_semantics=sem, vmem_limit_bytes=V7X_VMEM_LIMIT)


def _silu(x):
    return x * (1.0 / (1.0 + jnp.exp(-x)))


def _sigmoid(x):
    return 1.0 / (1.0 + jnp.exp(-x))


def _softplus(x):
    return jnp.maximum(x, 0.0) + jnp.log(1.0 + jnp.exp(-jnp.abs(x)))


def _dot(a, b):
    return jnp.dot(a, b, preferred_element_type=F32)


def _dot_nt(a, b):
    return lax.dot_general(a, b, (((1,), (1,)), ((), ())), preferred_element_type=F32)


def _split3(x):
    hi = x.astype(BF16)
    r1 = x - hi.astype(F32)
    mid = r1.astype(BF16)
    lo = (r1 - mid.astype(F32)).astype(BF16)
    return hi, mid, lo


def _tri_left(tri, x):
    hi, mid, lo = _split3(x)
    return _dot(tri, hi) + _dot(tri, mid) + _dot(tri, lo)


def _tri_right(x, tri):
    hi, mid, lo = _split3(x)
    return _dot(hi, tri) + _dot(mid, tri) + _dot(lo, tri)


def _drop_alias_refs(refs, n_in, n_alias):
    refs = list(refs)
    return refs[:n_in - n_alias] + refs[n_in:]


def _mm_kernel(*refs, rms, ep, cast_w, w_shift):
    it = iter(refs)
    x_ref = next(it)
    w_ref = next(it)
    w_next_ref = next(it) if w_shift else None
    gain_ref = next(it) if rms else None
    m_ref = next(it) if ep in ("sig", "sigadd") else None
    prev_ref = next(it) if ep == "sigadd" else None
    res_ref = next(it) if ep == "resid" else None
    gate_ref = next(it) if ep == "resid" else None
    o_ref = next(it)
    wbf_ref = next(it) if cast_w else None

    if cast_w:
        @pl.when(pl.program_id(1) == 0)
        def _():
            if w_shift:
                n_grp = w_ref.shape[1] // LANES
                skip, rem = divmod(w_shift, LANES)

                def group(g):
                    ref = w_ref if g < n_grp else w_next_ref
                    g = g % n_grp
                    return ref[:, g * LANES:(g + 1) * LANES]

                for t in range(n_grp):
                    cols = group(t + skip)
                    if rem:
                        cols = jnp.concatenate([cols[:, rem:], group(t + skip + 1)[:, :rem]], axis=1)
                    wbf_ref[:, t * LANES:(t + 1) * LANES] = cols.astype(BF16)
            else:
                wbf_ref[...] = w_ref[...].astype(BF16)
        w = wbf_ref[...]
    else:
        w = w_ref[...]
    x = x_ref[...]
    if rms:
        xf = x.astype(F32)
        ms = jnp.mean(xf * xf, axis=-1, keepdims=True)
        x = xf * lax.rsqrt(ms + EPS) * gain_ref[...]
    acc = _dot(x.astype(BF16), w)
    if ep == "sig":
        acc = _sigmoid(m_ref[...]) * acc
    elif ep == "sigadd":
        acc = prev_ref[...].astype(F32) + _sigmoid(m_ref[...]) * acc
    elif ep == "resid":
        acc = res_ref[...] + gate_ref[...] * acc
    o_ref[...] = acc.astype(o_ref.dtype)


def _matmul(x, w, *, name, layer=None, n_out, tm, tn, out_dtype, w_col0=0, gain=None, x_col0=0, k=None,
            ep=None, m=None, m_col0=0, prev=None, res=None, gate_rows=None):
    M = x.shape[0]
    K = x.shape[1] if k is None else k
    assert M % tm == 0 and n_out % tn == 0 and m_col0 % tn == 0 and x_col0 % K == 0
    xj = x_col0 // K
    cast_w = w.dtype != BF16
    rms = gain is not None
    wj, w_shift = divmod(w_col0, tn)
    assert cast_w or not w_shift
    mj = m_col0 // tn

    def w_spec(extra):
        if layer is None:
            return pl.BlockSpec((K, tn), lambda j, i: (0, j + wj + extra))
        return pl.BlockSpec((None, K, tn), lambda j, i: (layer, 0, j + wj + extra))

    in_specs = [pl.BlockSpec((tm, K), lambda j, i: (i, xj)), w_spec(0)]
    args = [x, w]
    if w_shift:
        in_specs.append(w_spec(1))
        args.append(w)
    if rms:
        in_specs.append(pl.BlockSpec((1, K), lambda j, i: (0, 0)))
        args.append(gain.reshape(1, K))
    if ep in ("sig", "sigadd"):
        in_specs.append(pl.BlockSpec((tm, tn), lambda j, i: (i, j + mj)))
        args.append(m)
    if ep == "sigadd":
        in_specs.append(pl.BlockSpec((tm, tn), lambda j, i: (i, j)))
        args.append(prev)
    if ep == "resid":
        assert tm == SEQ_BLOCK
        in_specs.append(pl.BlockSpec((tm, tn), lambda j, i: (i, j)))
        args.append(res)
        in_specs.append(pl.BlockSpec((None, 1, tn), lambda j, i: (i, 0, j)))
        args.append(gate_rows)
    scratch = [pltpu.VMEM((K, tn), BF16)] if cast_w else []
    return pl.pallas_call(
        functools.partial(_mm_kernel, rms=rms, ep=ep, cast_w=cast_w, w_shift=w_shift),
        grid=(n_out // tn, M // tm),
        in_specs=in_specs,
        out_specs=pl.BlockSpec((tm, tn), lambda j, i: (i, j)),
        out_shape=jax.ShapeDtypeStruct((M, n_out), out_dtype),
        scratch_shapes=scratch,
        compiler_params=_cparams(("arbitrary", "arbitrary")),
        name=name,
    )(*args)


def _prenorm_kernel(x_ref, nw_ref, mod_ref, o_ref):
    x = x_ref[...]
    ms = jnp.mean(x * x, axis=-1, keepdims=True)
    y = x * lax.rsqrt(ms + EPS) * nw_ref[...]
    shift = mod_ref[:, 0:D_MODEL]
    scale = mod_ref[:, D_MODEL:2 * D_MODEL]
    o_ref[...] = (y * (1.0 + scale) + shift).astype(o_ref.dtype)


def _prenorm(x, norm_w, mod_rows):
    M = x.shape[0]
    return pl.pallas_call(
        _prenorm_kernel,
        grid=(M // SEQ_BLOCK,),
        in_specs=[pl.BlockSpec((SEQ_BLOCK, D_MODEL), lambda i: (i, 0)),
                  pl.BlockSpec((1, D_MODEL), lambda i: (0, 0)),
                  pl.BlockSpec((None, 1, 3 * D_MODEL), lambda i: (i, 0, 0))],
        out_specs=pl.BlockSpec((SEQ_BLOCK, D_MODEL), lambda i: (i, 0)),
        out_shape=jax.ShapeDtypeStruct((M, D_MODEL), BF16),
        compiler_params=_cparams(("arbitrary",)),
        name="prenorm",
    )(x, norm_w.reshape(1, D_MODEL), mod_rows)


def _rmsnorm_kernel(x_ref, w_ref, o_ref):
    x = x_ref[...]
    ms = jnp.mean(x * x, axis=-1, keepdims=True)
    o_ref[...] = (x * lax.rsqrt(ms + EPS) * w_ref[...]).astype(o_ref.dtype)


def _rmsnorm(x, w, *, x_col0=0, width=None, tm=512):
    M = x.shape[0]
    width = x.shape[1] if width is None else width
    assert x_col0 % width == 0 and M % tm == 0
    cj = x_col0 // width
    return pl.pallas_call(
        _rmsnorm_kernel,
        grid=(M // tm,),
        in_specs=[pl.BlockSpec((tm, width), lambda i: (i, cj)),
                  pl.BlockSpec((1, width), lambda i: (0, 0))],
        out_specs=pl.BlockSpec((tm, width), lambda i: (i, 0)),
        out_shape=jax.ShapeDtypeStruct((M, width), F32),
        compiler_params=_cparams(("arbitrary",)),
        name="rmsnorm",
    )(x, w.reshape(1, width))


def _swap_pairs(x):
    n = x.shape[-1]
    nxt = pltpu.roll(x, n - 1, axis=1)
    prv = pltpu.roll(x, 1, axis=1)
    lane = lax.broadcasted_iota(jnp.int32, x.shape, 1)
    return jnp.where((lane & 1) == 0, nxt, prv)


def _mla_kernel(*refs, latent, n_alias):
    refs = _drop_alias_refs(refs, (11 if latent else 5) + n_alias, n_alias)
    it = iter(refs)
    qn_ref = next(it)
    qr_ref = next(it)
    g_ref = next(it)
    kv_ref = next(it)
    sm_ref = next(it)
    if latent:
        kvc_ref = next(it)
        krc_ref = next(it)
        cq_ref = next(it)
        sq_ref = next(it)
        ck_ref = next(it)
        sk_ref = next(it)
    o_ref = next(it)

    scale = float(QK_NOPE + QK_ROPE) ** -0.5
    qr = qr_ref[...]
    sm = sm_ref[...]
    if latent:
        qr = qr * cq_ref[...] + _swap_pairs(qr) * sq_ref[...]
        sm = sm * ck_ref[...] + _swap_pairs(sm) * sk_ref[...]
    kr = sm[:, SM["k_r"]:SM["k_r"] + QK_ROPE].astype(BF16)
    if latent:
        krc = krc_ref[...].astype(BF16)

    for h in range(MLA_HEADS):
        qn_h = qn_ref[:, h * QK_NOPE:(h + 1) * QK_NOPE]
        qr_h = qr[:, h * QK_ROPE:(h + 1) * QK_ROPE].astype(BF16)
        c0 = h * (QK_NOPE + V_HEAD)
        kn_h = kv_ref[:, c0:c0 + QK_NOPE]
        v_h = kv_ref[:, c0 + QK_NOPE:c0 + QK_NOPE + V_HEAD]
        s_own = (_dot_nt(qn_h, kn_h) + _dot_nt(qr_h, kr)) * scale
        mx = jnp.max(s_own, axis=-1, keepdims=True)
        if latent:
            knc_h = kvc_ref[:, c0:c0 + QK_NOPE]
            vc_h = kvc_ref[:, c0 + QK_NOPE:c0 + QK_NOPE + V_HEAD]
            s_ctx = (_dot_nt(qn_h, knc_h) + _dot_nt(qr_h, krc)) * scale
            mx = jnp.maximum(mx, jnp.max(s_ctx, axis=-1, keepdims=True))
            p_ctx = jnp.exp(s_ctx - mx)
        p_own = jnp.exp(s_own - mx)
        den = jnp.sum(p_own, axis=-1, keepdims=True)
        if latent:
            den = den + jnp.sum(p_ctx, axis=-1, keepdims=True)
        inv = 1.0 / den
        o_h = _dot((p_own * inv).astype(BF16), v_h)
        if latent:
            o_h = o_h + _dot((p_ctx * inv).astype(BF16), vc_h)
        g_h = g_ref[:, h * V_HEAD:(h + 1) * V_HEAD]
        o_ref[:, h * V_HEAD:(h + 1) * V_HEAD] = (o_h * _silu(g_h)).astype(o_ref.dtype)


def _mla(qn, qr, P, kv, *, n_tok, n_batch, S, row0, latent, kv_ctx_row0=0, krope_ctx=None, tables=None,
         out_prev=None):
    tq = min(S, 256)
    nq = S // tq
    rb = row0 // tq
    sb = row0 // S
    HW = MLA_HEADS * (QK_NOPE + V_HEAD)
    g_reg, g_col = SEG["g_mla"]
    gj = g_col // MLA_WIDTH
    smj = SMALL_COL0 // SMALL_W
    in_specs = [
        pl.BlockSpec((tq, MLA_HEADS * QK_NOPE), lambda b, i: (rb + b * nq + i, 0)),
        pl.BlockSpec((tq, MLA_HEADS * QK_ROPE), lambda b, i: (rb + b * nq + i, 0)),
        pl.BlockSpec((tq, MLA_WIDTH), lambda b, i: (rb + b * nq + i, gj)),
        pl.BlockSpec((S, HW), lambda b, i: (sb + b, 0)),
        pl.BlockSpec((S, SMALL_W), lambda b, i: (sb + b, smj)),
    ]
    args = [qn, qr, P[g_reg], kv, P["s"]]
    if latent:
        Tc = krope_ctx.shape[1]
        cb = kv_ctx_row0 // Tc
        cos_q, sin_q, cos_k, sin_k = tables
        in_specs += [
            pl.BlockSpec((Tc, HW), lambda b, i: (cb + b, 0)),
            pl.BlockSpec((None, Tc, QK_ROPE), lambda b, i: (b, 0, 0)),
            pl.BlockSpec((tq, MLA_HEADS * QK_ROPE), lambda b, i: (i, 0)),
            pl.BlockSpec((tq, MLA_HEADS * QK_ROPE), lambda b, i: (i, 0)),
            pl.BlockSpec((S, SMALL_W), lambda b, i: (0, 0)),
            pl.BlockSpec((S, SMALL_W), lambda b, i: (0, 0)),
        ]
        args += [kv, krope_ctx, cos_q, sin_q, cos_k, sin_k]
    aliases = {}
    if out_prev is not None:
        aliases[len(args)] = 0
        in_specs.append(pl.BlockSpec(memory_space=pl.ANY))
        args.append(out_prev)
    return pl.pallas_call(
        functools.partial(_mla_kernel, latent=latent, n_alias=len(aliases)),
        grid=(n_batch, nq),
        in_specs=in_specs,
        out_specs=pl.BlockSpec((tq, MLA_WIDTH), lambda b, i: (rb + b * nq + i, 0)),
        out_shape=jax.ShapeDtypeStruct((n_tok, MLA_WIDTH), BF16),
        input_output_aliases=aliases,
        compiler_params=_cparams(("arbitrary", "arbitrary")),
        name="mla_latent" if latent else "mla_context",
    )(*args)


def _rope_tables(S):
    rows = jnp.repeat(jnp.arange(S // GRID_W, dtype=F32), GRID_W)
    cols = jnp.tile(jnp.arange(GRID_W, dtype=F32), S // GRID_W)
    n_freq = QK_ROPE // 4
    inv = ROPE_BASE ** (-jnp.arange(n_freq, dtype=F32) / n_freq)
    ang = jnp.concatenate([rows[:, None] * inv, cols[:, None] * inv], axis=-1)
    cos = jnp.repeat(jnp.cos(ang), 2, axis=-1)
    sin = jnp.repeat(jnp.sin(ang), 2, axis=-1)
    sign = jnp.tile(jnp.array([-1.0, 1.0], F32), QK_ROPE // 2)
    sin = sin * sign
    cos_q = jnp.tile(cos, (1, MLA_HEADS))
    sin_q = jnp.tile(sin, (1, MLA_HEADS))
    lo, hi = SM["k_r"], SMALL_W - SM["k_r"] - QK_ROPE
    cos_k = jnp.concatenate([jnp.ones((S, lo), F32), cos, jnp.ones((S, hi), F32)], axis=-1)
    sin_k = jnp.concatenate([jnp.zeros((S, lo), F32), sin, jnp.zeros((S, hi), F32)], axis=-1)
    return cos_q, sin_q, cos_k, sin_k


def _gla_kernel(*refs, S, C, has_state, emit_state, n_alias):
    refs = _drop_alias_refs(refs, 8 + (2 if has_state else 0) + n_alias, n_alias)
    it = iter(refs)
    q_ref = next(it)
    k_ref = next(it)
    v_ref = next(it)
    sm_ref = next(it)
    gg_ref = next(it)
    w2_ref = next(it)
    b2_ref = next(it)
    nw_ref = next(it)
    s0_refs = (next(it), next(it)) if has_state else None
    o_ref = next(it)
    so_refs = (next(it), next(it)) if emit_state else None
    vt_ref = next(it)
    la_ref = next(it)
    acc_ref = next(it)
    st_ref = next(it)

    n = S // C
    scale = float(GLA_DK) ** -0.5
    sm = sm_ref[...]
    for d, name in enumerate(("ga_f", "ga_b")):
        ga = sm[:, SM[name]:SM[name] + GLA_GATE_RANK].astype(BF16)
        xg = _dot(ga, w2_ref[d].astype(BF16)) + b2_ref[d]
        la_ref[d] = -_softplus(-xg) * (1.0 / GLA_GATE_NORM)
    for c in range(n):
        vt_ref[c] = v_ref[c * C:(c + 1) * C, :].T.astype(BF16)
    for d in range(2):
        if has_state:
            st_ref[d] = s0_refs[d][...].T
        else:
            st_ref[d] = jnp.zeros((GLA_DV, GLA_DK), F32)

    row = lax.broadcasted_iota(jnp.int32, (C, C), 0)
    col = lax.broadcasted_iota(jnp.int32, (C, C), 1)
    lower = row >= col
    upper = row <= col
    tri = (jnp.where(lower, 1.0, 0.0).astype(BF16), jnp.where(upper, 1.0, 0.0).astype(BF16))
    mid = C // 2

    def chunk(c, d):
        r0 = pl.multiple_of(c * C, C)
        q = q_ref[pl.ds(r0, C), :] * scale
        k = k_ref[pl.ds(r0, C), :]
        v = v_ref[pl.ds(r0, C), :].astype(BF16)
        g = la_ref[d, pl.ds(r0, C), :]
        b = _tri_left(tri[d], g)
        tot = b[C - 1:C, :] if d == 0 else b[0:1, :]
        bm = b[mid:mid + 1, :]
        st = st_ref[d]
        o = _dot_nt((q * jnp.exp(b)).astype(BF16), st.astype(BF16))
        qa = (q * jnp.exp(b - bm)).astype(BF16)
        ka = (k * jnp.exp(bm - b)).astype(BF16)
        att = _dot_nt(qa, ka)
        att = jnp.where(lower if d == 0 else upper, att, 0.0)
        o = o + _dot(att.astype(BF16), v)
        kd = (k * jnp.exp(tot - b)).astype(BF16)
        st_ref[d] = st * jnp.exp(tot) + _dot(vt_ref[c], kd)
        acc_ref[d, pl.ds(r0, C), :] = o

    def both(i, carry):
        chunk(i, 0)
        chunk(n - 1 - i, 1)
        return carry

    lax.fori_loop(0, n, both, 0)

    o = acc_ref[0] + acc_ref[1]
    ms = jnp.mean(o * o, axis=-1, keepdims=True)
    y = o * lax.rsqrt(ms + EPS) * nw_ref[...]
    o_ref[...] = (y * _silu(gg_ref[...])).astype(o_ref.dtype)
    if emit_state:
        for d in range(2):
            so_refs[d][...] = st_ref[d].T


def _gla(P, w2, b2, nw, *, layer, n_tok, n_batch, S, row0, states_in=None, states_out=None, out_prev=None):
    C = GLA_CHUNK
    sb = row0 // S
    col = {n: SEG[n][1] for n in ("q_l", "k_l", "v_l", "g_gla")}
    qj, kj, vj, gj = col["q_l"] // GLA_DK, col["k_l"] // GLA_DK, col["v_l"] // GLA_DV, col["g_gla"] // GLA_DV
    smj = SMALL_COL0 // SMALL_W
    in_specs = [
        pl.BlockSpec((S, GLA_DK), lambda b, h: (sb + b, qj + h)),
        pl.BlockSpec((S, GLA_DK), lambda b, h: (sb + b, kj + h)),
        pl.BlockSpec((S, GLA_DV), lambda b, h: (sb + b, vj + h)),
        pl.BlockSpec((S, SMALL_W), lambda b, h: (sb + b, smj)),
        pl.BlockSpec((S, GLA_DV), lambda b, h: (sb + b, gj + h)),
        pl.BlockSpec((None, 2, GLA_GATE_RANK, GLA_DK), lambda b, h: (layer, 0, 0, h)),
        pl.BlockSpec((None, 2, 1, GLA_DK), lambda b, h: (layer, 0, 0, h)),
        pl.BlockSpec((None, 1, GLA_DV), lambda b, h: (layer, 0, 0)),
    ]
    args = [P[SEG["q_l"][0]], P[SEG["k_l"][0]], P[SEG["v_l"][0]], P["s"], P[SEG["g_gla"][0]],
            w2, b2.reshape(DEPTH, 2, 1, GLA_KEY), nw.reshape(DEPTH, 1, GLA_DV)]
    has_state = states_in is not None
    st_block = (None, None, None, GLA_DK, GLA_DV)
    if has_state:
        for s in states_in:
            in_specs.append(pl.BlockSpec(st_block, lambda b, h: (b, layer, h, 0, 0)))
            args.append(s)
    out_specs = [pl.BlockSpec((S, GLA_DV), lambda b, h: (sb + b, h))]
    out_shape = [jax.ShapeDtypeStruct((n_tok, GLA_WIDTH), BF16)]
    aliased = [out_prev]
    emit_state = states_out is not None
    if emit_state:
        for s in states_out:
            out_specs.append(pl.BlockSpec(st_block, lambda b, h: (b, layer, h, 0, 0)))
            out_shape.append(jax.ShapeDtypeStruct((n_batch, DEPTH, GLA_HEADS, GLA_DK, GLA_DV), F32))
            aliased.append(s)
    aliases = {}
    for oi, buf in enumerate(aliased):
        if buf is not None:
            aliases[len(args)] = oi
            in_specs.append(pl.BlockSpec(memory_space=pl.ANY))
            args.append(buf)
    n = S // C
    return pl.pallas_call(
        functools.partial(_gla_kernel, S=S, C=C, has_state=has_state, emit_state=emit_state,
                          n_alias=len(aliases)),
        grid=(n_batch, GLA_HEADS),
        in_specs=in_specs,
        out_specs=out_specs,
        out_shape=out_shape,
        input_output_aliases=aliases,
        scratch_shapes=[pltpu.VMEM((n, GLA_DV, C), BF16), pltpu.VMEM((2, S, GLA_DK), F32),
                        pltpu.VMEM((2, S, GLA_DV), F32), pltpu.VMEM((2, GLA_DV, GLA_DK), F32)],
        compiler_params=_cparams(("arbitrary", "arbitrary")),
        name="gla_latent" if has_state else "gla_context",
    )(*args)


CONV_TW = 512
CONV_PAD = 8


def _conv_kernel(*refs, S, n_alias):
    x_ref, w_ref, b_ref, o_ref, xp_ref = _drop_alias_refs(refs, 3 + n_alias, n_alias)
    left = (SSM_CONV - 1) // 2
    xp_ref[0:CONV_PAD, :] = jnp.zeros((CONV_PAD, CONV_TW), F32)
    xp_ref[CONV_PAD + S:CONV_PAD + S + CONV_PAD, :] = jnp.zeros((CONV_PAD, CONV_TW), F32)
    xp_ref[CONV_PAD:CONV_PAD + S, :] = x_ref[...]
    acc = jnp.zeros((S, CONV_TW), F32) + b_ref[...]
    for j in range(SSM_CONV):
        acc = acc + xp_ref[CONV_PAD - left + j:CONV_PAD - left + j + S, :] * w_ref[j:j + 1, :]
    o_ref[...] = _silu(acc)


def _conv(P, cw, cb, *, layer, n_tok, n_batch, S, row0, out_prev=None):
    sb = row0 // S
    x_reg, x_col = SEG["xbc"]
    xj = x_col // CONV_TW
    in_specs = [pl.BlockSpec((S, CONV_TW), lambda b, j: (sb + b, xj + j)),
                pl.BlockSpec((None, SSM_CONV, CONV_TW), lambda b, j: (layer, 0, j)),
                pl.BlockSpec((None, 1, CONV_TW), lambda b, j: (layer, 0, j))]
    args = [P[x_reg], cw, cb.reshape(DEPTH, 1, SSM_CONV_CH)]
    aliases = {}
    if out_prev is not None:
        aliases[len(args)] = 0
        in_specs.append(pl.BlockSpec(memory_space=pl.ANY))
        args.append(out_prev)
    return pl.pallas_call(
        functools.partial(_conv_kernel, S=S, n_alias=len(aliases)),
        grid=(n_batch, SSM_CONV_CH // CONV_TW),
        in_specs=in_specs,
        out_specs=pl.BlockSpec((S, CONV_TW), lambda b, j: (sb + b, j)),
        out_shape=jax.ShapeDtypeStruct((n_tok, SSM_CONV_CH), F32),
        input_output_aliases=aliases,
        scratch_shapes=[pltpu.VMEM((S + 2 * CONV_PAD, CONV_TW), F32)],
        compiler_params=_cparams(("arbitrary", "arbitrary")),
        name="ssm_conv",
    )(*args)


def _expand_heads(cols, width):
    C = cols.shape[0]
    lane_head = lax.broadcasted_iota(jnp.int32, (C, width), 1) // SSM_HEADDIM
    out = jnp.zeros((C, width), F32)
    for j in range(SSM_HPG):
        out = jnp.where(lane_head == j, cols[:, j:j + 1], out)
    return out


def _ssd_kernel(*refs, S, C, has_state, emit_state, n_alias):
    refs = _drop_alias_refs(refs, 9 + (2 if has_state else 0) + n_alias, n_alias)
    it = iter(refs)
    x_ref = next(it)
    b_ref = next(it)
    c_ref = next(it)
    z_ref = next(it)
    dtc_ref = next(it)
    dtr_ref = next(it)
    pc_ref = next(it)
    pr_ref = next(it)
    d_ref = next(it)
    s0_refs = (next(it), next(it)) if has_state else None
    y_ref = next(it)
    so_refs = (next(it), next(it)) if emit_state else None
    bt_ref = next(it)
    acc_ref = next(it)
    st_ref = next(it)

    n = S // C
    for c in range(n):
        bt_ref[c] = b_ref[c * C:(c + 1) * C, :].T.astype(BF16)
    for d in range(2):
        if has_state:
            s0 = jnp.concatenate([s0_refs[d][j] for j in range(SSM_HPG)], axis=0)
            st_ref[d] = s0.T
        else:
            st_ref[d] = jnp.zeros((SSM_STATE, SSM_GW), F32)

    row = lax.broadcasted_iota(jnp.int32, (C, C), 0)
    col = lax.broadcasted_iota(jnp.int32, (C, C), 1)
    lower = row >= col
    upper = row <= col
    tri_l = jnp.where(lower, 1.0, 0.0).astype(BF16)
    tri_u = jnp.where(upper, 1.0, 0.0).astype(BF16)

    def chunk(c, d):
        r0 = pl.multiple_of(c * C, C)
        x = x_ref[pl.ds(r0, C), :]
        bm = b_ref[pl.ds(r0, C), :].astype(BF16)
        cm = c_ref[pl.ds(r0, C), :].astype(BF16)
        a_c = -jnp.exp(pc_ref[d, 1])
        a_r = -jnp.exp(pr_ref[d, 1])
        dt_c = _softplus(dtc_ref[d, pl.ds(r0, C), :] + pc_ref[d, 0])
        dt_r = _softplus(dtr_ref[d, c] + pr_ref[d, 0])
        if d == 0:
            cum_c = _tri_left(tri_l, dt_c * a_c)
            cum_r = _tri_right(dt_r * a_r, tri_u)
        else:
            cum_c = _tri_left(tri_u, dt_c * a_c)
            cum_r = _tri_right(dt_r * a_r, tri_l)
        mask = lower if d == 0 else upper
        cum_e = _expand_heads(cum_c, SSM_GW)
        dt_e = _expand_heads(dt_c, SSM_GW)
        xg = x * dt_e
        st = st_ref[d]
        y = _dot(cm, st.astype(BF16)) * jnp.exp(cum_e)
        cb = _dot_nt(cm, bm)
        parts = []
        for j in range(SSM_HPG):
            dec = jnp.where(mask, jnp.exp(cum_c[:, j:j + 1] - cum_r[j:j + 1, :]), 0.0)
            parts.append(_dot((cb * dec).astype(BF16), xg[:, j * SSM_HEADDIM:(j + 1) * SSM_HEADDIM].astype(BF16)))
        y = y + jnp.concatenate(parts, axis=-1)
        last = cum_e[C - 1:C, :] if d == 0 else cum_e[0:1, :]
        xw = (xg * jnp.exp(last - cum_e)).astype(BF16)
        st_ref[d] = st * jnp.exp(last) + _dot(bt_ref[c], xw)
        acc_ref[d, pl.ds(r0, C), :] = y

    def both(i, carry):
        chunk(i, 0)
        chunk(n - 1 - i, 1)
        return carry

    lax.fori_loop(0, n, both, 0)

    y = acc_ref[0] + acc_ref[1] + x_ref[...] * d_ref[...]
    y_ref[...] = y * _silu(z_ref[...])
    if emit_state:
        for d in range(2):
            s_t = st_ref[d].T
            for j in range(SSM_HPG):
                so_refs[d][j] = s_t[j * SSM_HEADDIM:(j + 1) * SSM_HEADDIM, :]


def _ssd(xbc, P, dt_col, dt_row, p_col, p_row, d_lane, *, layer, n_tok, n_batch, S, row0,
         states_in=None, states_out=None, out_prev=None):
    C = SSM_CHUNK
    sb = row0 // S
    z_reg, z_col = SEG["z"]
    zj = z_col // SSM_GW
    bj = SSM_WIDTH // SSM_STATE
    cj = (SSM_WIDTH + SSM_BC) // SSM_STATE
    in_specs = [
        pl.BlockSpec((S, SSM_GW), lambda b, g: (sb + b, g)),
        pl.BlockSpec((S, SSM_STATE), lambda b, g: (sb + b, bj + g)),
        pl.BlockSpec((S, SSM_STATE), lambda b, g: (sb + b, cj + g)),
        pl.BlockSpec((S, SSM_GW), lambda b, g: (sb + b, zj + g)),
        pl.BlockSpec((2, None, S, SSM_HPG), lambda b, g: (0, g, sb + b, 0)),
        pl.BlockSpec((2, None, S // C, SSM_HPG, C), lambda b, g: (0, g, sb + b, 0, 0)),
        pl.BlockSpec((None, 2, 2, None, 1, SSM_HPG), lambda b, g: (layer, 0, 0, g, 0, 0)),
        pl.BlockSpec((None, 2, 2, None, SSM_HPG, 1), lambda b, g: (layer, 0, 0, g, 0, 0)),
        pl.BlockSpec((None, 1, SSM_GW), lambda b, g: (layer, 0, g)),
    ]
    args = [xbc, xbc, xbc, P[z_reg], dt_col, dt_row, p_col, p_row, d_lane]
    has_state = states_in is not None
    st_block = (None, None, SSM_HPG, SSM_HEADDIM, SSM_STATE)
    if has_state:
        for s in states_in:
            in_specs.append(pl.BlockSpec(st_block, lambda b, g: (b, layer, g, 0, 0)))
            args.append(s)
    out_specs = [pl.BlockSpec((S, SSM_GW), lambda b, g: (sb + b, g))]
    out_shape = [jax.ShapeDtypeStruct((n_tok, SSM_WIDTH), F32)]
    aliased = [out_prev]
    emit_state = states_out is not None
    if emit_state:
        for s in states_out:
            out_specs.append(pl.BlockSpec(st_block, lambda b, g: (b, layer, g, 0, 0)))
            out_shape.append(jax.ShapeDtypeStruct((n_batch, DEPTH, SSM_HEADS, SSM_HEADDIM, SSM_STATE), F32))
            aliased.append(s)
    aliases = {}
    for oi, buf in enumerate(aliased):
        if buf is not None:
            aliases[len(args)] = oi
            in_specs.append(pl.BlockSpec(memory_space=pl.ANY))
            args.append(buf)
    n = S // C
    return pl.pallas_call(
        functools.partial(_ssd_kernel, S=S, C=C, has_state=has_state, emit_state=emit_state,
                          n_alias=len(aliases)),
        grid=(n_batch, SSM_GROUPS),
        in_specs=in_specs,
        out_specs=out_specs,
        out_shape=out_shape,
        input_output_aliases=aliases,
        scratch_shapes=[pltpu.VMEM((n, SSM_STATE, C), BF16), pltpu.VMEM((2, S, SSM_GW), F32),
                        pltpu.VMEM((2, SSM_STATE, SSM_GW), F32)],
        compiler_params=_cparams(("arbitrary", "arbitrary")),
        name="ssd_latent" if has_state else "ssd_context",
    )(*args)


def _gather_small_w_in(w_in):
    names = REGIONS["s"]
    cols = [w_in[:, :, _IN_OFF[n]:_IN_OFF[n] + _IN_SIZE[n]] for n in names]
    used = sum(_IN_SIZE[n] for n in names)
    cols.append(jnp.zeros((DEPTH, D_MODEL, REGION_W["s"] - used), w_in.dtype))
    return jnp.concatenate(cols, axis=-1)


def kernel(x_prompt, x_sample, cache_mla_ckv, cache_mla_krope, state_gla_fwd, state_gla_bwd,
           state_ssm_fwd, state_ssm_bwd, c, c_ctx, w_mod, b_mod, norm_w, w_in, q_a_norm, w_q_b,
           kv_a_norm, w_kv_b, gla_w_gate2, gla_b_gate, gla_norm, ssm_conv_w, ssm_conv_b,
           ssm_dt_bias, ssm_a_log, ssm_d, ssm_norm, w_br_mla, w_br_gla, w_br_ssm, w_out, final_norm):
    nb_p, s_p, _ = x_prompt.shape
    nb_l, s_l, _ = x_sample.shape
    n_p = nb_p * s_p
    n_l = nb_l * s_l
    n_tok = n_p + n_l
    assert s_p == SEQ_BLOCK and s_l % SEQ_BLOCK == 0 and n_p % s_l == 0
    t_c = cache_mla_ckv.shape[2]
    assert t_c == SEQ_BLOCK

    x = jnp.concatenate([x_prompt.reshape(n_p, D_MODEL), x_sample.reshape(n_l, D_MODEL)], axis=0)

    w_in_s = _gather_small_w_in(w_in)
    wq = w_q_b.reshape(DEPTH, Q_LORA, MLA_HEADS, QK_NOPE + QK_ROPE)
    wq_nope = wq[..., :QK_NOPE].reshape(DEPTH, Q_LORA, MLA_HEADS * QK_NOPE)
    wq_rope = wq[..., QK_NOPE:].reshape(DEPTH, Q_LORA, MLA_HEADS * QK_ROPE)
    tables = _rope_tables(s_l)
    p_ssm = jnp.stack([ssm_dt_bias, ssm_a_log], axis=2).reshape(DEPTH, 2, 2, SSM_GROUPS, SSM_HPG)
    p_col = p_ssm[:, :, :, :, None, :]
    p_row = p_ssm[:, :, :, :, :, None]
    d_lane = jnp.repeat(ssm_d, SSM_HEADDIM, axis=-1).reshape(DEPTH, 1, SSM_WIDTH)
    blk_mod = np.concatenate([np.zeros(n_p // SEQ_BLOCK, np.int32),
                              1 + np.repeat(np.arange(nb_l, dtype=np.int32), s_l // SEQ_BLOCK)])
    cond = jnp.concatenate([c_ctx[None, :], c], axis=0)
    cond = _silu(cond)
    cond = jnp.concatenate([cond, jnp.zeros((8 - cond.shape[0] % 8, D_MODEL), F32)], axis=0)

    gla_f = gla_b = ssm_f = ssm_b = None
    ckv_l, kr_l = [], []
    for l in range(DEPTH):
        mod = _matmul(cond, w_mod, name="mm_mod", layer=l, n_out=3 * D_MODEL, tm=cond.shape[0], tn=1536,
                      out_dtype=F32)
        mod = mod[:1 + nb_l] + b_mod[l][None, :]
        mod_rows = mod[blk_mod][:, None, :]
        gate_rows = mod_rows[:, :, 2 * D_MODEL:]

        h = _prenorm(x, norm_w[l], mod_rows)
        P = {}
        for r in ("a", "b", "c"):
            first = REGIONS[r][0]
            P[r] = _matmul(h, w_in, name="mm_in_" + r, layer=l, w_col0=_IN_OFF[first], n_out=REGION_W[r],
                           tm=512, tn=IN_TN, out_dtype=F32)
        P["s"] = _matmul(h, w_in_s, name="mm_in_s", layer=l, n_out=REGION_W["s"], tm=512, tn=IN_TN,
                         out_dtype=F32)
        small = P["s"][:, SMALL_COL0:SMALL_COL0 + SMALL_W]

        qa_col, kva_col = SEG["q_a"][1], SEG["kv_a"][1]
        qn = _matmul(P["s"], wq_nope, name="mm_q_nope", x_col0=qa_col, k=Q_LORA, layer=l,
                     n_out=MLA_HEADS * QK_NOPE, tm=512, tn=512, out_dtype=BF16, gain=q_a_norm[l])
        qr = _matmul(P["s"], wq_rope, name="mm_q_rope", x_col0=qa_col, k=Q_LORA, layer=l,
                     n_out=MLA_HEADS * QK_ROPE, tm=512, tn=512, out_dtype=F32, gain=q_a_norm[l])
        ckv = _rmsnorm(P["s"], kv_a_norm[l], x_col0=kva_col, width=KV_LORA)
        ckv_all = jnp.concatenate([ckv, cache_mla_ckv[:, l].reshape(nb_l * t_c, KV_LORA)], axis=0)
        kv = _matmul(ckv_all, w_kv_b, name="mm_kv", layer=l, n_out=MLA_HEADS * (QK_NOPE + V_HEAD), tm=512,
                     tn=512, out_dtype=BF16)
        o_mla = _mla(qn, qr, P, kv, n_tok=n_tok, n_batch=nb_p, S=s_p, row0=0, latent=False)
        o_mla = _mla(qn, qr, P, kv, n_tok=n_tok, n_batch=nb_l, S=s_l, row0=n_p, latent=True,
                     kv_ctx_row0=n_tok, krope_ctx=cache_mla_krope[:, l], tables=tables, out_prev=o_mla)
        ckv_l.append(ckv[:n_p].reshape(nb_p, s_p, KV_LORA))
        kr_l.append(small[:n_p, SM["k_r"]:SM["k_r"] + QK_ROPE].reshape(nb_p, s_p, QK_ROPE))

        o_gla, gla_f, gla_b = _gla(P, gla_w_gate2, gla_b_gate, gla_norm, layer=l, n_tok=n_tok, n_batch=nb_p,
                                   S=s_p, row0=0, states_out=(gla_f, gla_b))
        (o_gla,) = _gla(P, gla_w_gate2, gla_b_gate, gla_norm, layer=l, n_tok=n_tok, n_batch=nb_l, S=s_l,
                        row0=n_p, states_in=(state_gla_fwd, state_gla_bwd), out_prev=o_gla)

        dt = small[:, SM["dt_f"]:SM["dt_f"] + 2 * SSM_HEADS].reshape(n_tok, 2, SSM_GROUPS, SSM_HPG)
        dt_col = jnp.transpose(dt, (1, 2, 0, 3))
        dt_row = jnp.transpose(dt.reshape(n_tok // SSM_CHUNK, SSM_CHUNK, 2, SSM_GROUPS, SSM_HPG),
                               (2, 3, 0, 4, 1))
        xbc = _conv(P, ssm_conv_w, ssm_conv_b, layer=l, n_tok=n_tok, n_batch=nb_p, S=s_p, row0=0)
        xbc = _conv(P, ssm_conv_w, ssm_conv_b, layer=l, n_tok=n_tok, n_batch=nb_l, S=s_l, row0=n_p,
                    out_prev=xbc)
        yz, ssm_f, ssm_b = _ssd(xbc, P, dt_col, dt_row, p_col, p_row, d_lane, layer=l, n_tok=n_tok,
                                n_batch=nb_p, S=s_p, row0=0, states_out=(ssm_f, ssm_b))
        (yz,) = _ssd(xbc, P, dt_col, dt_row, p_col, p_row, d_lane, layer=l, n_tok=n_tok, n_batch=nb_l,
                     S=s_l, row0=n_p, states_in=(state_ssm_fwd, state_ssm_bwd), out_prev=yz)

        mg = _matmul(o_mla, w_br_mla, name="mm_br_mla", layer=l, n_out=D_MODEL, tm=512, tn=512, out_dtype=F32,
                     ep="sig", m=P["c"], m_col0=SEG["m_mla"][1])
        mg = _matmul(o_gla, w_br_gla, name="mm_br_gla", layer=l, n_out=D_MODEL, tm=512, tn=512, out_dtype=F32,
                     ep="sigadd", m=P["c"], m_col0=SEG["m_gla"][1], prev=mg)
        mg = _matmul(yz, w_br_ssm, name="mm_br_ssm", layer=l, n_out=D_MODEL, tm=256, tn=512, out_dtype=BF16,
                     gain=ssm_norm[l], ep="sigadd", m=P["c"], m_col0=SEG["m_ssm"][1], prev=mg)
        x = _matmul(mg, w_out, name="mm_out", layer=l, n_out=D_MODEL, tm=SEQ_BLOCK, tn=512, out_dtype=F32,
                    ep="resid", res=x, gate_rows=gate_rows)

    y = _rmsnorm(x, final_norm)
    y_prompt = y[:n_p].reshape(nb_p, s_p, D_MODEL)
    y_sample = y[n_p:].reshape(nb_l, s_l, D_MODEL)
    return (y_prompt, y_sample, jnp.stack(ckv_l, axis=1), jnp.stack(kr_l, axis=1),
            gla_f, gla_b, ssm_f, ssm_b)
```

```python
import functools

import jax
import jax.numpy as jnp
import numpy as np
from jax import lax
from jax.experimental import pallas as pl
from jax.experimental.pallas import tpu as pltpu

F32 = jnp.float32
BF16 = jnp.bfloat16

D_MODEL = 2048
DEPTH = 4
GRID_W = 64
EPS = 1e-6
MLA_HEADS = 16
QK_NOPE = 128
QK_ROPE = 64
V_HEAD = 128
Q_LORA = 512
KV_LORA = 256
MLA_WIDTH = MLA_HEADS * V_HEAD
ROPE_BASE = 10000.0
GLA_HEADS = 4
GLA_DK = 256
GLA_DV = 512
GLA_KEY = GLA_HEADS * GLA_DK
GLA_WIDTH = GLA_HEADS * GLA_DV
GLA_GATE_RANK = 16
GLA_GATE_NORM = 16.0
SSM_HEADS = 64
SSM_HEADDIM = 64
SSM_WIDTH = SSM_HEADS * SSM_HEADDIM
SSM_GROUPS = 8
SSM_HPG = SSM_HEADS // SSM_GROUPS
SSM_GW = SSM_HPG * SSM_HEADDIM
SSM_STATE = 128
SSM_CONV = 5
SSM_BC = SSM_GROUPS * SSM_STATE
SSM_CONV_CH = SSM_WIDTH + 2 * SSM_BC

IN_SIZES = (Q_LORA, KV_LORA, QK_ROPE, MLA_WIDTH,
            GLA_KEY, GLA_KEY, GLA_WIDTH, GLA_GATE_RANK, GLA_GATE_RANK, GLA_WIDTH,
            SSM_WIDTH, SSM_CONV_CH, SSM_HEADS, SSM_HEADS,
            D_MODEL, D_MODEL, D_MODEL)
IN_NAMES = ("q_a", "kv_a", "k_r", "g_mla", "q_l", "k_l", "v_l", "ga_f", "ga_b", "g_gla",
            "z", "xbc", "dt_f", "dt_b", "m_mla", "m_gla", "m_ssm")
_IN_OFF = dict(zip(IN_NAMES, np.cumsum((0,) + IN_SIZES[:-1]).tolist()))
_IN_SIZE = dict(zip(IN_NAMES, IN_SIZES))

REGIONS = {
    "a": ("g_mla", "q_l", "k_l", "v_l"),
    "b": ("g_gla", "z", "xbc"),
    "c": ("m_mla", "m_gla", "m_ssm"),
    "s": ("q_a", "kv_a", "k_r", "ga_f", "ga_b", "dt_f", "dt_b"),
}
IN_TN = 1024
SEG = {}
REGION_W = {}
for _r, _names in REGIONS.items():
    _o = 0
    for _n in _names:
        SEG[_n] = (_r, _o)
        _o += _IN_SIZE[_n]
    REGION_W[_r] = -(-_o // IN_TN) * IN_TN
    if _r != "s":
        assert _o % IN_TN == 0
        assert all(_IN_OFF[_names[i]] + _IN_SIZE[_names[i]] == _IN_OFF[_names[i + 1]]
                   for i in range(len(_names) - 1))
SMALL_W = 256
SMALL_COL0 = SEG["k_r"][1]
assert SMALL_COL0 % SMALL_W == 0 and SEG["dt_b"][1] + _IN_SIZE["dt_b"] <= SMALL_COL0 + SMALL_W
SM = {n: SEG[n][1] - SMALL_COL0 for n in ("k_r", "ga_f", "ga_b", "dt_f", "dt_b")}

SUBLANES = 8
SEQ_BLOCK = 256
GLA_CHUNK = 128
SSM_CHUNK = 128
V7X_VMEM_LIMIT = 56 * 1024 * 1024


def _cparams(sem):
    return pltpu.CompilerParams(dimension_semantics=sem, vmem_limit_bytes=V7X_VMEM_LIMIT)


def _silu(x):
    return x * (1.0 / (1.0 + jnp.exp(-x)))


def _sigmoid(x):
    return 1.0 / (1.0 + jnp.exp(-x))


def _softplus(x):
    return jnp.maximum(x, 0.0) + jnp.log(1.0 + jnp.exp(-jnp.abs(x)))


def _dot(a, b):
    return jnp.dot(a, b, preferred_element_type=F32)


def _dot_nt(a, b):
    return lax.dot_general(a, b, (((1,), (1,)), ((), ())), preferred_element_type=F32)


def _split3(x):
    hi = x.astype(BF16)
    r1 = x - hi.astype(F32)
    mid = r1.astype(BF16)
    lo = (r1 - mid.astype(F32)).astype(BF16)
    return hi, mid, lo


def _tri_left(tri, x):
    hi, mid, lo = _split3(x)
    return _dot(tri, hi) + _dot(tri, mid) + _dot(tri, lo)


def _tri_right(x, tri):
    hi, mid, lo = _split3(x)
    return _dot(hi, tri) + _dot(mid, tri) + _dot(lo, tri)


def _drop_alias_refs(refs, n_in, n_alias):
    refs = list(refs)
    return refs[:n_in - n_alias] + refs[n_in:]


def _mm_kernel(*refs, rms, ep, cast_w, w_t):
    it = iter(refs)
    x_ref = next(it)
    w_ref = next(it)
    gain_ref = next(it) if rms else None
    m_ref = next(it) if ep in ("sig", "sigadd") else None
    prev_ref = next(it) if ep == "sigadd" else None
    res_ref = next(it) if ep == "resid" else None
    gate_ref = next(it) if ep == "resid" else None
    o_ref = next(it)
    wbf_ref = next(it) if cast_w else None

    if cast_w:
        @pl.when(pl.program_id(1) == 0)
        def _():
            wbf_ref[...] = (w_ref[0] if w_t else w_ref[...]).astype(BF16)
        w = wbf_ref[...]
    else:
        w = w_ref[...]
    x = x_ref[...]
    if rms:
        xf = x.astype(F32)
        ms = jnp.mean(xf * xf, axis=-1, keepdims=True)
        x = xf * lax.rsqrt(ms + EPS) * gain_ref[...]
    acc = _dot_nt(x.astype(BF16), w) if w_t else _dot(x.astype(BF16), w)
    if ep == "sig":
        acc = _sigmoid(m_ref[...]) * acc
    elif ep == "sigadd":
        acc = prev_ref[...].astype(F32) + _sigmoid(m_ref[...]) * acc
    elif ep == "resid":
        acc = res_ref[...] + gate_ref[...] * acc
    o_ref[...] = acc.astype(o_ref.dtype)


def _matmul(x, w, *, name, layer, n_out, tm, tn, out_dtype, w_col0=0, w_t=False, gain=None, x_col0=0, k=None,
            ep=None, m=None, m_col0=0, prev=None, res=None, gate_rows=None):
    M = x.shape[0]
    K = x.shape[1] if k is None else k
    assert M % tm == 0 and n_out % tn == 0 and m_col0 % tn == 0 and x_col0 % K == 0
    xj = x_col0 // K
    cast_w = w.dtype != BF16
    rms = gain is not None
    mj = m_col0 // tn
    if w_t:
        assert w_col0 % SUBLANES == 0
        w_spec = pl.BlockSpec((pl.Element(1), pl.Element(tn), pl.Element(K)),
                              lambda j, i: (layer, pl.multiple_of(w_col0 + j * tn, SUBLANES), 0))
    else:
        assert w_col0 % tn == 0
        wj = w_col0 // tn
        w_spec = pl.BlockSpec((None, K, tn), lambda j, i: (layer, 0, j + wj))

    in_specs = [pl.BlockSpec((tm, K), lambda j, i: (i, xj)), w_spec]
    args = [x, w]
    if rms:
        in_specs.append(pl.BlockSpec((1, K), lambda j, i: (0, 0)))
        args.append(gain.reshape(1, K))
    if ep in ("sig", "sigadd"):
        in_specs.append(pl.BlockSpec((tm, tn), lambda j, i: (i, j + mj)))
        args.append(m)
    if ep == "sigadd":
        in_specs.append(pl.BlockSpec((tm, tn), lambda j, i: (i, j)))
        args.append(prev)
    if ep == "resid":
        assert tm == SEQ_BLOCK
        in_specs.append(pl.BlockSpec((tm, tn), lambda j, i: (i, j)))
        args.append(res)
        in_specs.append(pl.BlockSpec((None, 1, tn), lambda j, i: (i, 0, j)))
        args.append(gate_rows)
    scratch = [pltpu.VMEM((tn, K) if w_t else (K, tn), BF16)] if cast_w else []
    return pl.pallas_call(
        functools.partial(_mm_kernel, rms=rms, ep=ep, cast_w=cast_w, w_t=w_t),
        grid=(n_out // tn, M // tm),
        in_specs=in_specs,
        out_specs=pl.BlockSpec((tm, tn), lambda j, i: (i, j)),
        out_shape=jax.ShapeDtypeStruct((M, n_out), out_dtype),
        scratch_shapes=scratch,
        compiler_params=_cparams(("arbitrary", "arbitrary")),
        name=name,
    )(*args)


def _prenorm_kernel(x_ref, nw_ref, mod_ref, o_ref):
    x = x_ref[...]
    ms = jnp.mean(x * x, axis=-1, keepdims=True)
    y = x * lax.rsqrt(ms + EPS) * nw_ref[...]
    shift = mod_ref[:, 0:D_MODEL]
    scale = mod_ref[:, D_MODEL:2 * D_MODEL]
    o_ref[...] = (y * (1.0 + scale) + shift).astype(o_ref.dtype)


def _prenorm(x, norm_w, mod_rows):
    M = x.shape[0]
    return pl.pallas_call(
        _prenorm_kernel,
        grid=(M // SEQ_BLOCK,),
        in_specs=[pl.BlockSpec((SEQ_BLOCK, D_MODEL), lambda i: (i, 0)),
                  pl.BlockSpec((1, D_MODEL), lambda i: (0, 0)),
                  pl.BlockSpec((None, 1, 3 * D_MODEL), lambda i: (i, 0, 0))],
        out_specs=pl.BlockSpec((SEQ_BLOCK, D_MODEL), lambda i: (i, 0)),
        out_shape=jax.ShapeDtypeStruct((M, D_MODEL), BF16),
        compiler_params=_cparams(("arbitrary",)),
        name="prenorm",
    )(x, norm_w.reshape(1, D_MODEL), mod_rows)


def _rmsnorm_kernel(x_ref, w_ref, o_ref):
    x = x_ref[...]
    ms = jnp.mean(x * x, axis=-1, keepdims=True)
    o_ref[...] = (x * lax.rsqrt(ms + EPS) * w_ref[...]).astype(o_ref.dtype)


def _rmsnorm(x, w, *, x_col0=0, width=None, tm=512):
    M = x.shape[0]
    width = x.shape[1] if width is None else width
    assert x_col0 % width == 0 and M % tm == 0
    cj = x_col0 // width
    return pl.pallas_call(
        _rmsnorm_kernel,
        grid=(M // tm,),
        in_specs=[pl.BlockSpec((tm, width), lambda i: (i, cj)),
                  pl.BlockSpec((1, width), lambda i: (0, 0))],
        out_specs=pl.BlockSpec((tm, width), lambda i: (i, 0)),
        out_shape=jax.ShapeDtypeStruct((M, width), F32),
        compiler_params=_cparams(("arbitrary",)),
        name="rmsnorm",
    )(x, w.reshape(1, width))


def _swap_pairs(x):
    n = x.shape[-1]
    nxt = pltpu.roll(x, n - 1, axis=1)
    prv = pltpu.roll(x, 1, axis=1)
    lane = lax.broadcasted_iota(jnp.int32, x.shape, 1)
    return jnp.where((lane & 1) == 0, nxt, prv)


def _mla_kernel(*refs, latent, n_alias):
    refs = _drop_alias_refs(refs, (11 if latent else 5) + n_alias, n_alias)
    it = iter(refs)
    qn_ref = next(it)
    qr_ref = next(it)
    g_ref = next(it)
    kv_ref = next(it)
    sm_ref = next(it)
    if latent:
        kvc_ref = next(it)
        krc_ref = next(it)
        cq_ref = next(it)
        sq_ref = next(it)
        ck_ref = next(it)
        sk_ref = next(it)
    o_ref = next(it)

    scale = float(QK_NOPE + QK_ROPE) ** -0.5
    qr = qr_ref[...]
    sm = sm_ref[...]
    if latent:
        qr = qr * cq_ref[...] + _swap_pairs(qr) * sq_ref[...]
        sm = sm * ck_ref[...] + _swap_pairs(sm) * sk_ref[...]
    kr = sm[:, SM["k_r"]:SM["k_r"] + QK_ROPE].astype(BF16)
    if latent:
        krc = krc_ref[...].astype(BF16)

    for h in range(MLA_HEADS):
        q_h = jnp.concatenate([qn_ref[:, h * QK_NOPE:(h + 1) * QK_NOPE],
                               qr[:, h * QK_ROPE:(h + 1) * QK_ROPE].astype(BF16)], axis=1)
        c0 = h * (QK_NOPE + V_HEAD)
        v_h = kv_ref[:, c0 + QK_NOPE:c0 + QK_NOPE + V_HEAD]
        s_own = _dot_nt(q_h, jnp.concatenate([kv_ref[:, c0:c0 + QK_NOPE], kr], axis=1)) * scale
        mx = jnp.max(s_own, axis=-1, keepdims=True)
        if latent:
            vc_h = kvc_ref[:, c0 + QK_NOPE:c0 + QK_NOPE + V_HEAD]
            s_ctx = _dot_nt(q_h, jnp.concatenate([kvc_ref[:, c0:c0 + QK_NOPE], krc], axis=1)) * scale
            mx = jnp.maximum(mx, jnp.max(s_ctx, axis=-1, keepdims=True))
            p_ctx = jnp.exp(s_ctx - mx)
        p_own = jnp.exp(s_own - mx)
        den = jnp.sum(p_own, axis=-1, keepdims=True)
        if latent:
            den = den + jnp.sum(p_ctx, axis=-1, keepdims=True)
        inv = 1.0 / den
        o_h = _dot((p_own * inv).astype(BF16), v_h)
        if latent:
            o_h = o_h + _dot((p_ctx * inv).astype(BF16), vc_h)
        g_h = g_ref[:, h * V_HEAD:(h + 1) * V_HEAD]
        o_ref[:, h * V_HEAD:(h + 1) * V_HEAD] = (o_h * _silu(g_h)).astype(o_ref.dtype)


def _mla(qn, qr, P, kv, *, n_tok, n_batch, S, row0, latent, kv_ctx_row0=0, krope_ctx=None, tables=None,
         out_prev=None):
    tq = min(S, 256)
    nq = S // tq
    rb = row0 // tq
    sb = row0 // S
    HW = MLA_HEADS * (QK_NOPE + V_HEAD)
    g_reg, g_col = SEG["g_mla"]
    gj = g_col // MLA_WIDTH
    smj = SMALL_COL0 // SMALL_W
    in_specs = [
        pl.BlockSpec((tq, MLA_HEADS * QK_NOPE), lambda b, i: (rb + b * nq + i, 0)),
        pl.BlockSpec((tq, MLA_HEADS * QK_ROPE), lambda b, i: (rb + b * nq + i, 0)),
        pl.BlockSpec((tq, MLA_WIDTH), lambda b, i: (rb + b * nq + i, gj)),
        pl.BlockSpec((S, HW), lambda b, i: (sb + b, 0)),
        pl.BlockSpec((S, SMALL_W), lambda b, i: (sb + b, smj)),
    ]
    args = [qn, qr, P[g_reg], kv, P["s"]]
    if latent:
        Tc = krope_ctx.shape[1]
        cb = kv_ctx_row0 // Tc
        cos_q, sin_q, cos_k, sin_k = tables
        in_specs += [
            pl.BlockSpec((Tc, HW), lambda b, i: (cb + b, 0)),
            pl.BlockSpec((None, Tc, QK_ROPE), lambda b, i: (b, 0, 0)),
            pl.BlockSpec((tq, MLA_HEADS * QK_ROPE), lambda b, i: (i, 0)),
            pl.BlockSpec((tq, MLA_HEADS * QK_ROPE), lambda b, i: (i, 0)),
            pl.BlockSpec((S, SMALL_W), lambda b, i: (0, 0)),
            pl.BlockSpec((S, SMALL_W), lambda b, i: (0, 0)),
        ]
        args += [kv, krope_ctx, cos_q, sin_q, cos_k, sin_k]
    aliases = {}
    if out_prev is not None:
        aliases[len(args)] = 0
        in_specs.append(pl.BlockSpec(memory_space=pl.ANY))
        args.append(out_prev)
    return pl.pallas_call(
        functools.partial(_mla_kernel, latent=latent, n_alias=len(aliases)),
        grid=(n_batch, nq),
        in_specs=in_specs,
        out_specs=pl.BlockSpec((tq, MLA_WIDTH), lambda b, i: (rb + b * nq + i, 0)),
        out_shape=jax.ShapeDtypeStruct((n_tok, MLA_WIDTH), BF16),
        input_output_aliases=aliases,
        compiler_params=_cparams(("arbitrary", "arbitrary")),
        name="mla_latent" if latent else "mla_context",
    )(*args)


def _rope_tables(S):
    rows = jnp.repeat(jnp.arange(S // GRID_W, dtype=F32), GRID_W)
    cols = jnp.tile(jnp.arange(GRID_W, dtype=F32), S // GRID_W)
    n_freq = QK_ROPE // 4
    inv = ROPE_BASE ** (-jnp.arange(n_freq, dtype=F32) / n_freq)
    ang = jnp.concatenate([rows[:, None] * inv, cols[:, None] * inv], axis=-1)
    cos = jnp.repeat(jnp.cos(ang), 2, axis=-1)
    sin = jnp.repeat(jnp.sin(ang), 2, axis=-1)
    sign = jnp.tile(jnp.array([-1.0, 1.0], F32), QK_ROPE // 2)
    sin = sin * sign
    cos_q = jnp.tile(cos, (1, MLA_HEADS))
    sin_q = jnp.tile(sin, (1, MLA_HEADS))
    lo, hi = SM["k_r"], SMALL_W - SM["k_r"] - QK_ROPE
    cos_k = jnp.concatenate([jnp.ones((S, lo), F32), cos, jnp.ones((S, hi), F32)], axis=-1)
    sin_k = jnp.concatenate([jnp.zeros((S, lo), F32), sin, jnp.zeros((S, hi), F32)], axis=-1)
    return cos_q, sin_q, cos_k, sin_k


def _gla_kernel(*refs, S, C, has_state, emit_state, n_alias):
    refs = _drop_alias_refs(refs, 8 + (2 if has_state else 0) + n_alias, n_alias)
    it = iter(refs)
    q_ref = next(it)
    k_ref = next(it)
    v_ref = next(it)
    sm_ref = next(it)
    gg_ref = next(it)
    w2_ref = next(it)
    b2_ref = next(it)
    nw_ref = next(it)
    s0_refs = (next(it), next(it)) if has_state else None
    o_ref = next(it)
    so_refs = (next(it), next(it)) if emit_state else None
    vt_ref = next(it)
    la_ref = next(it)
    acc_ref = next(it)
    st_ref = next(it)

    n = S // C
    scale = float(GLA_DK) ** -0.5
    sm = sm_ref[...]
    for d, name in enumerate(("ga_f", "ga_b")):
        ga = sm[:, SM[name]:SM[name] + GLA_GATE_RANK].astype(BF16)
        xg = _dot(ga, w2_ref[d].astype(BF16)) + b2_ref[d]
        la_ref[d] = -_softplus(-xg) * (1.0 / GLA_GATE_NORM)
    for c in range(n):
        vt_ref[c] = v_ref[c * C:(c + 1) * C, :].T.astype(BF16)
    for d in range(2):
        if has_state:
            st_ref[d] = s0_refs[d][...].T
        else:
            st_ref[d] = jnp.zeros((GLA_DV, GLA_DK), F32)

    row = lax.broadcasted_iota(jnp.int32, (C, C), 0)
    col = lax.broadcasted_iota(jnp.int32, (C, C), 1)
    lower = row >= col
    upper = row <= col
    tri = (jnp.where(lower, 1.0, 0.0).astype(BF16), jnp.where(upper, 1.0, 0.0).astype(BF16))
    mid = C // 2

    def chunk(c, d):
        r0 = pl.multiple_of(c * C, C)
        q = q_ref[pl.ds(r0, C), :] * scale
        k = k_ref[pl.ds(r0, C), :]
        v = v_ref[pl.ds(r0, C), :].astype(BF16)
        g = la_ref[d, pl.ds(r0, C), :]
        b = _tri_left(tri[d], g)
        tot = b[C - 1:C, :] if d == 0 else b[0:1, :]
        bm = b[mid:mid + 1, :]
        st = st_ref[d]
        o = _dot_nt((q * jnp.exp(b)).astype(BF16), st.astype(BF16))
        qa = (q * jnp.exp(b - bm)).astype(BF16)
        ka = (k * jnp.exp(bm - b)).astype(BF16)
        att = _dot_nt(qa, ka)
        att = jnp.where(lower if d == 0 else upper, att, 0.0)
        o = o + _dot(att.astype(BF16), v)
        kd = (k * jnp.exp(tot - b)).astype(BF16)
        st_ref[d] = st * jnp.exp(tot) + _dot(vt_ref[c], kd)
        acc_ref[d, pl.ds(r0, C), :] = o

    def both(i, carry):
        chunk(i, 0)
        chunk(n - 1 - i, 1)
        return carry

    lax.fori_loop(0, n, both, 0)

    o = acc_ref[0] + acc_ref[1]
    ms = jnp.mean(o * o, axis=-1, keepdims=True)
    y = o * lax.rsqrt(ms + EPS) * nw_ref[...]
    o_ref[...] = (y * _silu(gg_ref[...])).astype(o_ref.dtype)
    if emit_state:
        for d in range(2):
            so_refs[d][...] = st_ref[d].T


def _gla(P, w2, b2, nw, *, layer, n_tok, n_batch, S, row0, states_in=None, states_out=None, out_prev=None):
    C = GLA_CHUNK
    sb = row0 // S
    col = {n: SEG[n][1] for n in ("q_l", "k_l", "v_l", "g_gla")}
    qj, kj, vj, gj = col["q_l"] // GLA_DK, col["k_l"] // GLA_DK, col["v_l"] // GLA_DV, col["g_gla"] // GLA_DV
    smj = SMALL_COL0 // SMALL_W
    in_specs = [
        pl.BlockSpec((S, GLA_DK), lambda b, h: (sb + b, qj + h)),
        pl.BlockSpec((S, GLA_DK), lambda b, h: (sb + b, kj + h)),
        pl.BlockSpec((S, GLA_DV), lambda b, h: (sb + b, vj + h)),
        pl.BlockSpec((S, SMALL_W), lambda b, h: (sb + b, smj)),
        pl.BlockSpec((S, GLA_DV), lambda b, h: (sb + b, gj + h)),
        pl.BlockSpec((None, 2, GLA_GATE_RANK, GLA_DK), lambda b, h: (layer, 0, 0, h)),
        pl.BlockSpec((None, 2, 1, GLA_DK), lambda b, h: (layer, 0, 0, h)),
        pl.BlockSpec((None, 1, GLA_DV), lambda b, h: (layer, 0, 0)),
    ]
    args = [P[SEG["q_l"][0]], P[SEG["k_l"][0]], P[SEG["v_l"][0]], P["s"], P[SEG["g_gla"][0]],
            w2, b2.reshape(DEPTH, 2, 1, GLA_KEY), nw.reshape(DEPTH, 1, GLA_DV)]
    has_state = states_in is not None
    st_block = (None, None, None, GLA_DK, GLA_DV)
    if has_state:
        for s in states_in:
            in_specs.append(pl.BlockSpec(st_block, lambda b, h: (b, layer, h, 0, 0)))
            args.append(s)
    out_specs = [pl.BlockSpec((S, GLA_DV), lambda b, h: (sb + b, h))]
    out_shape = [jax.ShapeDtypeStruct((n_tok, GLA_WIDTH), BF16)]
    aliased = [out_prev]
    emit_state = states_out is not None
    if emit_state:
        for s in states_out:
            out_specs.append(pl.BlockSpec(st_block, lambda b, h: (b, layer, h, 0, 0)))
            out_shape.append(jax.ShapeDtypeStruct((n_batch, DEPTH, GLA_HEADS, GLA_DK, GLA_DV), F32))
            aliased.append(s)
    aliases = {}
    for oi, buf in enumerate(aliased):
        if buf is not None:
            aliases[len(args)] = oi
            in_specs.append(pl.BlockSpec(memory_space=pl.ANY))
            args.append(buf)
    n = S // C
    return pl.pallas_call(
        functools.partial(_gla_kernel, S=S, C=C, has_state=has_state, emit_state=emit_state,
                          n_alias=len(aliases)),
        grid=(n_batch, GLA_HEADS),
        in_specs=in_specs,
        out_specs=out_specs,
        out_shape=out_shape,
        input_output_aliases=aliases,
        scratch_shapes=[pltpu.VMEM((n, GLA_DV, C), BF16), pltpu.VMEM((2, S, GLA_DK), F32),
                        pltpu.VMEM((2, S, GLA_DV), F32), pltpu.VMEM((2, GLA_DV, GLA_DK), F32)],
        compiler_params=_cparams(("arbitrary", "arbitrary")),
        name="gla_latent" if has_state else "gla_context",
    )(*args)


CONV_PAD = SUBLANES
CONV_ROWS = 256


def _conv_silu(dst_ref, src_ref, w_ref, b_ref, xp_ref, S):
    W = src_ref.shape[1]
    left = (SSM_CONV - 1) // 2
    xp_ref[0:CONV_PAD, 0:W] = jnp.zeros((CONV_PAD, W), F32)
    xp_ref[CONV_PAD + S:CONV_PAD + S + CONV_PAD, 0:W] = jnp.zeros((CONV_PAD, W), F32)
    xp_ref[CONV_PAD:CONV_PAD + S, 0:W] = src_ref[...]
    for r in range(0, S, CONV_ROWS):
        acc = b_ref[...] + xp_ref[CONV_PAD - left + r:CONV_PAD - left + r + CONV_ROWS, 0:W] * w_ref[0:1, :]
        for j in range(1, SSM_CONV):
            r0 = CONV_PAD - left + j + r
            acc = acc + xp_ref[r0:r0 + CONV_ROWS, 0:W] * w_ref[j:j + 1, :]
        dst_ref[r:r + CONV_ROWS, :] = _silu(acc)
def _expand_heads(cols, width):
    C = cols.shape[0]
    lane_head = lax.broadcasted_iota(jnp.int32, (C, width), 1) // SSM_HEADDIM
    out = jnp.zeros((C, width), F32)
    for j in range(SSM_HPG):
        out = jnp.where(lane_head == j, cols[:, j:j + 1], out)
    return out


def _ssd_kernel(*refs, S, C, has_state, emit_state, n_alias):
    refs = _drop_alias_refs(refs, 15 + (2 if has_state else 0) + n_alias, n_alias)
    it = iter(refs)
    xin_ref = next(it)
    bin_ref = next(it)
    cin_ref = next(it)
    cw_refs = (next(it), next(it), next(it))
    cb_refs = (next(it), next(it), next(it))
    z_ref = next(it)
    dtc_ref = next(it)
    dtr_ref = next(it)
    pc_ref = next(it)
    pr_ref = next(it)
    d_ref = next(it)
    s0_refs = (next(it), next(it)) if has_state else None
    y_ref = next(it)
    so_refs = (next(it), next(it)) if emit_state else None
    bt_ref = next(it)
    acc_ref = next(it)
    st_ref = next(it)
    xp_ref = next(it)
    x_ref = next(it)
    b_ref = next(it)
    c_ref = next(it)

    for dst, src, w, b in zip((x_ref, b_ref, c_ref), (xin_ref, bin_ref, cin_ref), cw_refs, cb_refs):
        _conv_silu(dst, src, w, b, xp_ref, S)

    n = S // C
    for c in range(n):
        bt_ref[c] = b_ref[c * C:(c + 1) * C, :].T.astype(BF16)
    for d in range(2):
        if has_state:
            s0 = jnp.concatenate([s0_refs[d][j] for j in range(SSM_HPG)], axis=0)
            st_ref[d] = s0.T
        else:
            st_ref[d] = jnp.zeros((SSM_STATE, SSM_GW), F32)

    row = lax.broadcasted_iota(jnp.int32, (C, C), 0)
    col = lax.broadcasted_iota(jnp.int32, (C, C), 1)
    lower = row >= col
    upper = row <= col
    tri_l = jnp.where(lower, 1.0, 0.0).astype(BF16)
    tri_u = jnp.where(upper, 1.0, 0.0).astype(BF16)

    def chunk(c, d):
        r0 = pl.multiple_of(c * C, C)
        x = x_ref[pl.ds(r0, C), :]
        bm = b_ref[pl.ds(r0, C), :].astype(BF16)
        cm = c_ref[pl.ds(r0, C), :].astype(BF16)
        a_c = -jnp.exp(pc_ref[d, 1])
        a_r = -jnp.exp(pr_ref[d, 1])
        dt_c = _softplus(dtc_ref[d, pl.ds(r0, C), :] + pc_ref[d, 0])
        dt_r = _softplus(dtr_ref[d, c] + pr_ref[d, 0])
        if d == 0:
            cum_c = _tri_left(tri_l, dt_c * a_c)
            cum_r = _tri_right(dt_r * a_r, tri_u)
        else:
            cum_c = _tri_left(tri_u, dt_c * a_c)
            cum_r = _tri_right(dt_r * a_r, tri_l)
        mask = lower if d == 0 else upper
        cum_e = _expand_heads(cum_c, SSM_GW)
        dt_e = _expand_heads(dt_c, SSM_GW)
        xg = x * dt_e
        st = st_ref[d]
        y = _dot(cm, st.astype(BF16)) * jnp.exp(cum_e)
        cb = _dot_nt(cm, bm)
        parts = []
        for j in range(SSM_HPG):
            dec = jnp.where(mask, jnp.exp(cum_c[:, j:j + 1] - cum_r[j:j + 1, :]), 0.0)
            parts.append(_dot((cb * dec).astype(BF16), xg[:, j * SSM_HEADDIM:(j + 1) * SSM_HEADDIM].astype(BF16)))
        y = y + jnp.concatenate(parts, axis=-1)
        last = cum_e[C - 1:C, :] if d == 0 else cum_e[0:1, :]
        xw = (xg * jnp.exp(last - cum_e)).astype(BF16)
        st_ref[d] = st * jnp.exp(last) + _dot(bt_ref[c], xw)
        acc_ref[d, pl.ds(r0, C), :] = y

    def both(i, carry):
        chunk(i, 0)
        chunk(n - 1 - i, 1)
        return carry

    lax.fori_loop(0, n, both, 0)

    y = acc_ref[0] + acc_ref[1] + x_ref[...] * d_ref[...]
    y_ref[...] = y * _silu(z_ref[...])
    if emit_state:
        for d in range(2):
            s_t = st_ref[d].T
            for j in range(SSM_HPG):
                so_refs[d][j] = s_t[j * SSM_HEADDIM:(j + 1) * SSM_HEADDIM, :]


def _ssd(P, cw, cb, dt_col, dt_row, p_col, p_row, d_lane, *, layer, n_tok, n_batch, S, row0,
         states_in=None, states_out=None, out_prev=None):
    C = SSM_CHUNK
    sb = row0 // S
    z_reg, z_col = SEG["z"]
    x_reg, x_col = SEG["xbc"]
    zj = z_col // SSM_GW
    xj = x_col // SSM_GW
    bj = SSM_WIDTH // SSM_STATE
    cj = (SSM_WIDTH + SSM_BC) // SSM_STATE
    xbj = x_col // SSM_STATE

    def taps(width, j0):
        return pl.BlockSpec((None, SSM_CONV, width), lambda b, g: (layer, 0, j0 + g))

    def bias(width, j0):
        return pl.BlockSpec((None, 1, width), lambda b, g: (layer, 0, j0 + g))

    in_specs = [
        pl.BlockSpec((S, SSM_GW), lambda b, g: (sb + b, xj + g)),
        pl.BlockSpec((S, SSM_STATE), lambda b, g: (sb + b, xbj + bj + g)),
        pl.BlockSpec((S, SSM_STATE), lambda b, g: (sb + b, xbj + cj + g)),
        taps(SSM_GW, 0), taps(SSM_STATE, bj), taps(SSM_STATE, cj),
        bias(SSM_GW, 0), bias(SSM_STATE, bj), bias(SSM_STATE, cj),
        pl.BlockSpec((S, SSM_GW), lambda b, g: (sb + b, zj + g)),
        pl.BlockSpec((2, None, S, SSM_HPG), lambda b, g: (0, g, sb + b, 0)),
        pl.BlockSpec((2, None, S // C, SSM_HPG, C), lambda b, g: (0, g, sb + b, 0, 0)),
        pl.BlockSpec((None, 2, 2, None, 1, SSM_HPG), lambda b, g: (layer, 0, 0, g, 0, 0)),
        pl.BlockSpec((None, 2, 2, None, SSM_HPG, 1), lambda b, g: (layer, 0, 0, g, 0, 0)),
        pl.BlockSpec((None, 1, SSM_GW), lambda b, g: (layer, 0, g)),
    ]
    cb3 = cb.reshape(DEPTH, 1, SSM_CONV_CH)
    args = [P[x_reg], P[x_reg], P[x_reg], cw, cw, cw, cb3, cb3, cb3, P[z_reg], dt_col, dt_row, p_col, p_row,
            d_lane]
    has_state = states_in is not None
    st_block = (None, None, SSM_HPG, SSM_HEADDIM, SSM_STATE)
    if has_state:
        for s in states_in:
            in_specs.append(pl.BlockSpec(st_block, lambda b, g: (b, layer, g, 0, 0)))
            args.append(s)
    out_specs = [pl.BlockSpec((S, SSM_GW), lambda b, g: (sb + b, g))]
    out_shape = [jax.ShapeDtypeStruct((n_tok, SSM_WIDTH), F32)]
    aliased = [out_prev]
    emit_state = states_out is not None
    if emit_state:
        for s in states_out:
            out_specs.append(pl.BlockSpec(st_block, lambda b, g: (b, layer, g, 0, 0)))
            out_shape.append(jax.ShapeDtypeStruct((n_batch, DEPTH, SSM_HEADS, SSM_HEADDIM, SSM_STATE), F32))
            aliased.append(s)
    aliases = {}
    for oi, buf in enumerate(aliased):
        if buf is not None:
            aliases[len(args)] = oi
            in_specs.append(pl.BlockSpec(memory_space=pl.ANY))
            args.append(buf)
    n = S // C
    return pl.pallas_call(
        functools.partial(_ssd_kernel, S=S, C=C, has_state=has_state, emit_state=emit_state,
                          n_alias=len(aliases)),
        grid=(n_batch, SSM_GROUPS),
        in_specs=in_specs,
        out_specs=out_specs,
        out_shape=out_shape,
        input_output_aliases=aliases,
        scratch_shapes=[pltpu.VMEM((n, SSM_STATE, C), BF16), pltpu.VMEM((2, S, SSM_GW), F32),
                        pltpu.VMEM((2, SSM_STATE, SSM_GW), F32),
                        pltpu.VMEM((S + 2 * CONV_PAD, SSM_GW), F32), pltpu.VMEM((S, SSM_GW), F32),
                        pltpu.VMEM((S, SSM_STATE), F32), pltpu.VMEM((S, SSM_STATE), F32)],
        compiler_params=_cparams(("arbitrary", "arbitrary")),
        name="ssd_latent" if has_state else "ssd_context",
    )(*args)


def _gather_small_w_in(w_in_t):
    names = REGIONS["s"]
    rows = [w_in_t[:, _IN_OFF[n]:_IN_OFF[n] + _IN_SIZE[n], :] for n in names]
    used = sum(_IN_SIZE[n] for n in names)
    rows.append(jnp.zeros((DEPTH, REGION_W["s"] - used, D_MODEL), w_in_t.dtype))
    return jnp.concatenate(rows, axis=1)


def kernel(x_prompt, x_sample, cache_mla_ckv, cache_mla_krope, state_gla_fwd, state_gla_bwd,
           state_ssm_fwd, state_ssm_bwd, c, c_ctx, w_mod, b_mod, norm_w, w_in, q_a_norm, w_q_b,
           kv_a_norm, w_kv_b, gla_w_gate2, gla_b_gate, gla_norm, ssm_conv_w, ssm_conv_b,
           ssm_dt_bias, ssm_a_log, ssm_d, ssm_norm, w_br_mla, w_br_gla, w_br_ssm, w_out, final_norm):
    nb_p, s_p, _ = x_prompt.shape
    nb_l, s_l, _ = x_sample.shape
    n_p = nb_p * s_p
    n_l = nb_l * s_l
    n_tok = n_p + n_l
    assert s_p == SEQ_BLOCK and s_l % SEQ_BLOCK == 0 and n_p % s_l == 0
    t_c = cache_mla_ckv.shape[2]
    assert t_c == SEQ_BLOCK

    x = jnp.concatenate([x_prompt.reshape(n_p, D_MODEL), x_sample.reshape(n_l, D_MODEL)], axis=0)

    w_in_t = jnp.swapaxes(w_in, 1, 2)
    w_in_s = _gather_small_w_in(w_in_t)
    wq = w_q_b.reshape(DEPTH, Q_LORA, MLA_HEADS, QK_NOPE + QK_ROPE)
    wq_nope = wq[..., :QK_NOPE].reshape(DEPTH, Q_LORA, MLA_HEADS * QK_NOPE)
    wq_rope = wq[..., QK_NOPE:].reshape(DEPTH, Q_LORA, MLA_HEADS * QK_ROPE)
    tables = _rope_tables(s_l)
    p_ssm = jnp.stack([ssm_dt_bias, ssm_a_log], axis=2).reshape(DEPTH, 2, 2, SSM_GROUPS, SSM_HPG)
    p_col = p_ssm[:, :, :, :, None, :]
    p_row = p_ssm[:, :, :, :, :, None]
    d_lane = jnp.repeat(ssm_d, SSM_HEADDIM, axis=-1).reshape(DEPTH, 1, SSM_WIDTH)
    blk_mod = np.concatenate([np.zeros(n_p // SEQ_BLOCK, np.int32),
                              1 + np.repeat(np.arange(nb_l, dtype=np.int32), s_l // SEQ_BLOCK)])
    cond = jnp.concatenate([c_ctx[None, :], c], axis=0)
    cond = _silu(cond)
    cond = jnp.concatenate([cond, jnp.zeros((8 - cond.shape[0] % 8, D_MODEL), F32)], axis=0)

    gla_f = gla_b = ssm_f = ssm_b = None
    ckv_l, kr_l = [], []
    for l in range(DEPTH):
        mod = _matmul(cond, w_mod, name="mm_mod", layer=l, n_out=3 * D_MODEL, tm=cond.shape[0], tn=1536,
                      out_dtype=F32)
        mod = mod[:1 + nb_l] + b_mod[l][None, :]
        mod_rows = mod[blk_mod][:, None, :]
        gate_rows = mod_rows[:, :, 2 * D_MODEL:]

        h = _prenorm(x, norm_w[l], mod_rows)
        P = {}
        for r in ("a", "b", "c"):
            first = REGIONS[r][0]
            P[r] = _matmul(h, w_in_t, name="mm_in_" + r, layer=l, w_t=True, w_col0=_IN_OFF[first],
                           n_out=REGION_W[r], tm=512, tn=IN_TN, out_dtype=F32)
        P["s"] = _matmul(h, w_in_s, name="mm_in_s", layer=l, w_t=True, n_out=REGION_W["s"], tm=512, tn=IN_TN,
                         out_dtype=F32)
        small = P["s"][:, SMALL_COL0:SMALL_COL0 + SMALL_W]

        qa_col, kva_col = SEG["q_a"][1], SEG["kv_a"][1]
        qn = _matmul(P["s"], wq_nope, name="mm_q_nope", x_col0=qa_col, k=Q_LORA, layer=l,
                     n_out=MLA_HEADS * QK_NOPE, tm=512, tn=512, out_dtype=BF16, gain=q_a_norm[l])
        qr = _matmul(P["s"], wq_rope, name="mm_q_rope", x_col0=qa_col, k=Q_LORA, layer=l,
                     n_out=MLA_HEADS * QK_ROPE, tm=512, tn=512, out_dtype=F32, gain=q_a_norm[l])
        ckv = _rmsnorm(P["s"], kv_a_norm[l], x_col0=kva_col, width=KV_LORA)
        ckv_all = jnp.concatenate([ckv, cache_mla_ckv[:, l].reshape(nb_l * t_c, KV_LORA)], axis=0)
        kv = _matmul(ckv_all, w_kv_b, name="mm_kv", layer=l, n_out=MLA_HEADS * (QK_NOPE + V_HEAD), tm=512,
                     tn=512, out_dtype=BF16)
        o_mla = _mla(qn, qr, P, kv, n_tok=n_tok, n_batch=nb_p, S=s_p, row0=0, latent=False)
        o_mla = _mla(qn, qr, P, kv, n_tok=n_tok, n_batch=nb_l, S=s_l, row0=n_p, latent=True,
                     kv_ctx_row0=n_tok, krope_ctx=cache_mla_krope[:, l], tables=tables, out_prev=o_mla)
        ckv_l.append(ckv[:n_p].reshape(nb_p, s_p, KV_LORA))
        kr_l.append(small[:n_p, SM["k_r"]:SM["k_r"] + QK_ROPE].reshape(nb_p, s_p, QK_ROPE))

        o_gla, gla_f, gla_b = _gla(P, gla_w_gate2, gla_b_gate, gla_norm, layer=l, n_tok=n_tok, n_batch=nb_p,
                                   S=s_p, row0=0, states_out=(gla_f, gla_b))
        (o_gla,) = _gla(P, gla_w_gate2, gla_b_gate, gla_norm, layer=l, n_tok=n_tok, n_batch=nb_l, S=s_l,
                        row0=n_p, states_in=(state_gla_fwd, state_gla_bwd), out_prev=o_gla)

        dt = small[:, SM["dt_f"]:SM["dt_f"] + 2 * SSM_HEADS].reshape(n_tok, 2, SSM_GROUPS, SSM_HPG)
        dt_col = jnp.transpose(dt, (1, 2, 0, 3))
        dt_row = jnp.transpose(dt.reshape(n_tok // SSM_CHUNK, SSM_CHUNK, 2, SSM_GROUPS, SSM_HPG),
                               (2, 3, 0, 4, 1))
        yz, ssm_f, ssm_b = _ssd(P, ssm_conv_w, ssm_conv_b, dt_col, dt_row, p_col, p_row, d_lane, layer=l,
                                n_tok=n_tok, n_batch=nb_p, S=s_p, row0=0, states_out=(ssm_f, ssm_b))
        (yz,) = _ssd(P, ssm_conv_w, ssm_conv_b, dt_col, dt_row, p_col, p_row, d_lane, layer=l, n_tok=n_tok,
                     n_batch=nb_l, S=s_l, row0=n_p, states_in=(state_ssm_fwd, state_ssm_bwd), out_prev=yz)

        mg = _matmul(o_mla, w_br_mla, name="mm_br_mla", layer=l, n_out=D_MODEL, tm=512, tn=512, out_dtype=F32,
                     ep="sig", m=P["c"], m_col0=SEG["m_mla"][1])
        mg = _matmul(o_gla, w_br_gla, name="mm_br_gla", layer=l, n_out=D_MODEL, tm=512, tn=512, out_dtype=F32,
                     ep="sigadd", m=P["c"], m_col0=SEG["m_gla"][1], prev=mg)
        mg = _matmul(yz, w_br_ssm, name="mm_br_ssm", layer=l, n_out=D_MODEL, tm=256, tn=512, out_dtype=BF16,
                     gain=ssm_norm[l], ep="sigadd", m=P["c"], m_col0=SEG["m_ssm"][1], prev=mg)
        x = _matmul(mg, w_out, name="mm_out", layer=l, n_out=D_MODEL, tm=SEQ_BLOCK, tn=512, out_dtype=F32,
                    ep="resid", res=x, gate_rows=gate_rows)

    y = _rmsnorm(x, final_norm)
    y_prompt = y[:n_p].reshape(nb_p, s_p, D_MODEL)
    y_sample = y[n_p:].reshape(nb_l, s_l, D_MODEL)
    return (y_prompt, y_sample, jnp.stack(ckv_l, axis=1), jnp.stack(kr_l, axis=1),
            gla_f, gla_b, ssm_f, ssm_b)
```

```python
import functools

import jax
import jax.numpy as jnp
import numpy as np
from jax import lax
from jax.experimental import pallas as pl
from jax.experimental.pallas import tpu as pltpu

F32 = jnp.float32
BF16 = jnp.bfloat16

D_MODEL = 2048
DEPTH = 4
GRID_W = 64
EPS = 1e-6
MLA_HEADS = 16
QK_NOPE = 128
QK_ROPE = 64
V_HEAD = 128
Q_LORA = 512
KV_LORA = 256
MLA_WIDTH = MLA_HEADS * V_HEAD
ROPE_BASE = 10000.0
GLA_HEADS = 4
GLA_DK = 256
GLA_DV = 512
GLA_KEY = GLA_HEADS * GLA_DK
GLA_WIDTH = GLA_HEADS * GLA_DV
GLA_GATE_RANK = 16
GLA_GATE_NORM = 16.0
SSM_HEADS = 64
SSM_HEADDIM = 64
SSM_WIDTH = SSM_HEADS * SSM_HEADDIM
SSM_GROUPS = 8
SSM_HPG = SSM_HEADS // SSM_GROUPS
SSM_GW = SSM_HPG * SSM_HEADDIM
SSM_STATE = 128
SSM_CONV = 5
SSM_BC = SSM_GROUPS * SSM_STATE
SSM_CONV_CH = SSM_WIDTH + 2 * SSM_BC

IN_SIZES = (Q_LORA, KV_LORA, QK_ROPE, MLA_WIDTH,
            GLA_KEY, GLA_KEY, GLA_WIDTH, GLA_GATE_RANK, GLA_GATE_RANK, GLA_WIDTH,
            SSM_WIDTH, SSM_CONV_CH, SSM_HEADS, SSM_HEADS,
            D_MODEL, D_MODEL, D_MODEL)
IN_NAMES = ("q_a", "kv_a", "k_r", "g_mla", "q_l", "k_l", "v_l", "ga_f", "ga_b", "g_gla",
            "z", "xbc", "dt_f", "dt_b", "m_mla", "m_gla", "m_ssm")
_IN_OFF = dict(zip(IN_NAMES, np.cumsum((0,) + IN_SIZES[:-1]).tolist()))
_IN_SIZE = dict(zip(IN_NAMES, IN_SIZES))

REGIONS = {
    "a": ("g_mla", "q_l", "k_l", "v_l"),
    "b": ("g_gla", "z", "xbc"),
    "c": ("m_mla", "m_gla", "m_ssm"),
    "s": ("q_a", "kv_a", "k_r", "ga_f", "ga_b", "dt_f", "dt_b"),
}
IN_TN = 1024
SEG = {}
REGION_W = {}
for _r, _names in REGIONS.items():
    _o = 0
    for _n in _names:
        SEG[_n] = (_r, _o)
        _o += _IN_SIZE[_n]
    REGION_W[_r] = -(-_o // IN_TN) * IN_TN
    if _r != "s":
        assert _o % IN_TN == 0
        assert all(_IN_OFF[_names[i]] + _IN_SIZE[_names[i]] == _IN_OFF[_names[i + 1]]
                   for i in range(len(_names) - 1))
SMALL_W = 256
SMALL_COL0 = SEG["k_r"][1]
assert SMALL_COL0 % SMALL_W == 0 and SEG["dt_b"][1] + _IN_SIZE["dt_b"] <= SMALL_COL0 + SMALL_W
SM = {n: SEG[n][1] - SMALL_COL0 for n in ("k_r", "ga_f", "ga_b", "dt_f", "dt_b")}

SUBLANES = 8
SEQ_BLOCK = 256
GLA_CHUNK = 128
SSM_CHUNK = 128
V7X_VMEM_LIMIT = 56 * 1024 * 1024


def _cparams(sem):
    return pltpu.CompilerParams(dimension_semantics=sem, vmem_limit_bytes=V7X_VMEM_LIMIT)


def _silu(x):
    return x * (1.0 / (1.0 + jnp.exp(-x)))


def _sigmoid(x):
    return 1.0 / (1.0 + jnp.exp(-x))


def _softplus(x):
    return jnp.maximum(x, 0.0) + jnp.log(1.0 + jnp.exp(-jnp.abs(x)))


def _dot(a, b):
    return jnp.dot(a, b, preferred_element_type=F32)


def _dot_nt(a, b):
    return lax.dot_general(a, b, (((1,), (1,)), ((), ())), preferred_element_type=F32)


def _split3(x):
    hi = x.astype(BF16)
    r1 = x - hi.astype(F32)
    mid = r1.astype(BF16)
    lo = (r1 - mid.astype(F32)).astype(BF16)
    return hi, mid, lo


def _tri_left(tri, x):
    hi, mid, lo = _split3(x)
    return _dot(tri, hi) + _dot(tri, mid) + _dot(tri, lo)


def _tri_right(x, tri):
    hi, mid, lo = _split3(x)
    return _dot(hi, tri) + _dot(mid, tri) + _dot(lo, tri)


def _drop_alias_refs(refs, n_in, n_alias):
    refs = list(refs)
    return refs[:n_in - n_alias] + refs[n_in:]


def _mm_kernel(*refs, rms, ep, cast_w, w_t):
    it = iter(refs)
    x_ref = next(it)
    w_ref = next(it)
    gain_ref = next(it) if rms else None
    m_ref = next(it) if ep in ("sig", "sigadd") else None
    prev_ref = next(it) if ep == "sigadd" else None
    res_ref = next(it) if ep == "resid" else None
    gate_ref = next(it) if ep == "resid" else None
    o_ref = next(it)
    wbf_ref = next(it) if cast_w else None

    if cast_w:
        @pl.when(pl.program_id(1) == 0)
        def _():
            wbf_ref[...] = (w_ref[0] if w_t else w_ref[...]).astype(BF16)
        w = wbf_ref[...]
    else:
        w = w_ref[...]
    x = x_ref[...]
    if rms:
        xf = x.astype(F32)
        ms = jnp.mean(xf * xf, axis=-1, keepdims=True)
        x = xf * lax.rsqrt(ms + EPS) * gain_ref[...]
    acc = _dot_nt(x.astype(BF16), w) if w_t else _dot(x.astype(BF16), w)
    if ep == "sig":
        acc = _sigmoid(m_ref[...]) * acc
    elif ep == "sigadd":
        acc = prev_ref[...].astype(F32) + _sigmoid(m_ref[...]) * acc
    elif ep == "resid":
        for r in range(gate_ref.shape[0]):
            rows = slice(r * SEQ_BLOCK, (r + 1) * SEQ_BLOCK)
            o_ref[rows, :] = res_ref[rows, :] + gate_ref[r] * acc[rows, :]
        return
    o_ref[...] = acc.astype(o_ref.dtype)


def _matmul(x, w, *, name, layer, n_out, tm, tn, out_dtype, w_col0=0, w_t=False, gain=None, x_col0=0, k=None,
            ep=None, m=None, m_col0=0, prev=None, res=None, gate_rows=None):
    M = x.shape[0]
    K = x.shape[1] if k is None else k
    assert M % tm == 0 and n_out % tn == 0 and m_col0 % tn == 0 and x_col0 % K == 0
    xj = x_col0 // K
    cast_w = w.dtype != BF16
    rms = gain is not None
    mj = m_col0 // tn
    if w_t:
        assert w_col0 % SUBLANES == 0
        w_spec = pl.BlockSpec((pl.Element(1), pl.Element(tn), pl.Element(K)),
                              lambda j, i: (layer, pl.multiple_of(w_col0 + j * tn, SUBLANES), 0))
    else:
        assert w_col0 % tn == 0
        wj = w_col0 // tn
        w_spec = pl.BlockSpec((None, K, tn), lambda j, i: (layer, 0, j + wj))

    in_specs = [pl.BlockSpec((tm, K), lambda j, i: (i, xj)), w_spec]
    args = [x, w]
    if rms:
        in_specs.append(pl.BlockSpec((1, K), lambda j, i: (0, 0)))
        args.append(gain.reshape(1, K))
    if ep in ("sig", "sigadd"):
        in_specs.append(pl.BlockSpec((tm, tn), lambda j, i: (i, j + mj)))
        args.append(m)
    if ep == "sigadd":
        in_specs.append(pl.BlockSpec((tm, tn), lambda j, i: (i, j)))
        args.append(prev)
    if ep == "resid":
        assert tm % SEQ_BLOCK == 0 and out_dtype == res.dtype
        in_specs.append(pl.BlockSpec((tm, tn), lambda j, i: (i, j)))
        args.append(res)
        in_specs.append(pl.BlockSpec((tm // SEQ_BLOCK, 1, tn), lambda j, i: (i, 0, j)))
        args.append(gate_rows)
    scratch = [pltpu.VMEM((tn, K) if w_t else (K, tn), BF16)] if cast_w else []
    return pl.pallas_call(
        functools.partial(_mm_kernel, rms=rms, ep=ep, cast_w=cast_w, w_t=w_t),
        grid=(n_out // tn, M // tm),
        in_specs=in_specs,
        out_specs=pl.BlockSpec((tm, tn), lambda j, i: (i, j)),
        out_shape=jax.ShapeDtypeStruct((M, n_out), out_dtype),
        scratch_shapes=scratch,
        compiler_params=_cparams(("arbitrary", "arbitrary")),
        name=name,
    )(*args)


def _prenorm_kernel(x_ref, nw_ref, mod_ref, o_ref):
    x = x_ref[...]
    ms = jnp.mean(x * x, axis=-1, keepdims=True)
    y = x * lax.rsqrt(ms + EPS) * nw_ref[...]
    shift = mod_ref[:, 0:D_MODEL]
    scale = mod_ref[:, D_MODEL:2 * D_MODEL]
    o_ref[...] = (y * (1.0 + scale) + shift).astype(o_ref.dtype)


def _prenorm(x, norm_w, mod_rows):
    M = x.shape[0]
    return pl.pallas_call(
        _prenorm_kernel,
        grid=(M // SEQ_BLOCK,),
        in_specs=[pl.BlockSpec((SEQ_BLOCK, D_MODEL), lambda i: (i, 0)),
                  pl.BlockSpec((1, D_MODEL), lambda i: (0, 0)),
                  pl.BlockSpec((None, 1, 3 * D_MODEL), lambda i: (i, 0, 0))],
        out_specs=pl.BlockSpec((SEQ_BLOCK, D_MODEL), lambda i: (i, 0)),
        out_shape=jax.ShapeDtypeStruct((M, D_MODEL), BF16),
        compiler_params=_cparams(("arbitrary",)),
        name="prenorm",
    )(x, norm_w.reshape(1, D_MODEL), mod_rows)


def _rmsnorm_kernel(x_ref, w_ref, o_ref):
    x = x_ref[...]
    ms = jnp.mean(x * x, axis=-1, keepdims=True)
    o_ref[...] = (x * lax.rsqrt(ms + EPS) * w_ref[...]).astype(o_ref.dtype)


def _rmsnorm(x, w, *, x_col0=0, width=None, tm=512):
    M = x.shape[0]
    width = x.shape[1] if width is None else width
    assert x_col0 % width == 0 and M % tm == 0
    cj = x_col0 // width
    return pl.pallas_call(
        _rmsnorm_kernel,
        grid=(M // tm,),
        in_specs=[pl.BlockSpec((tm, width), lambda i: (i, cj)),
                  pl.BlockSpec((1, width), lambda i: (0, 0))],
        out_specs=pl.BlockSpec((tm, width), lambda i: (i, 0)),
        out_shape=jax.ShapeDtypeStruct((M, width), F32),
        compiler_params=_cparams(("arbitrary",)),
        name="rmsnorm",
    )(x, w.reshape(1, width))


def _swap_pairs(x):
    n = x.shape[-1]
    nxt = pltpu.roll(x, n - 1, axis=1)
    prv = pltpu.roll(x, 1, axis=1)
    lane = lax.broadcasted_iota(jnp.int32, x.shape, 1)
    return jnp.where((lane & 1) == 0, nxt, prv)


def _mla_kernel(*refs, latent, n_alias):
    refs = _drop_alias_refs(refs, (11 if latent else 5) + n_alias, n_alias)
    it = iter(refs)
    qn_ref = next(it)
    qr_ref = next(it)
    g_ref = next(it)
    kv_ref = next(it)
    sm_ref = next(it)
    if latent:
        kvc_ref = next(it)
        krc_ref = next(it)
        cq_ref = next(it)
        sq_ref = next(it)
        ck_ref = next(it)
        sk_ref = next(it)
    o_ref = next(it)

    scale = float(QK_NOPE + QK_ROPE) ** -0.5
    qr = qr_ref[...]
    sm = sm_ref[...]
    if latent:
        qr = qr * cq_ref[...] + _swap_pairs(qr) * sq_ref[...]
        sm = sm * ck_ref[...] + _swap_pairs(sm) * sk_ref[...]
    kr = sm[:, SM["k_r"]:SM["k_r"] + QK_ROPE].astype(BF16)
    if latent:
        krc = krc_ref[...].astype(BF16)

    for h in range(MLA_HEADS):
        q_h = jnp.concatenate([qn_ref[:, h * QK_NOPE:(h + 1) * QK_NOPE],
                               qr[:, h * QK_ROPE:(h + 1) * QK_ROPE].astype(BF16)], axis=1)
        c0 = h * (QK_NOPE + V_HEAD)
        v_h = kv_ref[:, c0 + QK_NOPE:c0 + QK_NOPE + V_HEAD]
        s_own = _dot_nt(q_h, jnp.concatenate([kv_ref[:, c0:c0 + QK_NOPE], kr], axis=1)) * scale
        mx = jnp.max(s_own, axis=-1, keepdims=True)
        if latent:
            vc_h = kvc_ref[:, c0 + QK_NOPE:c0 + QK_NOPE + V_HEAD]
            s_ctx = _dot_nt(q_h, jnp.concatenate([kvc_ref[:, c0:c0 + QK_NOPE], krc], axis=1)) * scale
            mx = jnp.maximum(mx, jnp.max(s_ctx, axis=-1, keepdims=True))
            p_ctx = jnp.exp(s_ctx - mx)
        p_own = jnp.exp(s_own - mx)
        den = jnp.sum(p_own, axis=-1, keepdims=True)
        if latent:
            den = den + jnp.sum(p_ctx, axis=-1, keepdims=True)
        inv = 1.0 / den
        o_h = _dot((p_own * inv).astype(BF16), v_h)
        if latent:
            o_h = o_h + _dot((p_ctx * inv).astype(BF16), vc_h)
        g_h = g_ref[:, h * V_HEAD:(h + 1) * V_HEAD]
        o_ref[:, h * V_HEAD:(h + 1) * V_HEAD] = (o_h * _silu(g_h)).astype(o_ref.dtype)


def _mla(qn, qr, P, kv, *, n_tok, n_batch, S, row0, latent, kv_ctx_row0=0, krope_ctx=None, tables=None,
         out_prev=None):
    tq = min(S, 256)
    nq = S // tq
    rb = row0 // tq
    sb = row0 // S
    HW = MLA_HEADS * (QK_NOPE + V_HEAD)
    g_reg, g_col = SEG["g_mla"]
    gj = g_col // MLA_WIDTH
    smj = SMALL_COL0 // SMALL_W
    in_specs = [
        pl.BlockSpec((tq, MLA_HEADS * QK_NOPE), lambda b, i: (rb + b * nq + i, 0)),
        pl.BlockSpec((tq, MLA_HEADS * QK_ROPE), lambda b, i: (rb + b * nq + i, 0)),
        pl.BlockSpec((tq, MLA_WIDTH), lambda b, i: (rb + b * nq + i, gj)),
        pl.BlockSpec((S, HW), lambda b, i: (sb + b, 0)),
        pl.BlockSpec((S, SMALL_W), lambda b, i: (sb + b, smj)),
    ]
    args = [qn, qr, P[g_reg], kv, P["s"]]
    if latent:
        Tc = krope_ctx.shape[1]
        cb = kv_ctx_row0 // Tc
        cos_q, sin_q, cos_k, sin_k = tables
        in_specs += [
            pl.BlockSpec((Tc, HW), lambda b, i: (cb + b, 0)),
            pl.BlockSpec((None, Tc, QK_ROPE), lambda b, i: (b, 0, 0)),
            pl.BlockSpec((tq, MLA_HEADS * QK_ROPE), lambda b, i: (i, 0)),
            pl.BlockSpec((tq, MLA_HEADS * QK_ROPE), lambda b, i: (i, 0)),
            pl.BlockSpec((S, SMALL_W), lambda b, i: (0, 0)),
            pl.BlockSpec((S, SMALL_W), lambda b, i: (0, 0)),
        ]
        args += [kv, krope_ctx, cos_q, sin_q, cos_k, sin_k]
    aliases = {}
    if out_prev is not None:
        aliases[len(args)] = 0
        in_specs.append(pl.BlockSpec(memory_space=pl.ANY))
        args.append(out_prev)
    return pl.pallas_call(
        functools.partial(_mla_kernel, latent=latent, n_alias=len(aliases)),
        grid=(n_batch, nq),
        in_specs=in_specs,
        out_specs=pl.BlockSpec((tq, MLA_WIDTH), lambda b, i: (rb + b * nq + i, 0)),
        out_shape=jax.ShapeDtypeStruct((n_tok, MLA_WIDTH), BF16),
        input_output_aliases=aliases,
        compiler_params=_cparams(("arbitrary", "arbitrary")),
        name="mla_latent" if latent else "mla_context",
    )(*args)


def _rope_tables(S):
    rows = jnp.repeat(jnp.arange(S // GRID_W, dtype=F32), GRID_W)
    cols = jnp.tile(jnp.arange(GRID_W, dtype=F32), S // GRID_W)
    n_freq = QK_ROPE // 4
    inv = ROPE_BASE ** (-jnp.arange(n_freq, dtype=F32) / n_freq)
    ang = jnp.concatenate([rows[:, None] * inv, cols[:, None] * inv], axis=-1)
    cos = jnp.repeat(jnp.cos(ang), 2, axis=-1)
    sin = jnp.repeat(jnp.sin(ang), 2, axis=-1)
    sign = jnp.tile(jnp.array([-1.0, 1.0], F32), QK_ROPE // 2)
    sin = sin * sign
    cos_q = jnp.tile(cos, (1, MLA_HEADS))
    sin_q = jnp.tile(sin, (1, MLA_HEADS))
    lo, hi = SM["k_r"], SMALL_W - SM["k_r"] - QK_ROPE
    cos_k = jnp.concatenate([jnp.ones((S, lo), F32), cos, jnp.ones((S, hi), F32)], axis=-1)
    sin_k = jnp.concatenate([jnp.zeros((S, lo), F32), sin, jnp.zeros((S, hi), F32)], axis=-1)
    return cos_q, sin_q, cos_k, sin_k


def _gla_kernel(*refs, S, C, has_state, emit_state, n_alias):
    refs = _drop_alias_refs(refs, 8 + (2 if has_state else 0) + n_alias, n_alias)
    it = iter(refs)
    q_ref = next(it)
    k_ref = next(it)
    v_ref = next(it)
    sm_ref = next(it)
    gg_ref = next(it)
    w2_ref = next(it)
    b2_ref = next(it)
    nw_ref = next(it)
    s0_refs = (next(it), next(it)) if has_state else None
    o_ref = next(it)
    so_refs = (next(it), next(it)) if emit_state else None
    vt_ref = next(it)
    la_ref = next(it)
    acc_ref = next(it)
    st_ref = next(it)

    n = S // C
    scale = float(GLA_DK) ** -0.5
    sm = sm_ref[...]
    for d, name in enumerate(("ga_f", "ga_b")):
        ga = sm[:, SM[name]:SM[name] + GLA_GATE_RANK].astype(BF16)
        xg = _dot(ga, w2_ref[d].astype(BF16)) + b2_ref[d]
        la_ref[d] = -_softplus(-xg) * (1.0 / GLA_GATE_NORM)
    for c in range(n):
        vt_ref[c] = v_ref[c * C:(c + 1) * C, :].T.astype(BF16)
    for d in range(2):
        if has_state:
            st_ref[d] = s0_refs[d][...].T
        else:
            st_ref[d] = jnp.zeros((GLA_DV, GLA_DK), F32)

    row = lax.broadcasted_iota(jnp.int32, (C, C), 0)
    col = lax.broadcasted_iota(jnp.int32, (C, C), 1)
    lower = row >= col
    upper = row <= col
    tri = (jnp.where(lower, 1.0, 0.0).astype(BF16), jnp.where(upper, 1.0, 0.0).astype(BF16))
    mid = C // 2

    def chunk(c, d):
        r0 = pl.multiple_of(c * C, C)
        q = q_ref[pl.ds(r0, C), :] * scale
        k = k_ref[pl.ds(r0, C), :]
        v = v_ref[pl.ds(r0, C), :].astype(BF16)
        g = la_ref[d, pl.ds(r0, C), :]
        b = _tri_left(tri[d], g)
        tot = b[C - 1:C, :] if d == 0 else b[0:1, :]
        bm = b[mid:mid + 1, :]
        st = st_ref[d]
        o = _dot_nt((q * jnp.exp(b)).astype(BF16), st.astype(BF16))
        qa = (q * jnp.exp(b - bm)).astype(BF16)
        ka = (k * jnp.exp(bm - b)).astype(BF16)
        att = _dot_nt(qa, ka)
        att = jnp.where(lower if d == 0 else upper, att, 0.0)
        o = o + _dot(att.astype(BF16), v)
        kd = (k * jnp.exp(tot - b)).astype(BF16)
        st_ref[d] = st * jnp.exp(tot) + _dot(vt_ref[c], kd)
        acc_ref[d, pl.ds(r0, C), :] = o

    def both(i, carry):
        chunk(i, 0)
        chunk(n - 1 - i, 1)
        return carry

    lax.fori_loop(0, n, both, 0)

    o = acc_ref[0] + acc_ref[1]
    ms = jnp.mean(o * o, axis=-1, keepdims=True)
    y = o * lax.rsqrt(ms + EPS) * nw_ref[...]
    o_ref[...] = (y * _silu(gg_ref[...])).astype(o_ref.dtype)
    if emit_state:
        for d in range(2):
            so_refs[d][...] = st_ref[d].T


def _gla(P, w2, b2, nw, *, layer, n_tok, n_batch, S, row0, states_in=None, states_out=None, out_prev=None):
    C = GLA_CHUNK
    sb = row0 // S
    col = {n: SEG[n][1] for n in ("q_l", "k_l", "v_l", "g_gla")}
    qj, kj, vj, gj = col["q_l"] // GLA_DK, col["k_l"] // GLA_DK, col["v_l"] // GLA_DV, col["g_gla"] // GLA_DV
    smj = SMALL_COL0 // SMALL_W
    in_specs = [
        pl.BlockSpec((S, GLA_DK), lambda b, h: (sb + b, qj + h)),
        pl.BlockSpec((S, GLA_DK), lambda b, h: (sb + b, kj + h)),
        pl.BlockSpec((S, GLA_DV), lambda b, h: (sb + b, vj + h)),
        pl.BlockSpec((S, SMALL_W), lambda b, h: (sb + b, smj)),
        pl.BlockSpec((S, GLA_DV), lambda b, h: (sb + b, gj + h)),
        pl.BlockSpec((None, 2, GLA_GATE_RANK, GLA_DK), lambda b, h: (layer, 0, 0, h)),
        pl.BlockSpec((None, 2, 1, GLA_DK), lambda b, h: (layer, 0, 0, h)),
        pl.BlockSpec((None, 1, GLA_DV), lambda b, h: (layer, 0, 0)),
    ]
    args = [P[SEG["q_l"][0]], P[SEG["k_l"][0]], P[SEG["v_l"][0]], P["s"], P[SEG["g_gla"][0]],
            w2, b2.reshape(DEPTH, 2, 1, GLA_KEY), nw.reshape(DEPTH, 1, GLA_DV)]
    has_state = states_in is not None
    st_block = (None, None, None, GLA_DK, GLA_DV)
    if has_state:
        for s in states_in:
            in_specs.append(pl.BlockSpec(st_block, lambda b, h: (b, layer, h, 0, 0)))
            args.append(s)
    out_specs = [pl.BlockSpec((S, GLA_DV), lambda b, h: (sb + b, h))]
    out_shape = [jax.ShapeDtypeStruct((n_tok, GLA_WIDTH), BF16)]
    aliased = [out_prev]
    emit_state = states_out is not None
    if emit_state:
        for s in states_out:
            out_specs.append(pl.BlockSpec(st_block, lambda b, h: (b, layer, h, 0, 0)))
            out_shape.append(jax.ShapeDtypeStruct((n_batch, DEPTH, GLA_HEADS, GLA_DK, GLA_DV), F32))
            aliased.append(s)
    aliases = {}
    for oi, buf in enumerate(aliased):
        if buf is not None:
            aliases[len(args)] = oi
            in_specs.append(pl.BlockSpec(memory_space=pl.ANY))
            args.append(buf)
    n = S // C
    return pl.pallas_call(
        functools.partial(_gla_kernel, S=S, C=C, has_state=has_state, emit_state=emit_state,
                          n_alias=len(aliases)),
        grid=(n_batch, GLA_HEADS),
        in_specs=in_specs,
        out_specs=out_specs,
        out_shape=out_shape,
        input_output_aliases=aliases,
        scratch_shapes=[pltpu.VMEM((n, GLA_DV, C), BF16), pltpu.VMEM((2, S, GLA_DK), F32),
                        pltpu.VMEM((2, S, GLA_DV), F32), pltpu.VMEM((2, GLA_DV, GLA_DK), F32)],
        compiler_params=_cparams(("arbitrary", "arbitrary")),
        name="gla_latent" if has_state else "gla_context",
    )(*args)


CONV_PAD = SUBLANES
CONV_ROWS = 256


def _conv_silu(dst_ref, src_ref, w_ref, b_ref, xp_ref, S):
    W = src_ref.shape[1]
    left = (SSM_CONV - 1) // 2
    xp_ref[0:CONV_PAD, 0:W] = jnp.zeros((CONV_PAD, W), F32)
    xp_ref[CONV_PAD + S:CONV_PAD + S + CONV_PAD, 0:W] = jnp.zeros((CONV_PAD, W), F32)
    xp_ref[CONV_PAD:CONV_PAD + S, 0:W] = src_ref[...]
    for r in range(0, S, CONV_ROWS):
        acc = b_ref[...] + xp_ref[CONV_PAD - left + r:CONV_PAD - left + r + CONV_ROWS, 0:W] * w_ref[0:1, :]
        for j in range(1, SSM_CONV):
            r0 = CONV_PAD - left + j + r
            acc = acc + xp_ref[r0:r0 + CONV_ROWS, 0:W] * w_ref[j:j + 1, :]
        dst_ref[r:r + CONV_ROWS, :] = _silu(acc)
def _ssd_kernel(*refs, S, C, has_state, emit_state, n_alias):
    refs = _drop_alias_refs(refs, 15 + (2 if has_state else 0) + n_alias, n_alias)
    it = iter(refs)
    xin_ref = next(it)
    bin_ref = next(it)
    cin_ref = next(it)
    cw_refs = (next(it), next(it), next(it))
    cb_refs = (next(it), next(it), next(it))
    z_ref = next(it)
    dtc_ref = next(it)
    dtr_ref = next(it)
    pc_ref = next(it)
    pr_ref = next(it)
    d_ref = next(it)
    s0_refs = (next(it), next(it)) if has_state else None
    y_ref = next(it)
    so_refs = (next(it), next(it)) if emit_state else None
    bt_ref = next(it)
    acc_ref = next(it)
    st_ref = next(it)
    xp_ref = next(it)
    xh_ref = next(it)
    x_ref = next(it)
    b_ref = next(it)
    c_ref = next(it)

    for dst, src, w, b in zip((x_ref, b_ref, c_ref), (xin_ref, bin_ref, cin_ref), cw_refs, cb_refs):
        _conv_silu(dst, src, w, b, xp_ref, S)

    n = S // C
    P = SSM_HEADDIM
    for r in range(0, S, CONV_ROWS):
        xr = x_ref[r:r + CONV_ROWS, :]
        for j in range(SSM_HPG):
            xh_ref[j, r:r + CONV_ROWS, :] = xr[:, j * P:(j + 1) * P].astype(BF16)
    for c in range(n):
        bt_ref[c] = b_ref[c * C:(c + 1) * C, :].T
    for d in range(2):
        if has_state:
            s0_t = jnp.concatenate([s0_refs[d][j] for j in range(SSM_HPG)], axis=0).T
            for j in range(SSM_HPG):
                st_ref[d, j] = s0_t[:, j * P:(j + 1) * P]
        else:
            st_ref[d] = jnp.zeros((SSM_HPG, SSM_STATE, P), F32)

    row = lax.broadcasted_iota(jnp.int32, (C, C), 0)
    col = lax.broadcasted_iota(jnp.int32, (C, C), 1)
    lower = row >= col
    upper = row <= col
    tri_l = jnp.where(lower, 1.0, 0.0).astype(BF16)
    tri_u = jnp.where(upper, 1.0, 0.0).astype(BF16)

    def chunk(c, d):
        r0 = pl.multiple_of(c * C, C)
        bm = b_ref[pl.ds(r0, C), :].astype(BF16)
        cm = c_ref[pl.ds(r0, C), :]
        bt = bt_ref[c]
        a_c = -jnp.exp(pc_ref[d, 1])
        a_r = -jnp.exp(pr_ref[d, 1])
        dt_c = _softplus(dtc_ref[d, pl.ds(r0, C), :] + pc_ref[d, 0])
        dt_r = _softplus(dtr_ref[d, c] + pr_ref[d, 0])
        if d == 0:
            cum_c = _tri_left(tri_l, dt_c * a_c)
            cum_r = _tri_right(dt_r * a_r, tri_u)
        else:
            cum_c = _tri_left(tri_u, dt_c * a_c)
            cum_r = _tri_right(dt_r * a_r, tri_l)
        mask = lower if d == 0 else upper
        last_r = cum_r[:, C - 1:C] if d == 0 else cum_r[:, 0:1]
        u_r = dt_r * jnp.exp(last_r - cum_r)
        e_last = jnp.exp(last_r)
        cb = _dot_nt(cm.astype(BF16), bm)
        for j in range(SSM_HPG):
            cum_q = jnp.broadcast_to(cum_c[:, j:j + 1], (C, C))
            dec = jnp.where(mask, jnp.exp(cum_q - cum_r[j:j + 1, :]), 0.0)
            lhs = jnp.concatenate([(cb * dec * dt_r[j:j + 1, :]).astype(BF16),
                                   (cm * jnp.exp(cum_q)).astype(BF16)], axis=1)
            x_j = xh_ref[j, pl.ds(r0, C), :]
            st_j = st_ref[d, j]
            acc_ref[d, j, pl.ds(r0, C), :] = _dot(lhs, jnp.concatenate([x_j, st_j.astype(BF16)], axis=0))
            st_ref[d, j] = st_j * e_last[j:j + 1, :] + _dot((bt * u_r[j:j + 1, :]).astype(BF16), x_j)

    def both(i, carry):
        chunk(i, 0)
        chunk(n - 1 - i, 1)
        return carry

    lax.fori_loop(0, n, both, 0)

    for r in range(0, S, CONV_ROWS):
        rows = slice(r, r + CONV_ROWS)
        y = jnp.concatenate([acc_ref[0, j, rows, :] + acc_ref[1, j, rows, :] for j in range(SSM_HPG)], axis=-1)
        y = y + x_ref[rows, :] * d_ref[...]
        y_ref[rows, :] = y * _silu(z_ref[rows, :])
    if emit_state:
        for d in range(2):
            s_t = jnp.concatenate([st_ref[d, j] for j in range(SSM_HPG)], axis=1).T
            for j in range(SSM_HPG):
                so_refs[d][j] = s_t[j * P:(j + 1) * P, :]


def _ssd(P, cw, cb, dt_col, dt_row, p_col, p_row, d_lane, *, layer, n_tok, n_batch, S, row0,
         states_in=None, states_out=None, out_prev=None):
    C = SSM_CHUNK
    sb = row0 // S
    z_reg, z_col = SEG["z"]
    x_reg, x_col = SEG["xbc"]
    zj = z_col // SSM_GW
    xj = x_col // SSM_GW
    bj = SSM_WIDTH // SSM_STATE
    cj = (SSM_WIDTH + SSM_BC) // SSM_STATE
    xbj = x_col // SSM_STATE

    def taps(width, j0):
        return pl.BlockSpec((None, SSM_CONV, width), lambda b, g: (layer, 0, j0 + g))

    def bias(width, j0):
        return pl.BlockSpec((None, 1, width), lambda b, g: (layer, 0, j0 + g))

    in_specs = [
        pl.BlockSpec((S, SSM_GW), lambda b, g: (sb + b, xj + g)),
        pl.BlockSpec((S, SSM_STATE), lambda b, g: (sb + b, xbj + bj + g)),
        pl.BlockSpec((S, SSM_STATE), lambda b, g: (sb + b, xbj + cj + g)),
        taps(SSM_GW, 0), taps(SSM_STATE, bj), taps(SSM_STATE, cj),
        bias(SSM_GW, 0), bias(SSM_STATE, bj), bias(SSM_STATE, cj),
        pl.BlockSpec((S, SSM_GW), lambda b, g: (sb + b, zj + g)),
        pl.BlockSpec((2, None, S, SSM_HPG), lambda b, g: (0, g, sb + b, 0)),
        pl.BlockSpec((2, None, S // C, SSM_HPG, C), lambda b, g: (0, g, sb + b, 0, 0)),
        pl.BlockSpec((None, 2, 2, None, 1, SSM_HPG), lambda b, g: (layer, 0, 0, g, 0, 0)),
        pl.BlockSpec((None, 2, 2, None, SSM_HPG, 1), lambda b, g: (layer, 0, 0, g, 0, 0)),
        pl.BlockSpec((None, 1, SSM_GW), lambda b, g: (layer, 0, g)),
    ]
    cb3 = cb.reshape(DEPTH, 1, SSM_CONV_CH)
    args = [P[x_reg], P[x_reg], P[x_reg], cw, cw, cw, cb3, cb3, cb3, P[z_reg], dt_col, dt_row, p_col, p_row,
            d_lane]
    has_state = states_in is not None
    st_block = (None, None, SSM_HPG, SSM_HEADDIM, SSM_STATE)
    if has_state:
        for s in states_in:
            in_specs.append(pl.BlockSpec(st_block, lambda b, g: (b, layer, g, 0, 0)))
            args.append(s)
    out_specs = [pl.BlockSpec((S, SSM_GW), lambda b, g: (sb + b, g))]
    out_shape = [jax.ShapeDtypeStruct((n_tok, SSM_WIDTH), F32)]
    aliased = [out_prev]
    emit_state = states_out is not None
    if emit_state:
        for s in states_out:
            out_specs.append(pl.BlockSpec(st_block, lambda b, g: (b, layer, g, 0, 0)))
            out_shape.append(jax.ShapeDtypeStruct((n_batch, DEPTH, SSM_HEADS, SSM_HEADDIM, SSM_STATE), F32))
            aliased.append(s)
    aliases = {}
    for oi, buf in enumerate(aliased):
        if buf is not None:
            aliases[len(args)] = oi
            in_specs.append(pl.BlockSpec(memory_space=pl.ANY))
            args.append(buf)
    n = S // C
    return pl.pallas_call(
        functools.partial(_ssd_kernel, S=S, C=C, has_state=has_state, emit_state=emit_state,
                          n_alias=len(aliases)),
        grid=(n_batch, SSM_GROUPS),
        in_specs=in_specs,
        out_specs=out_specs,
        out_shape=out_shape,
        input_output_aliases=aliases,
        scratch_shapes=[pltpu.VMEM((n, SSM_STATE, C), F32), pltpu.VMEM((2, SSM_HPG, S, SSM_HEADDIM), F32),
                        pltpu.VMEM((2, SSM_HPG, SSM_STATE, SSM_HEADDIM), F32),
                        pltpu.VMEM((S + 2 * CONV_PAD, SSM_GW), F32),
                        pltpu.VMEM((SSM_HPG, S, SSM_HEADDIM), BF16), pltpu.VMEM((S, SSM_GW), F32),
                        pltpu.VMEM((S, SSM_STATE), F32), pltpu.VMEM((S, SSM_STATE), F32)],
        compiler_params=_cparams(("arbitrary", "arbitrary")),
        name="ssd_latent" if has_state else "ssd_context",
    )(*args)


def _gather_small_w_in(w_in_t):
    names = REGIONS["s"]
    rows = [w_in_t[:, _IN_OFF[n]:_IN_OFF[n] + _IN_SIZE[n], :] for n in names]
    used = sum(_IN_SIZE[n] for n in names)
    rows.append(jnp.zeros((DEPTH, REGION_W["s"] - used, D_MODEL), w_in_t.dtype))
    return jnp.concatenate(rows, axis=1)


def kernel(x_prompt, x_sample, cache_mla_ckv, cache_mla_krope, state_gla_fwd, state_gla_bwd,
           state_ssm_fwd, state_ssm_bwd, c, c_ctx, w_mod, b_mod, norm_w, w_in, q_a_norm, w_q_b,
           kv_a_norm, w_kv_b, gla_w_gate2, gla_b_gate, gla_norm, ssm_conv_w, ssm_conv_b,
           ssm_dt_bias, ssm_a_log, ssm_d, ssm_norm, w_br_mla, w_br_gla, w_br_ssm, w_out, final_norm):
    nb_p, s_p, _ = x_prompt.shape
    nb_l, s_l, _ = x_sample.shape
    n_p = nb_p * s_p
    n_l = nb_l * s_l
    n_tok = n_p + n_l
    assert s_p == SEQ_BLOCK and s_l % SEQ_BLOCK == 0 and n_p % s_l == 0
    t_c = cache_mla_ckv.shape[2]
    assert t_c == SEQ_BLOCK

    x = jnp.concatenate([x_prompt.reshape(n_p, D_MODEL), x_sample.reshape(n_l, D_MODEL)], axis=0)

    w_in_t = jnp.swapaxes(w_in, 1, 2)
    w_in_s = _gather_small_w_in(w_in_t)
    wq = w_q_b.reshape(DEPTH, Q_LORA, MLA_HEADS, QK_NOPE + QK_ROPE)
    wq_nope = wq[..., :QK_NOPE].reshape(DEPTH, Q_LORA, MLA_HEADS * QK_NOPE)
    wq_rope = wq[..., QK_NOPE:].reshape(DEPTH, Q_LORA, MLA_HEADS * QK_ROPE)
    tables = _rope_tables(s_l)
    p_ssm = jnp.stack([ssm_dt_bias, ssm_a_log], axis=2).reshape(DEPTH, 2, 2, SSM_GROUPS, SSM_HPG)
    p_col = p_ssm[:, :, :, :, None, :]
    p_row = p_ssm[:, :, :, :, :, None]
    d_lane = jnp.repeat(ssm_d, SSM_HEADDIM, axis=-1).reshape(DEPTH, 1, SSM_WIDTH)
    blk_mod = np.concatenate([np.zeros(n_p // SEQ_BLOCK, np.int32),
                              1 + np.repeat(np.arange(nb_l, dtype=np.int32), s_l // SEQ_BLOCK)])
    cond = jnp.concatenate([c_ctx[None, :], c], axis=0)
    cond = _silu(cond)
    cond = jnp.concatenate([cond, jnp.zeros((8 - cond.shape[0] % 8, D_MODEL), F32)], axis=0)

    gla_f = gla_b = ssm_f = ssm_b = None
    ckv_l, kr_l = [], []
    for l in range(DEPTH):
        mod = _matmul(cond, w_mod, name="mm_mod", layer=l, n_out=3 * D_MODEL, tm=cond.shape[0], tn=1536,
                      out_dtype=F32)
        mod = mod[:1 + nb_l] + b_mod[l][None, :]
        mod_rows = mod[blk_mod][:, None, :]
        gate_rows = mod_rows[:, :, 2 * D_MODEL:]

        h = _prenorm(x, norm_w[l], mod_rows)
        P = {}
        for r in ("a", "b", "c"):
            first = REGIONS[r][0]
            P[r] = _matmul(h, w_in_t, name="mm_in_" + r, layer=l, w_t=True, w_col0=_IN_OFF[first],
                           n_out=REGION_W[r], tm=512, tn=IN_TN, out_dtype=F32)
        P["s"] = _matmul(h, w_in_s, name="mm_in_s", layer=l, w_t=True, n_out=REGION_W["s"], tm=512, tn=IN_TN,
                         out_dtype=F32)
        small = P["s"][:, SMALL_COL0:SMALL_COL0 + SMALL_W]

        qa_col, kva_col = SEG["q_a"][1], SEG["kv_a"][1]
        qn = _matmul(P["s"], wq_nope, name="mm_q_nope", x_col0=qa_col, k=Q_LORA, layer=l,
                     n_out=MLA_HEADS * QK_NOPE, tm=1024, tn=1024, out_dtype=BF16, gain=q_a_norm[l])
        qr = _matmul(P["s"], wq_rope, name="mm_q_rope", x_col0=qa_col, k=Q_LORA, layer=l,
                     n_out=MLA_HEADS * QK_ROPE, tm=1024, tn=1024, out_dtype=F32, gain=q_a_norm[l])
        ckv = _rmsnorm(P["s"], kv_a_norm[l], x_col0=kva_col, width=KV_LORA)
        ckv_all = jnp.concatenate([ckv, cache_mla_ckv[:, l].reshape(nb_l * t_c, KV_LORA)], axis=0)
        kv = _matmul(ckv_all, w_kv_b, name="mm_kv", layer=l, n_out=MLA_HEADS * (QK_NOPE + V_HEAD),
                     tm=ckv_all.shape[0] // 4, tn=1024, out_dtype=BF16)
        o_mla = _mla(qn, qr, P, kv, n_tok=n_tok, n_batch=nb_p, S=s_p, row0=0, latent=False)
        o_mla = _mla(qn, qr, P, kv, n_tok=n_tok, n_batch=nb_l, S=s_l, row0=n_p, latent=True,
                     kv_ctx_row0=n_tok, krope_ctx=cache_mla_krope[:, l], tables=tables, out_prev=o_mla)
        ckv_l.append(ckv[:n_p].reshape(nb_p, s_p, KV_LORA))
        kr_l.append(small[:n_p, SM["k_r"]:SM["k_r"] + QK_ROPE].reshape(nb_p, s_p, QK_ROPE))

        o_gla, gla_f, gla_b = _gla(P, gla_w_gate2, gla_b_gate, gla_norm, layer=l, n_tok=n_tok, n_batch=nb_p,
                                   S=s_p, row0=0, states_out=(gla_f, gla_b))
        (o_gla,) = _gla(P, gla_w_gate2, gla_b_gate, gla_norm, layer=l, n_tok=n_tok, n_batch=nb_l, S=s_l,
                        row0=n_p, states_in=(state_gla_fwd, state_gla_bwd), out_prev=o_gla)

        dt = small[:, SM["dt_f"]:SM["dt_f"] + 2 * SSM_HEADS].reshape(n_tok, 2, SSM_GROUPS, SSM_HPG)
        dt_col = jnp.transpose(dt, (1, 2, 0, 3))
        dt_row = jnp.transpose(dt.reshape(n_tok // SSM_CHUNK, SSM_CHUNK, 2, SSM_GROUPS, SSM_HPG),
                               (2, 3, 0, 4, 1))
        yz, ssm_f, ssm_b = _ssd(P, ssm_conv_w, ssm_conv_b, dt_col, dt_row, p_col, p_row, d_lane, layer=l,
                                n_tok=n_tok, n_batch=nb_p, S=s_p, row0=0, states_out=(ssm_f, ssm_b))
        (yz,) = _ssd(P, ssm_conv_w, ssm_conv_b, dt_col, dt_row, p_col, p_row, d_lane, layer=l, n_tok=n_tok,
                     n_batch=nb_l, S=s_l, row0=n_p, states_in=(state_ssm_fwd, state_ssm_bwd), out_prev=yz)

        mg = _matmul(o_mla, w_br_mla, name="mm_br_mla", layer=l, n_out=D_MODEL, tm=512, tn=1024, out_dtype=F32,
                     ep="sig", m=P["c"], m_col0=SEG["m_mla"][1])
        mg = _matmul(o_gla, w_br_gla, name="mm_br_gla", layer=l, n_out=D_MODEL, tm=512, tn=1024, out_dtype=F32,
                     ep="sigadd", m=P["c"], m_col0=SEG["m_gla"][1], prev=mg)
        mg = _matmul(yz, w_br_ssm, name="mm_br_ssm", layer=l, n_out=D_MODEL, tm=512, tn=512, out_dtype=BF16,
                     gain=ssm_norm[l], ep="sigadd", m=P["c"], m_col0=SEG["m_ssm"][1], prev=mg)
        x = _matmul(mg, w_out, name="mm_out", layer=l, n_out=D_MODEL, tm=512, tn=1024, out_dtype=F32,
                    ep="resid", res=x, gate_rows=gate_rows)

    y = _rmsnorm(x, final_norm)
    y_prompt = y[:n_p].reshape(nb_p, s_p, D_MODEL)
    y_sample = y[n_p:].reshape(nb_l, s_l, D_MODEL)
    return (y_prompt, y_sample, jnp.stack(ckv_l, axis=1), jnp.stack(kr_l, axis=1),
            gla_f, gla_b, ssm_f, ssm_b)
```

```python
import functools

import jax
import jax.numpy as jnp
import numpy as np
from jax import lax
from jax.experimental import pallas as pl
from jax.experimental.pallas import tpu as pltpu

F32 = jnp.float32
BF16 = jnp.bfloat16

D_MODEL = 2048
DEPTH = 4
GRID_W = 64
EPS = 1e-6
LOG2_E = 1.4426950408889634
MLA_HEADS = 16
QK_NOPE = 128
QK_ROPE = 64
V_HEAD = 128
Q_LORA = 512
KV_LORA = 256
MLA_WIDTH = MLA_HEADS * V_HEAD
ROPE_BASE = 10000.0
GLA_HEADS = 4
GLA_DK = 256
GLA_DV = 512
GLA_KEY = GLA_HEADS * GLA_DK
GLA_WIDTH = GLA_HEADS * GLA_DV
GLA_GATE_RANK = 16
GLA_GATE_NORM = 16.0
SSM_HEADS = 64
SSM_HEADDIM = 64
SSM_WIDTH = SSM_HEADS * SSM_HEADDIM
SSM_GROUPS = 8
SSM_HPG = SSM_HEADS // SSM_GROUPS
SSM_GW = SSM_HPG * SSM_HEADDIM
SSM_STATE = 128
SSM_CONV = 5
SSM_BC = SSM_GROUPS * SSM_STATE
SSM_CONV_CH = SSM_WIDTH + 2 * SSM_BC

IN_SIZES = (Q_LORA, KV_LORA, QK_ROPE, MLA_WIDTH,
            GLA_KEY, GLA_KEY, GLA_WIDTH, GLA_GATE_RANK, GLA_GATE_RANK, GLA_WIDTH,
            SSM_WIDTH, SSM_CONV_CH, SSM_HEADS, SSM_HEADS,
            D_MODEL, D_MODEL, D_MODEL)
IN_NAMES = ("q_a", "kv_a", "k_r", "g_mla", "q_l", "k_l", "v_l", "ga_f", "ga_b", "g_gla",
            "z", "xbc", "dt_f", "dt_b", "m_mla", "m_gla", "m_ssm")
_IN_OFF = dict(zip(IN_NAMES, np.cumsum((0,) + IN_SIZES[:-1]).tolist()))
_IN_SIZE = dict(zip(IN_NAMES, IN_SIZES))

REGIONS = {
    "a": ("g_mla", "q_l", "k_l", "v_l"),
    "b": ("g_gla", "z", "xbc"),
    "c": ("m_mla", "m_gla", "m_ssm"),
    "s": ("q_a", "kv_a", "k_r", "ga_f", "ga_b", "dt_f", "dt_b"),
}
IN_TN = 1024
SEG = {}
REGION_W = {}
for _r, _names in REGIONS.items():
    _o = 0
    for _n in _names:
        SEG[_n] = (_r, _o)
        _o += _IN_SIZE[_n]
    REGION_W[_r] = -(-_o // IN_TN) * IN_TN
    if _r != "s":
        assert _o % IN_TN == 0
        assert all(_IN_OFF[_names[i]] + _IN_SIZE[_names[i]] == _IN_OFF[_names[i + 1]]
                   for i in range(len(_names) - 1))
SMALL_W = 256
SMALL_COL0 = SEG["k_r"][1]
assert SMALL_COL0 % SMALL_W == 0 and SEG["dt_b"][1] + _IN_SIZE["dt_b"] <= SMALL_COL0 + SMALL_W
SM = {n: SEG[n][1] - SMALL_COL0 for n in ("k_r", "ga_f", "ga_b", "dt_f", "dt_b")}

SUBLANES = 8
SEQ_BLOCK = 256
GLA_CHUNK = 128
SSM_CHUNK = 128
V7X_VMEM_LIMIT = 56 * 1024 * 1024


def _cparams(sem):
    return pltpu.CompilerParams(dimension_semantics=sem, vmem_limit_bytes=V7X_VMEM_LIMIT)


def _silu(x):
    return x * (1.0 / (1.0 + jnp.exp(-x)))


def _sigmoid(x):
    return 1.0 / (1.0 + jnp.exp(-x))


def _softplus(x):
    return jnp.maximum(x, 0.0) + jnp.log(1.0 + jnp.exp(-jnp.abs(x)))


def _dot(a, b):
    return jnp.dot(a, b, preferred_element_type=F32)


def _dot_nt(a, b):
    return lax.dot_general(a, b, (((1,), (1,)), ((), ())), preferred_element_type=F32)


def _split3(x):
    hi = x.astype(BF16)
    r1 = x - hi.astype(F32)
    mid = r1.astype(BF16)
    lo = (r1 - mid.astype(F32)).astype(BF16)
    return hi, mid, lo


def _tri_left(tri, x):
    hi, mid, lo = _split3(x)
    return _dot(tri, hi) + _dot(tri, mid) + _dot(tri, lo)


def _tri_right(x, tri):
    hi, mid, lo = _split3(x)
    return _dot(hi, tri) + _dot(mid, tri) + _dot(lo, tri)


def _drop_alias_refs(refs, n_in, n_alias):
    refs = list(refs)
    return refs[:n_in - n_alias] + refs[n_in:]


def _mm_kernel(*refs, rms, ep, cast_w, w_t):
    it = iter(refs)
    x_ref = next(it)
    w_ref = next(it)
    gain_ref = next(it) if rms else None
    m_ref = next(it) if ep in ("sig", "sigadd") else None
    prev_ref = next(it) if ep == "sigadd" else None
    res_ref = next(it) if ep == "resid" else None
    gate_ref = next(it) if ep == "resid" else None
    o_ref = next(it)
    wbf_ref = next(it) if cast_w else None

    if cast_w:
        @pl.when(pl.program_id(1) == 0)
        def _():
            wbf_ref[...] = (w_ref[0] if w_t else w_ref[...]).astype(BF16)
        w = wbf_ref[...]
    else:
        w = w_ref[...]
    x = x_ref[...]
    if rms:
        xf = x.astype(F32)
        ms = jnp.mean(xf * xf, axis=-1, keepdims=True)
        x = xf * lax.rsqrt(ms + EPS) * gain_ref[...]
    acc = _dot_nt(x.astype(BF16), w) if w_t else _dot(x.astype(BF16), w)
    if ep == "sig":
        acc = _sigmoid(m_ref[...]) * acc
    elif ep == "sigadd":
        acc = prev_ref[...].astype(F32) + _sigmoid(m_ref[...]) * acc
    elif ep == "resid":
        for r in range(gate_ref.shape[0]):
            rows = slice(r * SEQ_BLOCK, (r + 1) * SEQ_BLOCK)
            o_ref[rows, :] = res_ref[rows, :] + gate_ref[r] * acc[rows, :]
        return
    o_ref[...] = acc.astype(o_ref.dtype)


def _matmul(x, w, *, name, layer, n_out, tm, tn, out_dtype, w_col0=0, w_t=False, gain=None, x_col0=0, k=None,
            ep=None, m=None, m_col0=0, prev=None, res=None, gate_rows=None):
    M = x.shape[0]
    K = x.shape[1] if k is None else k
    assert M % tm == 0 and n_out % tn == 0 and m_col0 % tn == 0 and x_col0 % K == 0
    xj = x_col0 // K
    cast_w = w.dtype != BF16
    rms = gain is not None
    mj = m_col0 // tn
    if w_t:
        assert w_col0 % SUBLANES == 0
        w_spec = pl.BlockSpec((pl.Element(1), pl.Element(tn), pl.Element(K)),
                              lambda j, i: (layer, pl.multiple_of(w_col0 + j * tn, SUBLANES), 0))
    else:
        assert w_col0 % tn == 0
        wj = w_col0 // tn
        w_spec = pl.BlockSpec((None, K, tn), lambda j, i: (layer, 0, j + wj))

    in_specs = [pl.BlockSpec((tm, K), lambda j, i: (i, xj)), w_spec]
    args = [x, w]
    if rms:
        in_specs.append(pl.BlockSpec((1, K), lambda j, i: (0, 0)))
        args.append(gain.reshape(1, K))
    if ep in ("sig", "sigadd"):
        in_specs.append(pl.BlockSpec((tm, tn), lambda j, i: (i, j + mj)))
        args.append(m)
    if ep == "sigadd":
        in_specs.append(pl.BlockSpec((tm, tn), lambda j, i: (i, j)))
        args.append(prev)
    if ep == "resid":
        assert tm % SEQ_BLOCK == 0 and out_dtype == res.dtype
        in_specs.append(pl.BlockSpec((tm, tn), lambda j, i: (i, j)))
        args.append(res)
        in_specs.append(pl.BlockSpec((tm // SEQ_BLOCK, 1, tn), lambda j, i: (i, 0, j)))
        args.append(gate_rows)
    scratch = [pltpu.VMEM((tn, K) if w_t else (K, tn), BF16)] if cast_w else []
    return pl.pallas_call(
        functools.partial(_mm_kernel, rms=rms, ep=ep, cast_w=cast_w, w_t=w_t),
        grid=(n_out // tn, M // tm),
        in_specs=in_specs,
        out_specs=pl.BlockSpec((tm, tn), lambda j, i: (i, j)),
        out_shape=jax.ShapeDtypeStruct((M, n_out), out_dtype),
        scratch_shapes=scratch,
        compiler_params=_cparams(("arbitrary", "arbitrary")),
        name=name,
    )(*args)


def _prenorm_kernel(x_ref, nw_ref, mod_ref, o_ref):
    x = x_ref[...]
    ms = jnp.mean(x * x, axis=-1, keepdims=True)
    y = x * lax.rsqrt(ms + EPS) * nw_ref[...]
    shift = mod_ref[:, 0:D_MODEL]
    scale = mod_ref[:, D_MODEL:2 * D_MODEL]
    o_ref[...] = (y * (1.0 + scale) + shift).astype(o_ref.dtype)


def _prenorm(x, norm_w, mod_rows):
    M = x.shape[0]
    return pl.pallas_call(
        _prenorm_kernel,
        grid=(M // SEQ_BLOCK,),
        in_specs=[pl.BlockSpec((SEQ_BLOCK, D_MODEL), lambda i: (i, 0)),
                  pl.BlockSpec((1, D_MODEL), lambda i: (0, 0)),
                  pl.BlockSpec((None, 1, 3 * D_MODEL), lambda i: (i, 0, 0))],
        out_specs=pl.BlockSpec((SEQ_BLOCK, D_MODEL), lambda i: (i, 0)),
        out_shape=jax.ShapeDtypeStruct((M, D_MODEL), BF16),
        compiler_params=_cparams(("arbitrary",)),
        name="prenorm",
    )(x, norm_w.reshape(1, D_MODEL), mod_rows)


def _rmsnorm_kernel(x_ref, w_ref, o_ref):
    x = x_ref[...]
    ms = jnp.mean(x * x, axis=-1, keepdims=True)
    o_ref[...] = (x * lax.rsqrt(ms + EPS) * w_ref[...]).astype(o_ref.dtype)


def _rmsnorm(x, w, *, x_col0=0, width=None, tm=512, out_dtype=F32):
    M = x.shape[0]
    width = x.shape[1] if width is None else width
    assert x_col0 % width == 0 and M % tm == 0
    cj = x_col0 // width
    return pl.pallas_call(
        _rmsnorm_kernel,
        grid=(M // tm,),
        in_specs=[pl.BlockSpec((tm, width), lambda i: (i, cj)),
                  pl.BlockSpec((1, width), lambda i: (0, 0))],
        out_specs=pl.BlockSpec((tm, width), lambda i: (i, 0)),
        out_shape=jax.ShapeDtypeStruct((M, width), out_dtype),
        compiler_params=_cparams(("arbitrary",)),
        name="rmsnorm",
    )(x, w.reshape(1, width))


def _swap_pairs(x):
    n = x.shape[-1]
    nxt = pltpu.roll(x, n - 1, axis=1)
    prv = pltpu.roll(x, 1, axis=1)
    lane = lax.broadcasted_iota(jnp.int32, x.shape, 1)
    return jnp.where((lane & 1) == 0, nxt, prv)


def _mla_kernel(*refs, latent, n_alias):
    refs = _drop_alias_refs(refs, (11 if latent else 5) + n_alias, n_alias)
    it = iter(refs)
    qn_ref = next(it)
    qr_ref = next(it)
    g_ref = next(it)
    kv_ref = next(it)
    sm_ref = next(it)
    if latent:
        kvc_ref = next(it)
        krc_ref = next(it)
        cq_ref = next(it)
        sq_ref = next(it)
        ck_ref = next(it)
        sk_ref = next(it)
    o_ref = next(it)

    scale = float(QK_NOPE + QK_ROPE) ** -0.5
    qr = qr_ref[...]
    sm = sm_ref[...]
    if latent:
        qr = qr * cq_ref[...] + _swap_pairs(qr) * sq_ref[...]
        sm = sm * ck_ref[...] + _swap_pairs(sm) * sk_ref[...]
    kr = sm[:, SM["k_r"]:SM["k_r"] + QK_ROPE].astype(BF16)
    if latent:
        krc = krc_ref[...].astype(BF16)

    for h in range(MLA_HEADS):
        q_h = jnp.concatenate([qn_ref[:, h * QK_NOPE:(h + 1) * QK_NOPE],
                               qr[:, h * QK_ROPE:(h + 1) * QK_ROPE].astype(BF16)], axis=1)
        c0 = h * (QK_NOPE + V_HEAD)
        v_h = kv_ref[:, c0 + QK_NOPE:c0 + QK_NOPE + V_HEAD]
        s_own = _dot_nt(q_h, jnp.concatenate([kv_ref[:, c0:c0 + QK_NOPE], kr], axis=1)) * scale
        mx = jnp.max(s_own, axis=-1, keepdims=True)
        if latent:
            vc_h = kvc_ref[:, c0 + QK_NOPE:c0 + QK_NOPE + V_HEAD]
            s_ctx = _dot_nt(q_h, jnp.concatenate([kvc_ref[:, c0:c0 + QK_NOPE], krc], axis=1)) * scale
            mx = jnp.maximum(mx, jnp.max(s_ctx, axis=-1, keepdims=True))
            p_ctx = jnp.exp(s_ctx - mx)
        p_own = jnp.exp(s_own - mx)
        den = jnp.sum(p_own, axis=-1, keepdims=True)
        if latent:
            den = den + jnp.sum(p_ctx, axis=-1, keepdims=True)
        inv = 1.0 / den
        o_h = _dot((p_own * inv).astype(BF16), v_h)
        if latent:
            o_h = o_h + _dot((p_ctx * inv).astype(BF16), vc_h)
        g_h = g_ref[:, h * V_HEAD:(h + 1) * V_HEAD]
        o_ref[:, h * V_HEAD:(h + 1) * V_HEAD] = (o_h * _silu(g_h)).astype(o_ref.dtype)


def _mla(qn, qr, P, kv, *, n_tok, n_batch, S, row0, latent, kv_ctx_row0=0, krope_ctx=None, tables=None,
         out_prev=None):
    tq = min(S, 256)
    nq = S // tq
    rb = row0 // tq
    sb = row0 // S
    HW = MLA_HEADS * (QK_NOPE + V_HEAD)
    g_reg, g_col = SEG["g_mla"]
    gj = g_col // MLA_WIDTH
    smj = SMALL_COL0 // SMALL_W
    in_specs = [
        pl.BlockSpec((tq, MLA_HEADS * QK_NOPE), lambda b, i: (rb + b * nq + i, 0)),
        pl.BlockSpec((tq, MLA_HEADS * QK_ROPE), lambda b, i: (rb + b * nq + i, 0)),
        pl.BlockSpec((tq, MLA_WIDTH), lambda b, i: (rb + b * nq + i, gj)),
        pl.BlockSpec((S, HW), lambda b, i: (sb + b, 0)),
        pl.BlockSpec((S, SMALL_W), lambda b, i: (sb + b, smj)),
    ]
    args = [qn, qr, P[g_reg], kv, P["s"]]
    if latent:
        Tc = krope_ctx.shape[1]
        cb = kv_ctx_row0 // Tc
        cos_q, sin_q, cos_k, sin_k = tables
        in_specs += [
            pl.BlockSpec((Tc, HW), lambda b, i: (cb + b, 0)),
            pl.BlockSpec((None, Tc, QK_ROPE), lambda b, i: (b, 0, 0)),
            pl.BlockSpec((tq, MLA_HEADS * QK_ROPE), lambda b, i: (i, 0)),
            pl.BlockSpec((tq, MLA_HEADS * QK_ROPE), lambda b, i: (i, 0)),
            pl.BlockSpec((S, SMALL_W), lambda b, i: (0, 0)),
            pl.BlockSpec((S, SMALL_W), lambda b, i: (0, 0)),
        ]
        args += [kv, krope_ctx, cos_q, sin_q, cos_k, sin_k]
    aliases = {}
    if out_prev is not None:
        aliases[len(args)] = 0
        in_specs.append(pl.BlockSpec(memory_space=pl.ANY))
        args.append(out_prev)
    return pl.pallas_call(
        functools.partial(_mla_kernel, latent=latent, n_alias=len(aliases)),
        grid=(n_batch, nq),
        in_specs=in_specs,
        out_specs=pl.BlockSpec((tq, MLA_WIDTH), lambda b, i: (rb + b * nq + i, 0)),
        out_shape=jax.ShapeDtypeStruct((n_tok, MLA_WIDTH), BF16),
        input_output_aliases=aliases,
        compiler_params=_cparams(("arbitrary", "arbitrary")),
        name="mla_latent" if latent else "mla_context",
    )(*args)


def _rope_tables(S):
    rows = jnp.repeat(jnp.arange(S // GRID_W, dtype=F32), GRID_W)
    cols = jnp.tile(jnp.arange(GRID_W, dtype=F32), S // GRID_W)
    n_freq = QK_ROPE // 4
    inv = ROPE_BASE ** (-jnp.arange(n_freq, dtype=F32) / n_freq)
    ang = jnp.concatenate([rows[:, None] * inv, cols[:, None] * inv], axis=-1)
    cos = jnp.repeat(jnp.cos(ang), 2, axis=-1)
    sin = jnp.repeat(jnp.sin(ang), 2, axis=-1)
    sign = jnp.tile(jnp.array([-1.0, 1.0], F32), QK_ROPE // 2)
    sin = sin * sign
    cos_q = jnp.tile(cos, (1, MLA_HEADS))
    sin_q = jnp.tile(sin, (1, MLA_HEADS))
    lo, hi = SM["k_r"], SMALL_W - SM["k_r"] - QK_ROPE
    cos_k = jnp.concatenate([jnp.ones((S, lo), F32), cos, jnp.ones((S, hi), F32)], axis=-1)
    sin_k = jnp.concatenate([jnp.zeros((S, lo), F32), sin, jnp.zeros((S, hi), F32)], axis=-1)
    return cos_q, sin_q, cos_k, sin_k


def _gla_kernel(*refs, S, C, hps, has_state, emit_state, n_alias):
    refs = _drop_alias_refs(refs, 8 + (2 if has_state else 0) + n_alias, n_alias)
    it = iter(refs)
    q_ref = next(it)
    k_ref = next(it)
    v_ref = next(it)
    sm_ref = next(it)
    gg_ref = next(it)
    w2_ref = next(it)
    b2_ref = next(it)
    nw_ref = next(it)
    s0_refs = (next(it), next(it)) if has_state else None
    o_ref = next(it)
    so_refs = (next(it), next(it)) if emit_state else None
    vt_ref = next(it)
    la_ref = next(it)
    acc_ref = next(it)
    st_ref = next(it)

    n = S // C
    DK, DV = GLA_DK, GLA_DV
    scale = float(GLA_DK) ** -0.5
    sm = sm_ref[...]
    for d, name in enumerate(("ga_f", "ga_b")):
        ga = sm[:, SM[name]:SM[name] + GLA_GATE_RANK].astype(BF16)
        xg = _dot(ga, w2_ref[d].astype(BF16)) + b2_ref[d]
        la_ref[d] = -_softplus(-xg) * (1.0 / GLA_GATE_NORM)
    for hh in range(hps):
        for c in range(n):
            vt_ref[hh, c] = v_ref[c * C:(c + 1) * C, hh * DV:(hh + 1) * DV].T.astype(BF16)
        for d in range(2):
            if has_state:
                st_ref[d, hh] = s0_refs[d][hh].T
            else:
                st_ref[d, hh] = jnp.zeros((DV, DK), F32)

    row = lax.broadcasted_iota(jnp.int32, (C, C), 0)
    col = lax.broadcasted_iota(jnp.int32, (C, C), 1)
    lower = row >= col
    upper = row <= col
    tri = (jnp.where(lower, 1.0, 0.0).astype(BF16), jnp.where(upper, 1.0, 0.0).astype(BF16))
    mid = C // 2

    def chunk(c, d, hh):
        r0 = pl.multiple_of(c * C, C)
        kcols = slice(hh * DK, (hh + 1) * DK)
        vcols = slice(hh * DV, (hh + 1) * DV)
        q = q_ref[pl.ds(r0, C), kcols] * scale
        k = k_ref[pl.ds(r0, C), kcols]
        v = v_ref[pl.ds(r0, C), vcols].astype(BF16)
        g = la_ref[d, pl.ds(r0, C), kcols]
        b = _tri_left(tri[d], g)
        tot = b[C - 1:C, :] if d == 0 else b[0:1, :]
        bm = b[mid:mid + 1, :]
        st = st_ref[d, hh]
        o = _dot_nt((q * jnp.exp(b)).astype(BF16), st.astype(BF16))
        qa = (q * jnp.exp(b - bm)).astype(BF16)
        ka = (k * jnp.exp(bm - b)).astype(BF16)
        att = _dot_nt(qa, ka)
        att = jnp.where(lower if d == 0 else upper, att, 0.0)
        o = o + _dot(att.astype(BF16), v)
        kd = (k * jnp.exp(tot - b)).astype(BF16)
        st_ref[d, hh] = st * jnp.exp(tot) + _dot(vt_ref[hh, c], kd)
        acc_ref[d, pl.ds(r0, C), vcols] = o

    def both(i, carry):
        for hh in range(hps):
            chunk(i, 0, hh)
            chunk(n - 1 - i, 1, hh)
        return carry

    lax.fori_loop(0, n, both, 0)

    for hh in range(hps):
        vcols = slice(hh * DV, (hh + 1) * DV)
        o = acc_ref[0, :, vcols] + acc_ref[1, :, vcols]
        ms = jnp.mean(o * o, axis=-1, keepdims=True)
        y = o * lax.rsqrt(ms + EPS) * nw_ref[...]
        o_ref[:, vcols] = (y * _silu(gg_ref[:, vcols])).astype(o_ref.dtype)
        if emit_state:
            for d in range(2):
                so_refs[d][hh] = st_ref[d, hh].T


def _gla(P, w2, b2, nw, *, layer, n_tok, n_batch, S, row0, states_in=None, states_out=None, out_prev=None):
    C = GLA_CHUNK
    sb = row0 // S
    hps = 2 if S <= SEQ_BLOCK else 1
    kw, vw = hps * GLA_DK, hps * GLA_DV
    col = {n: SEG[n][1] for n in ("q_l", "k_l", "v_l", "g_gla")}
    qj, kj, vj, gj = col["q_l"] // kw, col["k_l"] // kw, col["v_l"] // vw, col["g_gla"] // vw
    smj = SMALL_COL0 // SMALL_W
    in_specs = [
        pl.BlockSpec((S, kw), lambda b, h: (sb + b, qj + h)),
        pl.BlockSpec((S, kw), lambda b, h: (sb + b, kj + h)),
        pl.BlockSpec((S, vw), lambda b, h: (sb + b, vj + h)),
        pl.BlockSpec((S, SMALL_W), lambda b, h: (sb + b, smj)),
        pl.BlockSpec((S, vw), lambda b, h: (sb + b, gj + h)),
        pl.BlockSpec((None, 2, GLA_GATE_RANK, kw), lambda b, h: (layer, 0, 0, h)),
        pl.BlockSpec((None, 2, 1, kw), lambda b, h: (layer, 0, 0, h)),
        pl.BlockSpec((None, 1, GLA_DV), lambda b, h: (layer, 0, 0)),
    ]
    args = [P[SEG["q_l"][0]], P[SEG["k_l"][0]], P[SEG["v_l"][0]], P["s"], P[SEG["g_gla"][0]],
            w2, b2.reshape(DEPTH, 2, 1, GLA_KEY), nw.reshape(DEPTH, 1, GLA_DV)]
    has_state = states_in is not None
    st_block = (None, None, hps, GLA_DK, GLA_DV)
    if has_state:
        for s in states_in:
            in_specs.append(pl.BlockSpec(st_block, lambda b, h: (b, layer, h, 0, 0)))
            args.append(s)
    out_specs = [pl.BlockSpec((S, vw), lambda b, h: (sb + b, h))]
    out_shape = [jax.ShapeDtypeStruct((n_tok, GLA_WIDTH), BF16)]
    aliased = [out_prev]
    emit_state = states_out is not None
    if emit_state:
        for s in states_out:
            out_specs.append(pl.BlockSpec(st_block, lambda b, h: (b, layer, h, 0, 0)))
            out_shape.append(jax.ShapeDtypeStruct((n_batch, DEPTH, GLA_HEADS, GLA_DK, GLA_DV), F32))
            aliased.append(s)
    aliases = {}
    for oi, buf in enumerate(aliased):
        if buf is not None:
            aliases[len(args)] = oi
            in_specs.append(pl.BlockSpec(memory_space=pl.ANY))
            args.append(buf)
    n = S // C
    return pl.pallas_call(
        functools.partial(_gla_kernel, S=S, C=C, hps=hps, has_state=has_state, emit_state=emit_state,
                          n_alias=len(aliases)),
        grid=(n_batch, GLA_HEADS // hps),
        in_specs=in_specs,
        out_specs=out_specs,
        out_shape=out_shape,
        input_output_aliases=aliases,
        scratch_shapes=[pltpu.VMEM((hps, n, GLA_DV, C), BF16), pltpu.VMEM((2, S, kw), F32),
                        pltpu.VMEM((2, S, vw), F32), pltpu.VMEM((2, hps, GLA_DV, GLA_DK), F32)],
        compiler_params=_cparams(("arbitrary", "arbitrary")),
        name="gla_latent" if has_state else "gla_context",
    )(*args)


CONV_PAD = SUBLANES
CONV_ROWS = 256


def _conv_silu(dst_ref, src_ref, w_ref, b_ref, xp_ref, S):
    W = src_ref.shape[1]
    left = (SSM_CONV - 1) // 2
    xp_ref[0:CONV_PAD, 0:W] = jnp.zeros((CONV_PAD, W), F32)
    xp_ref[CONV_PAD + S:CONV_PAD + S + CONV_PAD, 0:W] = jnp.zeros((CONV_PAD, W), F32)
    xp_ref[CONV_PAD:CONV_PAD + S, 0:W] = src_ref[...]
    for r in range(0, S, CONV_ROWS):
        acc = b_ref[...] + xp_ref[CONV_PAD - left + r:CONV_PAD - left + r + CONV_ROWS, 0:W] * w_ref[0:1, :]
        for j in range(1, SSM_CONV):
            r0 = CONV_PAD - left + j + r
            acc = acc + xp_ref[r0:r0 + CONV_ROWS, 0:W] * w_ref[j:j + 1, :]
        dst_ref[r:r + CONV_ROWS, :] = _silu(acc)
def _ssd_kernel(*refs, S, C, has_state, emit_state, n_alias):
    refs = _drop_alias_refs(refs, 15 + (2 if has_state else 0) + n_alias, n_alias)
    it = iter(refs)
    xin_ref = next(it)
    bin_ref = next(it)
    cin_ref = next(it)
    cw_refs = (next(it), next(it), next(it))
    cb_refs = (next(it), next(it), next(it))
    z_ref = next(it)
    dtc_ref = next(it)
    dtr_ref = next(it)
    pc_ref = next(it)
    pr_ref = next(it)
    d_ref = next(it)
    s0_refs = (next(it), next(it)) if has_state else None
    y_ref = next(it)
    so_refs = (next(it), next(it)) if emit_state else None
    bt_ref = next(it)
    acc_ref = next(it)
    st_ref = next(it)
    xp_ref = next(it)
    xh_ref = next(it)
    x_ref = next(it)
    b_ref = next(it)
    c_ref = next(it)

    for dst, src, w, b in zip((x_ref, b_ref, c_ref), (xin_ref, bin_ref, cin_ref), cw_refs, cb_refs):
        _conv_silu(dst, src, w, b, xp_ref, S)

    n = S // C
    P = SSM_HEADDIM
    for r in range(0, S, CONV_ROWS):
        xr = x_ref[r:r + CONV_ROWS, :]
        for j in range(SSM_HPG):
            xh_ref[j, r:r + CONV_ROWS, :] = xr[:, j * P:(j + 1) * P].astype(BF16)
    for c in range(n):
        bt_ref[c] = b_ref[c * C:(c + 1) * C, :].T
    for d in range(2):
        if has_state:
            s0_t = jnp.concatenate([s0_refs[d][j] for j in range(SSM_HPG)], axis=0).T
            for j in range(SSM_HPG):
                st_ref[d, j] = s0_t[:, j * P:(j + 1) * P]
        else:
            st_ref[d] = jnp.zeros((SSM_HPG, SSM_STATE, P), F32)

    row = lax.broadcasted_iota(jnp.int32, (C, C), 0)
    col = lax.broadcasted_iota(jnp.int32, (C, C), 1)
    lower = row >= col
    upper = row <= col
    tri_l = jnp.where(lower, 1.0, 0.0).astype(BF16)
    tri_u = jnp.where(upper, 1.0, 0.0).astype(BF16)

    def chunk(c, d):
        r0 = pl.multiple_of(c * C, C)
        bm = b_ref[pl.ds(r0, C), :].astype(BF16)
        cm = c_ref[pl.ds(r0, C), :]
        bt = bt_ref[c]
        a_c = -jnp.exp(pc_ref[d, 1])
        a_r = -jnp.exp(pr_ref[d, 1])
        dt_c = _softplus(dtc_ref[d, pl.ds(r0, C), :] + pc_ref[d, 0])
        dt_r = _softplus(dtr_ref[d, c] + pr_ref[d, 0])
        if d == 0:
            cum_c = _tri_left(tri_l, dt_c * a_c)
            cum_r = _tri_right(dt_r * a_r, tri_u)
        else:
            cum_c = _tri_left(tri_u, dt_c * a_c)
            cum_r = _tri_right(dt_r * a_r, tri_l)
        mask = lower if d == 0 else upper
        last_r = cum_r[:, C - 1:C] if d == 0 else cum_r[:, 0:1]
        u_r = dt_r * jnp.exp(last_r - cum_r)
        e_last = jnp.exp(last_r)
        cb = _dot_nt(cm.astype(BF16), bm)
        cq2 = cum_c * LOG2_E
        cr2 = cum_r * LOG2_E
        for j in range(SSM_HPG):
            cum_q = jnp.broadcast_to(cq2[:, j:j + 1], (C, C))
            dec = jnp.where(mask, jnp.exp2(cum_q - cr2[j:j + 1, :]), 0.0)
            lhs = jnp.concatenate([(cb * dec * dt_r[j:j + 1, :]).astype(BF16),
                                   (cm * jnp.exp2(cum_q)).astype(BF16)], axis=1)
            x_j = xh_ref[j, pl.ds(r0, C), :]
            st_j = st_ref[d, j]
            acc_ref[d, j, pl.ds(r0, C), :] = _dot(lhs, jnp.concatenate([x_j, st_j.astype(BF16)], axis=0))
            st_ref[d, j] = st_j * e_last[j:j + 1, :] + _dot((bt * u_r[j:j + 1, :]).astype(BF16), x_j)

    def both(i, carry):
        chunk(i, 0)
        chunk(n - 1 - i, 1)
        return carry

    lax.fori_loop(0, n, both, 0)

    for r in range(0, S, CONV_ROWS):
        rows = slice(r, r + CONV_ROWS)
        y = jnp.concatenate([acc_ref[0, j, rows, :] + acc_ref[1, j, rows, :] for j in range(SSM_HPG)], axis=-1)
        y = y + x_ref[rows, :] * d_ref[...]
        y_ref[rows, :] = y * _silu(z_ref[rows, :])
    if emit_state:
        for d in range(2):
            s_t = jnp.concatenate([st_ref[d, j] for j in range(SSM_HPG)], axis=1).T
            for j in range(SSM_HPG):
                so_refs[d][j] = s_t[j * P:(j + 1) * P, :]


def _ssd(P, cw, cb, dt_col, dt_row, p_col, p_row, d_lane, *, layer, n_tok, n_batch, S, row0,
         states_in=None, states_out=None, out_prev=None):
    C = SSM_CHUNK
    sb = row0 // S
    z_reg, z_col = SEG["z"]
    x_reg, x_col = SEG["xbc"]
    zj = z_col // SSM_GW
    xj = x_col // SSM_GW
    bj = SSM_WIDTH // SSM_STATE
    cj = (SSM_WIDTH + SSM_BC) // SSM_STATE
    xbj = x_col // SSM_STATE

    def taps(width, j0):
        return pl.BlockSpec((None, SSM_CONV, width), lambda b, g: (layer, 0, j0 + g))

    def bias(width, j0):
        return pl.BlockSpec((None, 1, width), lambda b, g: (layer, 0, j0 + g))

    in_specs = [
        pl.BlockSpec((S, SSM_GW), lambda b, g: (sb + b, xj + g)),
        pl.BlockSpec((S, SSM_STATE), lambda b, g: (sb + b, xbj + bj + g)),
        pl.BlockSpec((S, SSM_STATE), lambda b, g: (sb + b, xbj + cj + g)),
        taps(SSM_GW, 0), taps(SSM_STATE, bj), taps(SSM_STATE, cj),
        bias(SSM_GW, 0), bias(SSM_STATE, bj), bias(SSM_STATE, cj),
        pl.BlockSpec((S, SSM_GW), lambda b, g: (sb + b, zj + g)),
        pl.BlockSpec((2, None, S, SSM_HPG), lambda b, g: (0, g, sb + b, 0)),
        pl.BlockSpec((2, None, S // C, SSM_HPG, C), lambda b, g: (0, g, sb + b, 0, 0)),
        pl.BlockSpec((None, 2, 2, None, 1, SSM_HPG), lambda b, g: (layer, 0, 0, g, 0, 0)),
        pl.BlockSpec((None, 2, 2, None, SSM_HPG, 1), lambda b, g: (layer, 0, 0, g, 0, 0)),
        pl.BlockSpec((None, 1, SSM_GW), lambda b, g: (layer, 0, g)),
    ]
    cb3 = cb.reshape(DEPTH, 1, SSM_CONV_CH)
    args = [P[x_reg], P[x_reg], P[x_reg], cw, cw, cw, cb3, cb3, cb3, P[z_reg], dt_col, dt_row, p_col, p_row,
            d_lane]
    has_state = states_in is not None
    st_block = (None, None, SSM_HPG, SSM_HEADDIM, SSM_STATE)
    if has_state:
        for s in states_in:
            in_specs.append(pl.BlockSpec(st_block, lambda b, g: (b, layer, g, 0, 0)))
            args.append(s)
    out_specs = [pl.BlockSpec((S, SSM_GW), lambda b, g: (sb + b, g))]
    out_shape = [jax.ShapeDtypeStruct((n_tok, SSM_WIDTH), F32)]
    aliased = [out_prev]
    emit_state = states_out is not None
    if emit_state:
        for s in states_out:
            out_specs.append(pl.BlockSpec(st_block, lambda b, g: (b, layer, g, 0, 0)))
            out_shape.append(jax.ShapeDtypeStruct((n_batch, DEPTH, SSM_HEADS, SSM_HEADDIM, SSM_STATE), F32))
            aliased.append(s)
    aliases = {}
    for oi, buf in enumerate(aliased):
        if buf is not None:
            aliases[len(args)] = oi
            in_specs.append(pl.BlockSpec(memory_space=pl.ANY))
            args.append(buf)
    n = S // C
    return pl.pallas_call(
        functools.partial(_ssd_kernel, S=S, C=C, has_state=has_state, emit_state=emit_state,
                          n_alias=len(aliases)),
        grid=(n_batch, SSM_GROUPS),
        in_specs=in_specs,
        out_specs=out_specs,
        out_shape=out_shape,
        input_output_aliases=aliases,
        scratch_shapes=[pltpu.VMEM((n, SSM_STATE, C), F32), pltpu.VMEM((2, SSM_HPG, S, SSM_HEADDIM), F32),
                        pltpu.VMEM((2, SSM_HPG, SSM_STATE, SSM_HEADDIM), F32),
                        pltpu.VMEM((S + 2 * CONV_PAD, SSM_GW), F32),
                        pltpu.VMEM((SSM_HPG, S, SSM_HEADDIM), BF16), pltpu.VMEM((S, SSM_GW), F32),
                        pltpu.VMEM((S, SSM_STATE), F32), pltpu.VMEM((S, SSM_STATE), F32)],
        compiler_params=_cparams(("arbitrary", "arbitrary")),
        name="ssd_latent" if has_state else "ssd_context",
    )(*args)


def _gather_small_w_in(w_in_t):
    names = REGIONS["s"]
    rows = [w_in_t[:, _IN_OFF[n]:_IN_OFF[n] + _IN_SIZE[n], :] for n in names]
    used = sum(_IN_SIZE[n] for n in names)
    rows.append(jnp.zeros((DEPTH, REGION_W["s"] - used, D_MODEL), w_in_t.dtype))
    return jnp.concatenate(rows, axis=1)


def kernel(x_prompt, x_sample, cache_mla_ckv, cache_mla_krope, state_gla_fwd, state_gla_bwd,
           state_ssm_fwd, state_ssm_bwd, c, c_ctx, w_mod, b_mod, norm_w, w_in, q_a_norm, w_q_b,
           kv_a_norm, w_kv_b, gla_w_gate2, gla_b_gate, gla_norm, ssm_conv_w, ssm_conv_b,
           ssm_dt_bias, ssm_a_log, ssm_d, ssm_norm, w_br_mla, w_br_gla, w_br_ssm, w_out, final_norm):
    nb_p, s_p, _ = x_prompt.shape
    nb_l, s_l, _ = x_sample.shape
    n_p = nb_p * s_p
    n_l = nb_l * s_l
    n_tok = n_p + n_l
    assert s_p == SEQ_BLOCK and s_l % SEQ_BLOCK == 0 and n_p % s_l == 0
    t_c = cache_mla_ckv.shape[2]
    assert t_c == SEQ_BLOCK

    x = jnp.concatenate([x_prompt.reshape(n_p, D_MODEL), x_sample.reshape(n_l, D_MODEL)], axis=0)

    w_in_t = jnp.swapaxes(w_in, 1, 2)
    w_in_s = _gather_small_w_in(w_in_t)
    wq = w_q_b.reshape(DEPTH, Q_LORA, MLA_HEADS, QK_NOPE + QK_ROPE)
    wq_nope = wq[..., :QK_NOPE].reshape(DEPTH, Q_LORA, MLA_HEADS * QK_NOPE)
    wq_rope = wq[..., QK_NOPE:].reshape(DEPTH, Q_LORA, MLA_HEADS * QK_ROPE)
    tables = _rope_tables(s_l)
    p_ssm = jnp.stack([ssm_dt_bias, ssm_a_log], axis=2).reshape(DEPTH, 2, 2, SSM_GROUPS, SSM_HPG)
    p_col = p_ssm[:, :, :, :, None, :]
    p_row = p_ssm[:, :, :, :, :, None]
    d_lane = jnp.repeat(ssm_d, SSM_HEADDIM, axis=-1).reshape(DEPTH, 1, SSM_WIDTH)
    blk_mod = np.concatenate([np.zeros(n_p // SEQ_BLOCK, np.int32),
                              1 + np.repeat(np.arange(nb_l, dtype=np.int32), s_l // SEQ_BLOCK)])
    cond = jnp.concatenate([c_ctx[None, :], c], axis=0)
    cond = _silu(cond)
    cond = jnp.concatenate([cond, jnp.zeros((8 - cond.shape[0] % 8, D_MODEL), F32)], axis=0)

    gla_f = gla_b = ssm_f = ssm_b = None
    ckv_l, kr_l = [], []
    for l in range(DEPTH):
        mod = _matmul(cond, w_mod, name="mm_mod", layer=l, n_out=3 * D_MODEL, tm=cond.shape[0], tn=1536,
                      out_dtype=F32)
        mod = mod[:1 + nb_l] + b_mod[l][None, :]
        mod_rows = mod[blk_mod][:, None, :]
        gate_rows = mod_rows[:, :, 2 * D_MODEL:]

        h = _prenorm(x, norm_w[l], mod_rows)
        P = {}
        for r in ("a", "b", "c"):
            first = REGIONS[r][0]
            P[r] = _matmul(h, w_in_t, name="mm_in_" + r, layer=l, w_t=True, w_col0=_IN_OFF[first],
                           n_out=REGION_W[r], tm=1024, tn=IN_TN, out_dtype=F32)
        P["s"] = _matmul(h, w_in_s, name="mm_in_s", layer=l, w_t=True, n_out=REGION_W["s"], tm=512, tn=IN_TN,
                         out_dtype=F32)
        small = P["s"][:, SMALL_COL0:SMALL_COL0 + SMALL_W]

        qa_col, kva_col = SEG["q_a"][1], SEG["kv_a"][1]
        qn = _matmul(P["s"], wq_nope, name="mm_q_nope", x_col0=qa_col, k=Q_LORA, layer=l,
                     n_out=MLA_HEADS * QK_NOPE, tm=1024, tn=1024, out_dtype=BF16, gain=q_a_norm[l])
        qr = _matmul(P["s"], wq_rope, name="mm_q_rope", x_col0=qa_col, k=Q_LORA, layer=l,
                     n_out=MLA_HEADS * QK_ROPE, tm=1024, tn=1024, out_dtype=F32, gain=q_a_norm[l])
        ckv = _rmsnorm(P["s"], kv_a_norm[l], x_col0=kva_col, width=KV_LORA)
        ckv_all = jnp.concatenate([ckv, cache_mla_ckv[:, l].reshape(nb_l * t_c, KV_LORA)], axis=0)
        kv = _matmul(ckv_all, w_kv_b, name="mm_kv", layer=l, n_out=MLA_HEADS * (QK_NOPE + V_HEAD),
                     tm=ckv_all.shape[0] // 4, tn=1024, out_dtype=BF16)
        o_mla = _mla(qn, qr, P, kv, n_tok=n_tok, n_batch=nb_p, S=s_p, row0=0, latent=False)
        o_mla = _mla(qn, qr, P, kv, n_tok=n_tok, n_batch=nb_l, S=s_l, row0=n_p, latent=True,
                     kv_ctx_row0=n_tok, krope_ctx=cache_mla_krope[:, l], tables=tables, out_prev=o_mla)
        ckv_l.append(ckv[:n_p].reshape(nb_p, s_p, KV_LORA))
        kr_l.append(small[:n_p, SM["k_r"]:SM["k_r"] + QK_ROPE].reshape(nb_p, s_p, QK_ROPE))

        o_gla, gla_f, gla_b = _gla(P, gla_w_gate2, gla_b_gate, gla_norm, layer=l, n_tok=n_tok, n_batch=nb_p,
                                   S=s_p, row0=0, states_out=(gla_f, gla_b))
        (o_gla,) = _gla(P, gla_w_gate2, gla_b_gate, gla_norm, layer=l, n_tok=n_tok, n_batch=nb_l, S=s_l,
                        row0=n_p, states_in=(state_gla_fwd, state_gla_bwd), out_prev=o_gla)

        dt = small[:, SM["dt_f"]:SM["dt_f"] + 2 * SSM_HEADS].reshape(n_tok, 2, SSM_GROUPS, SSM_HPG)
        dt_col = jnp.transpose(dt, (1, 2, 0, 3))
        dt_row = jnp.transpose(dt.reshape(n_tok // SSM_CHUNK, SSM_CHUNK, 2, SSM_GROUPS, SSM_HPG),
                               (2, 3, 0, 4, 1))
        yz, ssm_f, ssm_b = _ssd(P, ssm_conv_w, ssm_conv_b, dt_col, dt_row, p_col, p_row, d_lane, layer=l,
                                n_tok=n_tok, n_batch=nb_p, S=s_p, row0=0, states_out=(ssm_f, ssm_b))
        (yz,) = _ssd(P, ssm_conv_w, ssm_conv_b, dt_col, dt_row, p_col, p_row, d_lane, layer=l, n_tok=n_tok,
                     n_batch=nb_l, S=s_l, row0=n_p, states_in=(state_ssm_fwd, state_ssm_bwd), out_prev=yz)

        mg = _matmul(o_mla, w_br_mla, name="mm_br_mla", layer=l, n_out=D_MODEL, tm=512, tn=1024, out_dtype=F32,
                     ep="sig", m=P["c"], m_col0=SEG["m_mla"][1])
        mg = _matmul(o_gla, w_br_gla, name="mm_br_gla", layer=l, n_out=D_MODEL, tm=512, tn=1024, out_dtype=F32,
                     ep="sigadd", m=P["c"], m_col0=SEG["m_gla"][1], prev=mg)
        o_ssm = _rmsnorm(yz, ssm_norm[l], tm=256, out_dtype=BF16)
        mg = _matmul(o_ssm, w_br_ssm, name="mm_br_ssm", layer=l, n_out=D_MODEL, tm=512, tn=512, out_dtype=BF16,
                     ep="sigadd", m=P["c"], m_col0=SEG["m_ssm"][1], prev=mg)
        x = _matmul(mg, w_out, name="mm_out", layer=l, n_out=D_MODEL, tm=512, tn=1024, out_dtype=F32,
                    ep="resid", res=x, gate_rows=gate_rows)

    y = _rmsnorm(x, final_norm)
    y_prompt = y[:n_p].reshape(nb_p, s_p, D_MODEL)
    y_sample = y[n_p:].reshape(nb_l, s_l, D_MODEL)
    return (y_prompt, y_sample, jnp.stack(ckv_l, axis=1), jnp.stack(kr_l, axis=1),
            gla_f, gla_b, ssm_f, ssm_b)
```

```python
import functools

import jax
import jax.numpy as jnp
import numpy as np
from jax import lax
from jax.experimental import pallas as pl
from jax.experimental.pallas import tpu as pltpu

F32 = jnp.float32
BF16 = jnp.bfloat16

D_MODEL = 2048
DEPTH = 4
GRID_W = 64
EPS = 1e-6
LOG2_E = 1.4426950408889634
MLA_HEADS = 16
QK_NOPE = 128
QK_ROPE = 64
V_HEAD = 128
Q_LORA = 512
KV_LORA = 256
MLA_WIDTH = MLA_HEADS * V_HEAD
ROPE_BASE = 10000.0
GLA_HEADS = 4
GLA_DK = 256
GLA_DV = 512
GLA_KEY = GLA_HEADS * GLA_DK
GLA_WIDTH = GLA_HEADS * GLA_DV
GLA_GATE_RANK = 16
GLA_GATE_NORM = 16.0
SSM_HEADS = 64
SSM_HEADDIM = 64
SSM_WIDTH = SSM_HEADS * SSM_HEADDIM
SSM_GROUPS = 8
SSM_HPG = SSM_HEADS // SSM_GROUPS
SSM_GW = SSM_HPG * SSM_HEADDIM
SSM_STATE = 128
SSM_CONV = 5
SSM_BC = SSM_GROUPS * SSM_STATE
SSM_CONV_CH = SSM_WIDTH + 2 * SSM_BC

IN_SIZES = (Q_LORA, KV_LORA, QK_ROPE, MLA_WIDTH,
            GLA_KEY, GLA_KEY, GLA_WIDTH, GLA_GATE_RANK, GLA_GATE_RANK, GLA_WIDTH,
            SSM_WIDTH, SSM_CONV_CH, SSM_HEADS, SSM_HEADS,
            D_MODEL, D_MODEL, D_MODEL)
IN_NAMES = ("q_a", "kv_a", "k_r", "g_mla", "q_l", "k_l", "v_l", "ga_f", "ga_b", "g_gla",
            "z", "xbc", "dt_f", "dt_b", "m_mla", "m_gla", "m_ssm")
_IN_OFF = dict(zip(IN_NAMES, np.cumsum((0,) + IN_SIZES[:-1]).tolist()))
_IN_SIZE = dict(zip(IN_NAMES, IN_SIZES))

REGIONS = {
    "a": ("g_mla", "q_l", "k_l", "v_l"),
    "b": ("g_gla", "z", "xbc"),
    "c": ("m_mla", "m_gla", "m_ssm"),
    "s": ("q_a", "kv_a", "k_r", "ga_f", "ga_b", "dt_f", "dt_b"),
}
IN_TN = 1024
SEG = {}
REGION_W = {}
for _r, _names in REGIONS.items():
    _o = 0
    for _n in _names:
        SEG[_n] = (_r, _o)
        _o += _IN_SIZE[_n]
    REGION_W[_r] = -(-_o // IN_TN) * IN_TN
    if _r != "s":
        assert _o % IN_TN == 0
        assert all(_IN_OFF[_names[i]] + _IN_SIZE[_names[i]] == _IN_OFF[_names[i + 1]]
                   for i in range(len(_names) - 1))
SMALL_W = 256
SMALL_COL0 = SEG["k_r"][1]
assert SMALL_COL0 % SMALL_W == 0 and SEG["dt_b"][1] + _IN_SIZE["dt_b"] <= SMALL_COL0 + SMALL_W
SM = {n: SEG[n][1] - SMALL_COL0 for n in ("k_r", "ga_f", "ga_b", "dt_f", "dt_b")}

SUBLANES = 8
SEQ_BLOCK = 256
GLA_CHUNK = 128
SSM_CHUNK = 128
V7X_VMEM_LIMIT = 56 * 1024 * 1024


def _cparams(sem):
    return pltpu.CompilerParams(dimension_semantics=sem, vmem_limit_bytes=V7X_VMEM_LIMIT)


def _silu(x):
    return x * (1.0 / (1.0 + jnp.exp(-x)))


def _sigmoid(x):
    return 1.0 / (1.0 + jnp.exp(-x))


def _softplus(x):
    return jnp.maximum(x, 0.0) + jnp.log(1.0 + jnp.exp(-jnp.abs(x)))


def _dot(a, b):
    return jnp.dot(a, b, preferred_element_type=F32)


def _dot_nt(a, b):
    return lax.dot_general(a, b, (((1,), (1,)), ((), ())), preferred_element_type=F32)


def _split3(x):
    hi = x.astype(BF16)
    r1 = x - hi.astype(F32)
    mid = r1.astype(BF16)
    lo = (r1 - mid.astype(F32)).astype(BF16)
    return hi, mid, lo


def _tri_left(tri, x):
    hi, mid, lo = _split3(x)
    return _dot(tri, hi) + _dot(tri, mid) + _dot(tri, lo)


def _tri_right(x, tri):
    hi, mid, lo = _split3(x)
    return _dot(hi, tri) + _dot(mid, tri) + _dot(lo, tri)


def _drop_alias_refs(refs, n_in, n_alias):
    refs = list(refs)
    return refs[:n_in - n_alias] + refs[n_in:]


def _mm_kernel(*refs, rms, ep, cast_w, w_t):
    it = iter(refs)
    x_ref = next(it)
    w_ref = next(it)
    gain_ref = next(it) if rms else None
    m_ref = next(it) if ep in ("sig", "sigadd") else None
    prev_ref = next(it) if ep == "sigadd" else None
    res_ref = next(it) if ep == "resid" else None
    gate_ref = next(it) if ep == "resid" else None
    o_ref = next(it)
    wbf_ref = next(it) if cast_w else None

    if cast_w:
        @pl.when(pl.program_id(1) == 0)
        def _():
            wbf_ref[...] = (w_ref[0] if w_t else w_ref[...]).astype(BF16)
        w = wbf_ref[...]
    else:
        w = w_ref[...]
    x = x_ref[...]
    if rms:
        xf = x.astype(F32)
        ms = jnp.mean(xf * xf, axis=-1, keepdims=True)
        x = xf * lax.rsqrt(ms + EPS) * gain_ref[...]
    acc = _dot_nt(x.astype(BF16), w) if w_t else _dot(x.astype(BF16), w)
    if ep == "sig":
        acc = _sigmoid(m_ref[...]) * acc
    elif ep == "sigadd":
        acc = prev_ref[...].astype(F32) + _sigmoid(m_ref[...]) * acc
    elif ep == "resid":
        for r in range(gate_ref.shape[0]):
            rows = slice(r * SEQ_BLOCK, (r + 1) * SEQ_BLOCK)
            o_ref[rows, :] = res_ref[rows, :] + gate_ref[r] * acc[rows, :]
        return
    o_ref[...] = acc.astype(o_ref.dtype)


def _matmul(x, w, *, name, layer, n_out, tm, tn, out_dtype, w_col0=0, w_t=False, gain=None, x_col0=0, k=None,
            ep=None, m=None, m_col0=0, prev=None, res=None, gate_rows=None):
    M = x.shape[0]
    K = x.shape[1] if k is None else k
    assert M % tm == 0 and n_out % tn == 0 and m_col0 % tn == 0 and x_col0 % K == 0
    xj = x_col0 // K
    cast_w = w.dtype != BF16
    rms = gain is not None
    mj = m_col0 // tn
    if w_t:
        assert w_col0 % SUBLANES == 0
        w_spec = pl.BlockSpec((pl.Element(1), pl.Element(tn), pl.Element(K)),
                              lambda j, i: (layer, pl.multiple_of(w_col0 + j * tn, SUBLANES), 0))
    else:
        assert w_col0 % tn == 0
        wj = w_col0 // tn
        w_spec = pl.BlockSpec((None, K, tn), lambda j, i: (layer, 0, j + wj))

    in_specs = [pl.BlockSpec((tm, K), lambda j, i: (i, xj)), w_spec]
    args = [x, w]
    if rms:
        in_specs.append(pl.BlockSpec((1, K), lambda j, i: (0, 0)))
        args.append(gain.reshape(1, K))
    if ep in ("sig", "sigadd"):
        in_specs.append(pl.BlockSpec((tm, tn), lambda j, i: (i, j + mj)))
        args.append(m)
    if ep == "sigadd":
        in_specs.append(pl.BlockSpec((tm, tn), lambda j, i: (i, j)))
        args.append(prev)
    if ep == "resid":
        assert tm % SEQ_BLOCK == 0 and out_dtype == res.dtype
        in_specs.append(pl.BlockSpec((tm, tn), lambda j, i: (i, j)))
        args.append(res)
        in_specs.append(pl.BlockSpec((tm // SEQ_BLOCK, 1, tn), lambda j, i: (i, 0, j)))
        args.append(gate_rows)
    scratch = [pltpu.VMEM((tn, K) if w_t else (K, tn), BF16)] if cast_w else []
    return pl.pallas_call(
        functools.partial(_mm_kernel, rms=rms, ep=ep, cast_w=cast_w, w_t=w_t),
        grid=(n_out // tn, M // tm),
        in_specs=in_specs,
        out_specs=pl.BlockSpec((tm, tn), lambda j, i: (i, j)),
        out_shape=jax.ShapeDtypeStruct((M, n_out), out_dtype),
        scratch_shapes=scratch,
        compiler_params=_cparams(("arbitrary", "arbitrary")),
        name=name,
    )(*args)


def _prenorm_kernel(x_ref, nw_ref, mod_ref, o_ref):
    x = x_ref[...]
    ms = jnp.mean(x * x, axis=-1, keepdims=True)
    y = x * lax.rsqrt(ms + EPS) * nw_ref[...]
    shift = mod_ref[:, 0:D_MODEL]
    scale = mod_ref[:, D_MODEL:2 * D_MODEL]
    o_ref[...] = (y * (1.0 + scale) + shift).astype(o_ref.dtype)


def _prenorm(x, norm_w, mod_rows):
    M = x.shape[0]
    return pl.pallas_call(
        _prenorm_kernel,
        grid=(M // SEQ_BLOCK,),
        in_specs=[pl.BlockSpec((SEQ_BLOCK, D_MODEL), lambda i: (i, 0)),
                  pl.BlockSpec((1, D_MODEL), lambda i: (0, 0)),
                  pl.BlockSpec((None, 1, 3 * D_MODEL), lambda i: (i, 0, 0))],
        out_specs=pl.BlockSpec((SEQ_BLOCK, D_MODEL), lambda i: (i, 0)),
        out_shape=jax.ShapeDtypeStruct((M, D_MODEL), BF16),
        compiler_params=_cparams(("arbitrary",)),
        name="prenorm",
    )(x, norm_w.reshape(1, D_MODEL), mod_rows)


def _rmsnorm_kernel(x_ref, w_ref, o_ref):
    x = x_ref[...]
    ms = jnp.mean(x * x, axis=-1, keepdims=True)
    o_ref[...] = (x * lax.rsqrt(ms + EPS) * w_ref[...]).astype(o_ref.dtype)


def _rmsnorm(x, w, *, x_col0=0, width=None, tm=512, out_dtype=F32):
    M = x.shape[0]
    width = x.shape[1] if width is None else width
    assert x_col0 % width == 0 and M % tm == 0
    cj = x_col0 // width
    return pl.pallas_call(
        _rmsnorm_kernel,
        grid=(M // tm,),
        in_specs=[pl.BlockSpec((tm, width), lambda i: (i, cj)),
                  pl.BlockSpec((1, width), lambda i: (0, 0))],
        out_specs=pl.BlockSpec((tm, width), lambda i: (i, 0)),
        out_shape=jax.ShapeDtypeStruct((M, width), out_dtype),
        compiler_params=_cparams(("arbitrary",)),
        name="rmsnorm",
    )(x, w.reshape(1, width))


def _swap_pairs(x):
    n = x.shape[-1]
    nxt = pltpu.roll(x, n - 1, axis=1)
    prv = pltpu.roll(x, 1, axis=1)
    lane = lax.broadcasted_iota(jnp.int32, x.shape, 1)
    return jnp.where((lane & 1) == 0, nxt, prv)


def _mla_kernel(*refs, latent, n_alias):
    refs = _drop_alias_refs(refs, (11 if latent else 5) + n_alias, n_alias)
    it = iter(refs)
    qn_ref = next(it)
    qr_ref = next(it)
    g_ref = next(it)
    kv_ref = next(it)
    sm_ref = next(it)
    if latent:
        kvc_ref = next(it)
        krc_ref = next(it)
        cq_ref = next(it)
        sq_ref = next(it)
        ck_ref = next(it)
        sk_ref = next(it)
    o_ref = next(it)

    scale = float(QK_NOPE + QK_ROPE) ** -0.5
    qr = qr_ref[...]
    sm = sm_ref[...]
    if latent:
        qr = qr * cq_ref[...] + _swap_pairs(qr) * sq_ref[...]
        sm = sm * ck_ref[...] + _swap_pairs(sm) * sk_ref[...]
    kr = sm[:, SM["k_r"]:SM["k_r"] + QK_ROPE].astype(BF16)
    if latent:
        krc = krc_ref[...].astype(BF16)

    for h in range(MLA_HEADS):
        q_h = jnp.concatenate([qn_ref[:, h * QK_NOPE:(h + 1) * QK_NOPE],
                               qr[:, h * QK_ROPE:(h + 1) * QK_ROPE].astype(BF16)], axis=1)
        c0 = h * (QK_NOPE + V_HEAD)
        v_h = kv_ref[:, c0 + QK_NOPE:c0 + QK_NOPE + V_HEAD]
        s_own = _dot_nt(q_h, jnp.concatenate([kv_ref[:, c0:c0 + QK_NOPE], kr], axis=1)) * scale
        mx = jnp.max(s_own, axis=-1, keepdims=True)
        if latent:
            vc_h = kvc_ref[:, c0 + QK_NOPE:c0 + QK_NOPE + V_HEAD]
            s_ctx = _dot_nt(q_h, jnp.concatenate([kvc_ref[:, c0:c0 + QK_NOPE], krc], axis=1)) * scale
            mx = jnp.maximum(mx, jnp.max(s_ctx, axis=-1, keepdims=True))
            p_ctx = jnp.exp(s_ctx - mx)
        p_own = jnp.exp(s_own - mx)
        den = jnp.sum(p_own, axis=-1, keepdims=True)
        if latent:
            den = den + jnp.sum(p_ctx, axis=-1, keepdims=True)
        o_h = _dot(p_own.astype(BF16), v_h)
        if latent:
            o_h = o_h + _dot(p_ctx.astype(BF16), vc_h)
        g_h = g_ref[:, h * V_HEAD:(h + 1) * V_HEAD]
        o_ref[:, h * V_HEAD:(h + 1) * V_HEAD] = (o_h * (1.0 / den) * _silu(g_h)).astype(o_ref.dtype)


def _mla(qn, qr, P, kv, *, n_tok, n_batch, S, row0, latent, kv_ctx_row0=0, krope_ctx=None, tables=None,
         out_prev=None):
    tq = min(S, 256)
    nq = S // tq
    rb = row0 // tq
    sb = row0 // S
    HW = MLA_HEADS * (QK_NOPE + V_HEAD)
    g_reg, g_col = SEG["g_mla"]
    gj = g_col // MLA_WIDTH
    smj = SMALL_COL0 // SMALL_W
    in_specs = [
        pl.BlockSpec((tq, MLA_HEADS * QK_NOPE), lambda b, i: (rb + b * nq + i, 0)),
        pl.BlockSpec((tq, MLA_HEADS * QK_ROPE), lambda b, i: (rb + b * nq + i, 0)),
        pl.BlockSpec((tq, MLA_WIDTH), lambda b, i: (rb + b * nq + i, gj)),
        pl.BlockSpec((S, HW), lambda b, i: (sb + b, 0)),
        pl.BlockSpec((S, SMALL_W), lambda b, i: (sb + b, smj)),
    ]
    args = [qn, qr, P[g_reg], kv, P["s"]]
    if latent:
        Tc = krope_ctx.shape[1]
        cb = kv_ctx_row0 // Tc
        cos_q, sin_q, cos_k, sin_k = tables
        in_specs += [
            pl.BlockSpec((Tc, HW), lambda b, i: (cb + b, 0)),
            pl.BlockSpec((None, Tc, QK_ROPE), lambda b, i: (b, 0, 0)),
            pl.BlockSpec((tq, MLA_HEADS * QK_ROPE), lambda b, i: (i, 0)),
            pl.BlockSpec((tq, MLA_HEADS * QK_ROPE), lambda b, i: (i, 0)),
            pl.BlockSpec((S, SMALL_W), lambda b, i: (0, 0)),
            pl.BlockSpec((S, SMALL_W), lambda b, i: (0, 0)),
        ]
        args += [kv, krope_ctx, cos_q, sin_q, cos_k, sin_k]
    aliases = {}
    if out_prev is not None:
        aliases[len(args)] = 0
        in_specs.append(pl.BlockSpec(memory_space=pl.ANY))
        args.append(out_prev)
    return pl.pallas_call(
        functools.partial(_mla_kernel, latent=latent, n_alias=len(aliases)),
        grid=(n_batch, nq),
        in_specs=in_specs,
        out_specs=pl.BlockSpec((tq, MLA_WIDTH), lambda b, i: (rb + b * nq + i, 0)),
        out_shape=jax.ShapeDtypeStruct((n_tok, MLA_WIDTH), BF16),
        input_output_aliases=aliases,
        compiler_params=_cparams(("arbitrary", "arbitrary")),
        name="mla_latent" if latent else "mla_context",
    )(*args)


def _rope_tables(S):
    rows = jnp.repeat(jnp.arange(S // GRID_W, dtype=F32), GRID_W)
    cols = jnp.tile(jnp.arange(GRID_W, dtype=F32), S // GRID_W)
    n_freq = QK_ROPE // 4
    inv = ROPE_BASE ** (-jnp.arange(n_freq, dtype=F32) / n_freq)
    ang = jnp.concatenate([rows[:, None] * inv, cols[:, None] * inv], axis=-1)
    cos = jnp.repeat(jnp.cos(ang), 2, axis=-1)
    sin = jnp.repeat(jnp.sin(ang), 2, axis=-1)
    sign = jnp.tile(jnp.array([-1.0, 1.0], F32), QK_ROPE // 2)
    sin = sin * sign
    cos_q = jnp.tile(cos, (1, MLA_HEADS))
    sin_q = jnp.tile(sin, (1, MLA_HEADS))
    lo, hi = SM["k_r"], SMALL_W - SM["k_r"] - QK_ROPE
    cos_k = jnp.concatenate([jnp.ones((S, lo), F32), cos, jnp.ones((S, hi), F32)], axis=-1)
    sin_k = jnp.concatenate([jnp.zeros((S, lo), F32), sin, jnp.zeros((S, hi), F32)], axis=-1)
    return cos_q, sin_q, cos_k, sin_k


def _gla_kernel(*refs, S, C, hps, has_state, emit_state, n_alias):
    refs = _drop_alias_refs(refs, 8 + (2 if has_state else 0) + n_alias, n_alias)
    it = iter(refs)
    q_ref = next(it)
    k_ref = next(it)
    v_ref = next(it)
    sm_ref = next(it)
    gg_ref = next(it)
    w2_ref = next(it)
    b2_ref = next(it)
    nw_ref = next(it)
    s0_refs = (next(it), next(it)) if has_state else None
    o_ref = next(it)
    so_refs = (next(it), next(it)) if emit_state else None
    vt_ref = next(it)
    la_ref = next(it)
    acc_ref = next(it)
    st_ref = next(it)

    n = S // C
    DK, DV = GLA_DK, GLA_DV
    scale = float(GLA_DK) ** -0.5
    sm = sm_ref[...]
    for d, name in enumerate(("ga_f", "ga_b")):
        ga = sm[:, SM[name]:SM[name] + GLA_GATE_RANK].astype(BF16)
        xg = _dot(ga, w2_ref[d].astype(BF16)) + b2_ref[d]
        la_ref[d] = -_softplus(-xg) * (1.0 / GLA_GATE_NORM)
    for hh in range(hps):
        for c in range(n):
            vt_ref[hh, c] = v_ref[c * C:(c + 1) * C, hh * DV:(hh + 1) * DV].T.astype(BF16)
        for d in range(2):
            if has_state:
                st_ref[d, hh] = s0_refs[d][hh].T
            else:
                st_ref[d, hh] = jnp.zeros((DV, DK), F32)

    row = lax.broadcasted_iota(jnp.int32, (C, C), 0)
    col = lax.broadcasted_iota(jnp.int32, (C, C), 1)
    lower = row >= col
    upper = row <= col
    tri = (jnp.where(lower, 1.0, 0.0).astype(BF16), jnp.where(upper, 1.0, 0.0).astype(BF16))
    mid = C // 2

    def chunk(c, d, hh):
        r0 = pl.multiple_of(c * C, C)
        kcols = slice(hh * DK, (hh + 1) * DK)
        vcols = slice(hh * DV, (hh + 1) * DV)
        q = q_ref[pl.ds(r0, C), kcols] * scale
        k = k_ref[pl.ds(r0, C), kcols]
        v = v_ref[pl.ds(r0, C), vcols].astype(BF16)
        g = la_ref[d, pl.ds(r0, C), kcols]
        b = _tri_left(tri[d], g)
        tot = b[C - 1:C, :] if d == 0 else b[0:1, :]
        bm = b[mid:mid + 1, :]
        st = st_ref[d, hh]
        o = _dot_nt((q * jnp.exp(b)).astype(BF16), st.astype(BF16))
        qa = (q * jnp.exp(b - bm)).astype(BF16)
        ka = (k * jnp.exp(bm - b)).astype(BF16)
        att = _dot_nt(qa, ka)
        att = jnp.where(lower if d == 0 else upper, att, 0.0)
        o = o + _dot(att.astype(BF16), v)
        kd = (k * jnp.exp(tot - b)).astype(BF16)
        st_ref[d, hh] = st * jnp.exp(tot) + _dot(vt_ref[hh, c], kd)
        acc_ref[d, pl.ds(r0, C), vcols] = o

    def both(i, carry):
        for hh in range(hps):
            chunk(i, 0, hh)
            chunk(n - 1 - i, 1, hh)
        return carry

    lax.fori_loop(0, n, both, 0)

    for hh in range(hps):
        vcols = slice(hh * DV, (hh + 1) * DV)
        o = acc_ref[0, :, vcols] + acc_ref[1, :, vcols]
        ms = jnp.mean(o * o, axis=-1, keepdims=True)
        y = o * lax.rsqrt(ms + EPS) * nw_ref[...]
        o_ref[:, vcols] = (y * _silu(gg_ref[:, vcols])).astype(o_ref.dtype)
        if emit_state:
            for d in range(2):
                so_refs[d][hh] = st_ref[d, hh].T


def _gla(P, w2, b2, nw, *, layer, n_tok, n_batch, S, row0, states_in=None, states_out=None, out_prev=None):
    C = GLA_CHUNK
    sb = row0 // S
    hps = 2 if S <= SEQ_BLOCK else 1
    kw, vw = hps * GLA_DK, hps * GLA_DV
    col = {n: SEG[n][1] for n in ("q_l", "k_l", "v_l", "g_gla")}
    qj, kj, vj, gj = col["q_l"] // kw, col["k_l"] // kw, col["v_l"] // vw, col["g_gla"] // vw
    smj = SMALL_COL0 // SMALL_W
    in_specs = [
        pl.BlockSpec((S, kw), lambda b, h: (sb + b, qj + h)),
        pl.BlockSpec((S, kw), lambda b, h: (sb + b, kj + h)),
        pl.BlockSpec((S, vw), lambda b, h: (sb + b, vj + h)),
        pl.BlockSpec((S, SMALL_W), lambda b, h: (sb + b, smj)),
        pl.BlockSpec((S, vw), lambda b, h: (sb + b, gj + h)),
        pl.BlockSpec((None, 2, GLA_GATE_RANK, kw), lambda b, h: (layer, 0, 0, h)),
        pl.BlockSpec((None, 2, 1, kw), lambda b, h: (layer, 0, 0, h)),
        pl.BlockSpec((None, 1, GLA_DV), lambda b, h: (layer, 0, 0)),
    ]
    args = [P[SEG["q_l"][0]], P[SEG["k_l"][0]], P[SEG["v_l"][0]], P["s"], P[SEG["g_gla"][0]],
            w2, b2.reshape(DEPTH, 2, 1, GLA_KEY), nw.reshape(DEPTH, 1, GLA_DV)]
    has_state = states_in is not None
    st_block = (None, None, hps, GLA_DK, GLA_DV)
    if has_state:
        for s in states_in:
            in_specs.append(pl.BlockSpec(st_block, lambda b, h: (b, layer, h, 0, 0)))
            args.append(s)
    out_specs = [pl.BlockSpec((S, vw), lambda b, h: (sb + b, h))]
    out_shape = [jax.ShapeDtypeStruct((n_tok, GLA_WIDTH), BF16)]
    aliased = [out_prev]
    emit_state = states_out is not None
    if emit_state:
        for s in states_out:
            out_specs.append(pl.BlockSpec(st_block, lambda b, h: (b, layer, h, 0, 0)))
            out_shape.append(jax.ShapeDtypeStruct((n_batch, DEPTH, GLA_HEADS, GLA_DK, GLA_DV), F32))
            aliased.append(s)
    aliases = {}
    for oi, buf in enumerate(aliased):
        if buf is not None:
            aliases[len(args)] = oi
            in_specs.append(pl.BlockSpec(memory_space=pl.ANY))
            args.append(buf)
    n = S // C
    return pl.pallas_call(
        functools.partial(_gla_kernel, S=S, C=C, hps=hps, has_state=has_state, emit_state=emit_state,
                          n_alias=len(aliases)),
        grid=(n_batch, GLA_HEADS // hps),
        in_specs=in_specs,
        out_specs=out_specs,
        out_shape=out_shape,
        input_output_aliases=aliases,
        scratch_shapes=[pltpu.VMEM((hps, n, GLA_DV, C), BF16), pltpu.VMEM((2, S, kw), F32),
                        pltpu.VMEM((2, S, vw), F32), pltpu.VMEM((2, hps, GLA_DV, GLA_DK), F32)],
        compiler_params=_cparams(("arbitrary", "arbitrary")),
        name="gla_latent" if has_state else "gla_context",
    )(*args)


CONV_PAD = SUBLANES
CONV_ROWS = 256


def _conv_silu(dst_ref, src_ref, w_ref, b_ref, xp_ref, S):
    W = src_ref.shape[1]
    left = (SSM_CONV - 1) // 2
    xp_ref[0:CONV_PAD, 0:W] = jnp.zeros((CONV_PAD, W), F32)
    xp_ref[CONV_PAD + S:CONV_PAD + S + CONV_PAD, 0:W] = jnp.zeros((CONV_PAD, W), F32)
    xp_ref[CONV_PAD:CONV_PAD + S, 0:W] = src_ref[...]
    for r in range(0, S, CONV_ROWS):
        acc = b_ref[...] + xp_ref[CONV_PAD - left + r:CONV_PAD - left + r + CONV_ROWS, 0:W] * w_ref[0:1, :]
        for j in range(1, SSM_CONV):
            r0 = CONV_PAD - left + j + r
            acc = acc + xp_ref[r0:r0 + CONV_ROWS, 0:W] * w_ref[j:j + 1, :]
        dst_ref[r:r + CONV_ROWS, :] = _silu(acc)
def _ssd_kernel(*refs, S, C, has_state, emit_state, n_alias):
    refs = _drop_alias_refs(refs, 15 + (2 if has_state else 0) + n_alias, n_alias)
    it = iter(refs)
    xin_ref = next(it)
    bin_ref = next(it)
    cin_ref = next(it)
    cw_refs = (next(it), next(it), next(it))
    cb_refs = (next(it), next(it), next(it))
    z_ref = next(it)
    dtc_ref = next(it)
    dtr_ref = next(it)
    pc_ref = next(it)
    pr_ref = next(it)
    d_ref = next(it)
    s0_refs = (next(it), next(it)) if has_state else None
    y_ref = next(it)
    so_refs = (next(it), next(it)) if emit_state else None
    bt_ref = next(it)
    acc_ref = next(it)
    st_ref = next(it)
    xp_ref = next(it)
    xh_ref = next(it)
    x_ref = next(it)
    b_ref = next(it)
    c_ref = next(it)

    for dst, src, w, b in zip((x_ref, b_ref, c_ref), (xin_ref, bin_ref, cin_ref), cw_refs, cb_refs):
        _conv_silu(dst, src, w, b, xp_ref, S)

    n = S // C
    P = SSM_HEADDIM
    for r in range(0, S, CONV_ROWS):
        xr = x_ref[r:r + CONV_ROWS, :]
        for j in range(SSM_HPG):
            xh_ref[j, r:r + CONV_ROWS, :] = xr[:, j * P:(j + 1) * P].astype(BF16)
    for c in range(n):
        bt_ref[c] = b_ref[c * C:(c + 1) * C, :].T
    for d in range(2):
        if has_state:
            s0_t = jnp.concatenate([s0_refs[d][j] for j in range(SSM_HPG)], axis=0).T
            for j in range(SSM_HPG):
                st_ref[d, j] = s0_t[:, j * P:(j + 1) * P]
        else:
            st_ref[d] = jnp.zeros((SSM_HPG, SSM_STATE, P), F32)

    row = lax.broadcasted_iota(jnp.int32, (C, C), 0)
    col = lax.broadcasted_iota(jnp.int32, (C, C), 1)
    lower = row >= col
    upper = row <= col
    tri_l = jnp.where(lower, 1.0, 0.0).astype(BF16)
    tri_u = jnp.where(upper, 1.0, 0.0).astype(BF16)

    def chunk(c, d):
        r0 = pl.multiple_of(c * C, C)
        bm = b_ref[pl.ds(r0, C), :].astype(BF16)
        cm = c_ref[pl.ds(r0, C), :]
        bt = bt_ref[c]
        a_c = -jnp.exp(pc_ref[d, 1])
        a_r = -jnp.exp(pr_ref[d, 1])
        dt_c = _softplus(dtc_ref[d, pl.ds(r0, C), :] + pc_ref[d, 0])
        dt_r = _softplus(dtr_ref[d, c] + pr_ref[d, 0])
        if d == 0:
            cum_c = _tri_left(tri_l, dt_c * a_c)
            cum_r = _tri_right(dt_r * a_r, tri_u)
        else:
            cum_c = _tri_left(tri_u, dt_c * a_c)
            cum_r = _tri_right(dt_r * a_r, tri_l)
        mask = lower if d == 0 else upper
        last_r = cum_r[:, C - 1:C] if d == 0 else cum_r[:, 0:1]
        u_r = dt_r * jnp.exp(last_r - cum_r)
        e_last = jnp.exp(last_r)
        cb = _dot_nt(cm.astype(BF16), bm)
        cq2 = cum_c * LOG2_E
        cr2 = cum_r * LOG2_E - jnp.log2(dt_r)
        for j in range(SSM_HPG):
            cum_q = jnp.broadcast_to(cq2[:, j:j + 1], (C, C))
            dec = jnp.where(mask, jnp.exp2(cum_q - cr2[j:j + 1, :]), 0.0)
            lhs = jnp.concatenate([(cb * dec).astype(BF16),
                                   (cm * jnp.exp2(cum_q)).astype(BF16)], axis=1)
            x_j = xh_ref[j, pl.ds(r0, C), :]
            st_j = st_ref[d, j]
            acc_ref[d, j, pl.ds(r0, C), :] = _dot(lhs, jnp.concatenate([x_j, st_j.astype(BF16)], axis=0))
            st_ref[d, j] = st_j * e_last[j:j + 1, :] + _dot((bt * u_r[j:j + 1, :]).astype(BF16), x_j)

    def both(i, carry):
        chunk(i, 0)
        chunk(n - 1 - i, 1)
        return carry

    lax.fori_loop(0, n, both, 0)

    for r in range(0, S, CONV_ROWS):
        rows = slice(r, r + CONV_ROWS)
        y = jnp.concatenate([acc_ref[0, j, rows, :] + acc_ref[1, j, rows, :] for j in range(SSM_HPG)], axis=-1)
        y = y + x_ref[rows, :] * d_ref[...]
        y_ref[rows, :] = y * _silu(z_ref[rows, :])
    if emit_state:
        for d in range(2):
            s_t = jnp.concatenate([st_ref[d, j] for j in range(SSM_HPG)], axis=1).T
            for j in range(SSM_HPG):
                so_refs[d][j] = s_t[j * P:(j + 1) * P, :]


def _ssd(P, cw, cb, dt_col, dt_row, p_col, p_row, d_lane, *, layer, n_tok, n_batch, S, row0,
         states_in=None, states_out=None, out_prev=None):
    C = SSM_CHUNK
    sb = row0 // S
    z_reg, z_col = SEG["z"]
    x_reg, x_col = SEG["xbc"]
    zj = z_col // SSM_GW
    xj = x_col // SSM_GW
    bj = SSM_WIDTH // SSM_STATE
    cj = (SSM_WIDTH + SSM_BC) // SSM_STATE
    xbj = x_col // SSM_STATE

    def taps(width, j0):
        return pl.BlockSpec((None, SSM_CONV, width), lambda b, g: (layer, 0, j0 + g))

    def bias(width, j0):
        return pl.BlockSpec((None, 1, width), lambda b, g: (layer, 0, j0 + g))

    in_specs = [
        pl.BlockSpec((S, SSM_GW), lambda b, g: (sb + b, xj + g)),
        pl.BlockSpec((S, SSM_STATE), lambda b, g: (sb + b, xbj + bj + g)),
        pl.BlockSpec((S, SSM_STATE), lambda b, g: (sb + b, xbj + cj + g)),
        taps(SSM_GW, 0), taps(SSM_STATE, bj), taps(SSM_STATE, cj),
        bias(SSM_GW, 0), bias(SSM_STATE, bj), bias(SSM_STATE, cj),
        pl.BlockSpec((S, SSM_GW), lambda b, g: (sb + b, zj + g)),
        pl.BlockSpec((2, None, S, SSM_HPG), lambda b, g: (0, g, sb + b, 0)),
        pl.BlockSpec((2, None, S // C, SSM_HPG, C), lambda b, g: (0, g, sb + b, 0, 0)),
        pl.BlockSpec((None, 2, 2, None, 1, SSM_HPG), lambda b, g: (layer, 0, 0, g, 0, 0)),
        pl.BlockSpec((None, 2, 2, None, SSM_HPG, 1), lambda b, g: (layer, 0, 0, g, 0, 0)),
        pl.BlockSpec((None, 1, SSM_GW), lambda b, g: (layer, 0, g)),
    ]
    cb3 = cb.reshape(DEPTH, 1, SSM_CONV_CH)
    args = [P[x_reg], P[x_reg], P[x_reg], cw, cw, cw, cb3, cb3, cb3, P[z_reg], dt_col, dt_row, p_col, p_row,
            d_lane]
    has_state = states_in is not None
    st_block = (None, None, SSM_HPG, SSM_HEADDIM, SSM_STATE)
    if has_state:
        for s in states_in:
            in_specs.append(pl.BlockSpec(st_block, lambda b, g: (b, layer, g, 0, 0)))
            args.append(s)
    out_specs = [pl.BlockSpec((S, SSM_GW), lambda b, g: (sb + b, g))]
    out_shape = [jax.ShapeDtypeStruct((n_tok, SSM_WIDTH), F32)]
    aliased = [out_prev]
    emit_state = states_out is not None
    if emit_state:
        for s in states_out:
            out_specs.append(pl.BlockSpec(st_block, lambda b, g: (b, layer, g, 0, 0)))
            out_shape.append(jax.ShapeDtypeStruct((n_batch, DEPTH, SSM_HEADS, SSM_HEADDIM, SSM_STATE), F32))
            aliased.append(s)
    aliases = {}
    for oi, buf in enumerate(aliased):
        if buf is not None:
            aliases[len(args)] = oi
            in_specs.append(pl.BlockSpec(memory_space=pl.ANY))
            args.append(buf)
    n = S // C
    return pl.pallas_call(
        functools.partial(_ssd_kernel, S=S, C=C, has_state=has_state, emit_state=emit_state,
                          n_alias=len(aliases)),
        grid=(n_batch, SSM_GROUPS),
        in_specs=in_specs,
        out_specs=out_specs,
        out_shape=out_shape,
        input_output_aliases=aliases,
        scratch_shapes=[pltpu.VMEM((n, SSM_STATE, C), F32), pltpu.VMEM((2, SSM_HPG, S, SSM_HEADDIM), F32),
                        pltpu.VMEM((2, SSM_HPG, SSM_STATE, SSM_HEADDIM), F32),
                        pltpu.VMEM((S + 2 * CONV_PAD, SSM_GW), F32),
                        pltpu.VMEM((SSM_HPG, S, SSM_HEADDIM), BF16), pltpu.VMEM((S, SSM_GW), F32),
                        pltpu.VMEM((S, SSM_STATE), F32), pltpu.VMEM((S, SSM_STATE), F32)],
        compiler_params=_cparams(("arbitrary", "arbitrary")),
        name="ssd_latent" if has_state else "ssd_context",
    )(*args)


def _gather_small_w_in(w_in_t):
    names = REGIONS["s"]
    rows = [w_in_t[:, _IN_OFF[n]:_IN_OFF[n] + _IN_SIZE[n], :] for n in names]
    used = sum(_IN_SIZE[n] for n in names)
    rows.append(jnp.zeros((DEPTH, REGION_W["s"] - used, D_MODEL), w_in_t.dtype))
    return jnp.concatenate(rows, axis=1)


def kernel(x_prompt, x_sample, cache_mla_ckv, cache_mla_krope, state_gla_fwd, state_gla_bwd,
           state_ssm_fwd, state_ssm_bwd, c, c_ctx, w_mod, b_mod, norm_w, w_in, q_a_norm, w_q_b,
           kv_a_norm, w_kv_b, gla_w_gate2, gla_b_gate, gla_norm, ssm_conv_w, ssm_conv_b,
           ssm_dt_bias, ssm_a_log, ssm_d, ssm_norm, w_br_mla, w_br_gla, w_br_ssm, w_out, final_norm):
    nb_p, s_p, _ = x_prompt.shape
    nb_l, s_l, _ = x_sample.shape
    n_p = nb_p * s_p
    n_l = nb_l * s_l
    n_tok = n_p + n_l
    assert s_p == SEQ_BLOCK and s_l % SEQ_BLOCK == 0 and n_p % s_l == 0
    t_c = cache_mla_ckv.shape[2]
    assert t_c == SEQ_BLOCK

    x = jnp.concatenate([x_prompt.reshape(n_p, D_MODEL), x_sample.reshape(n_l, D_MODEL)], axis=0)

    w_in_t = jnp.swapaxes(w_in, 1, 2)
    w_in_s = _gather_small_w_in(w_in_t)
    wq = w_q_b.reshape(DEPTH, Q_LORA, MLA_HEADS, QK_NOPE + QK_ROPE)
    wq_nope = wq[..., :QK_NOPE].reshape(DEPTH, Q_LORA, MLA_HEADS * QK_NOPE)
    wq_rope = wq[..., QK_NOPE:].reshape(DEPTH, Q_LORA, MLA_HEADS * QK_ROPE)
    tables = _rope_tables(s_l)
    p_ssm = jnp.stack([ssm_dt_bias, ssm_a_log], axis=2).reshape(DEPTH, 2, 2, SSM_GROUPS, SSM_HPG)
    p_col = p_ssm[:, :, :, :, None, :]
    p_row = p_ssm[:, :, :, :, :, None]
    d_lane = jnp.repeat(ssm_d, SSM_HEADDIM, axis=-1).reshape(DEPTH, 1, SSM_WIDTH)
    blk_mod = np.concatenate([np.zeros(n_p // SEQ_BLOCK, np.int32),
                              1 + np.repeat(np.arange(nb_l, dtype=np.int32), s_l // SEQ_BLOCK)])
    cond = jnp.concatenate([c_ctx[None, :], c], axis=0)
    cond = _silu(cond)
    cond = jnp.concatenate([cond, jnp.zeros((8 - cond.shape[0] % 8, D_MODEL), F32)], axis=0)

    gla_f = gla_b = ssm_f = ssm_b = None
    ckv_l, kr_l = [], []
    for l in range(DEPTH):
        mod = _matmul(cond, w_mod, name="mm_mod", layer=l, n_out=3 * D_MODEL, tm=cond.shape[0], tn=1536,
                      out_dtype=F32)
        mod = mod[:1 + nb_l] + b_mod[l][None, :]
        mod_rows = mod[blk_mod][:, None, :]
        gate_rows = mod_rows[:, :, 2 * D_MODEL:]

        h = _prenorm(x, norm_w[l], mod_rows)
        P = {}
        for r in ("a", "b", "c"):
            first = REGIONS[r][0]
            P[r] = _matmul(h, w_in_t, name="mm_in_" + r, layer=l, w_t=True, w_col0=_IN_OFF[first],
                           n_out=REGION_W[r], tm=1536, tn=IN_TN, out_dtype=F32)
        P["s"] = _matmul(h, w_in_s, name="mm_in_s", layer=l, w_t=True, n_out=REGION_W["s"], tm=1536, tn=IN_TN,
                         out_dtype=F32)
        small = P["s"][:, SMALL_COL0:SMALL_COL0 + SMALL_W]

        qa_col, kva_col = SEG["q_a"][1], SEG["kv_a"][1]
        qn = _matmul(P["s"], wq_nope, name="mm_q_nope", x_col0=qa_col, k=Q_LORA, layer=l,
                     n_out=MLA_HEADS * QK_NOPE, tm=1024, tn=1024, out_dtype=BF16, gain=q_a_norm[l])
        qr = _matmul(P["s"], wq_rope, name="mm_q_rope", x_col0=qa_col, k=Q_LORA, layer=l,
                     n_out=MLA_HEADS * QK_ROPE, tm=1024, tn=1024, out_dtype=F32, gain=q_a_norm[l])
        ckv = _rmsnorm(P["s"], kv_a_norm[l], x_col0=kva_col, width=KV_LORA)
        ckv_all = jnp.concatenate([ckv, cache_mla_ckv[:, l].reshape(nb_l * t_c, KV_LORA)], axis=0)
        kv = _matmul(ckv_all, w_kv_b, name="mm_kv", layer=l, n_out=MLA_HEADS * (QK_NOPE + V_HEAD),
                     tm=ckv_all.shape[0] // 4, tn=1024, out_dtype=BF16)
        o_mla = _mla(qn, qr, P, kv, n_tok=n_tok, n_batch=nb_p, S=s_p, row0=0, latent=False)
        o_mla = _mla(qn, qr, P, kv, n_tok=n_tok, n_batch=nb_l, S=s_l, row0=n_p, latent=True,
                     kv_ctx_row0=n_tok, krope_ctx=cache_mla_krope[:, l], tables=tables, out_prev=o_mla)
        ckv_l.append(ckv[:n_p].reshape(nb_p, s_p, KV_LORA))
        kr_l.append(small[:n_p, SM["k_r"]:SM["k_r"] + QK_ROPE].reshape(nb_p, s_p, QK_ROPE))

        o_gla, gla_f, gla_b = _gla(P, gla_w_gate2, gla_b_gate, gla_norm, layer=l, n_tok=n_tok, n_batch=nb_p,
                                   S=s_p, row0=0, states_out=(gla_f, gla_b))
        (o_gla,) = _gla(P, gla_w_gate2, gla_b_gate, gla_norm, layer=l, n_tok=n_tok, n_batch=nb_l, S=s_l,
                        row0=n_p, states_in=(state_gla_fwd, state_gla_bwd), out_prev=o_gla)

        dt = small[:, SM["dt_f"]:SM["dt_f"] + 2 * SSM_HEADS].reshape(n_tok, 2, SSM_GROUPS, SSM_HPG)
        dt_col = jnp.transpose(dt, (1, 2, 0, 3))
        dt_row = jnp.transpose(dt.reshape(n_tok // SSM_CHUNK, SSM_CHUNK, 2, SSM_GROUPS, SSM_HPG),
                               (2, 3, 0, 4, 1))
        yz, ssm_f, ssm_b = _ssd(P, ssm_conv_w, ssm_conv_b, dt_col, dt_row, p_col, p_row, d_lane, layer=l,
                                n_tok=n_tok, n_batch=nb_p, S=s_p, row0=0, states_out=(ssm_f, ssm_b))
        (yz,) = _ssd(P, ssm_conv_w, ssm_conv_b, dt_col, dt_row, p_col, p_row, d_lane, layer=l, n_tok=n_tok,
                     n_batch=nb_l, S=s_l, row0=n_p, states_in=(state_ssm_fwd, state_ssm_bwd), out_prev=yz)

        mg = _matmul(o_mla, w_br_mla, name="mm_br_mla", layer=l, n_out=D_MODEL, tm=512, tn=1024, out_dtype=F32,
                     ep="sig", m=P["c"], m_col0=SEG["m_mla"][1])
        mg = _matmul(o_gla, w_br_gla, name="mm_br_gla", layer=l, n_out=D_MODEL, tm=512, tn=1024, out_dtype=F32,
                     ep="sigadd", m=P["c"], m_col0=SEG["m_gla"][1], prev=mg)
        mg = _matmul(yz, w_br_ssm, name="mm_br_ssm", layer=l, n_out=D_MODEL, tm=512, tn=512, out_dtype=BF16,
                     gain=ssm_norm[l], ep="sigadd", m=P["c"], m_col0=SEG["m_ssm"][1], prev=mg)
        x = _matmul(mg, w_out, name="mm_out", layer=l, n_out=D_MODEL, tm=512, tn=1024, out_dtype=F32,
                    ep="resid", res=x, gate_rows=gate_rows)

    y = _rmsnorm(x, final_norm)
    y_prompt = y[:n_p].reshape(nb_p, s_p, D_MODEL)
    y_sample = y[n_p:].reshape(nb_l, s_l, D_MODEL)
    return (y_prompt, y_sample, jnp.stack(ckv_l, axis=1), jnp.stack(kr_l, axis=1),
            gla_f, gla_b, ssm_f, ssm_b)
```

```python
import functools

import jax
import jax.numpy as jnp
import numpy as np
from jax import lax
from jax.experimental import pallas as pl
from jax.experimental.pallas import tpu as pltpu

F32 = jnp.float32
BF16 = jnp.bfloat16

D_MODEL = 2048
DEPTH = 4
GRID_W = 64
EPS = 1e-6
LOG2_E = 1.4426950408889634
MLA_HEADS = 16
QK_NOPE = 128
QK_ROPE = 64
V_HEAD = 128
Q_LORA = 512
KV_LORA = 256
MLA_WIDTH = MLA_HEADS * V_HEAD
ROPE_BASE = 10000.0
GLA_HEADS = 4
GLA_DK = 256
GLA_DV = 512
GLA_KEY = GLA_HEADS * GLA_DK
GLA_WIDTH = GLA_HEADS * GLA_DV
GLA_GATE_RANK = 16
GLA_GATE_NORM = 16.0
SSM_HEADS = 64
SSM_HEADDIM = 64
SSM_WIDTH = SSM_HEADS * SSM_HEADDIM
SSM_GROUPS = 8
SSM_HPG = SSM_HEADS // SSM_GROUPS
SSM_GW = SSM_HPG * SSM_HEADDIM
SSM_STATE = 128
SSM_CONV = 5
SSM_BC = SSM_GROUPS * SSM_STATE
SSM_CONV_CH = SSM_WIDTH + 2 * SSM_BC

IN_SIZES = (Q_LORA, KV_LORA, QK_ROPE, MLA_WIDTH,
            GLA_KEY, GLA_KEY, GLA_WIDTH, GLA_GATE_RANK, GLA_GATE_RANK, GLA_WIDTH,
            SSM_WIDTH, SSM_CONV_CH, SSM_HEADS, SSM_HEADS,
            D_MODEL, D_MODEL, D_MODEL)
IN_NAMES = ("q_a", "kv_a", "k_r", "g_mla", "q_l", "k_l", "v_l", "ga_f", "ga_b", "g_gla",
            "z", "xbc", "dt_f", "dt_b", "m_mla", "m_gla", "m_ssm")
_IN_OFF = dict(zip(IN_NAMES, np.cumsum((0,) + IN_SIZES[:-1]).tolist()))
_IN_SIZE = dict(zip(IN_NAMES, IN_SIZES))

REGIONS = {
    "a": ("g_mla", "q_l", "k_l", "v_l"),
    "b": ("g_gla", "z", "xbc"),
    "c": ("m_mla", "m_gla", "m_ssm"),
    "s": ("q_a", "kv_a", "k_r", "ga_f", "ga_b", "dt_f", "dt_b"),
}
IN_TN = 1024
SEG = {}
REGION_W = {}
for _r, _names in REGIONS.items():
    _o = 0
    for _n in _names:
        SEG[_n] = (_r, _o)
        _o += _IN_SIZE[_n]
    REGION_W[_r] = -(-_o // IN_TN) * IN_TN
    if _r != "s":
        assert _o % IN_TN == 0
        assert all(_IN_OFF[_names[i]] + _IN_SIZE[_names[i]] == _IN_OFF[_names[i + 1]]
                   for i in range(len(_names) - 1))
SMALL_W = 256
SMALL_COL0 = SEG["k_r"][1]
assert SMALL_COL0 % SMALL_W == 0 and SEG["dt_b"][1] + _IN_SIZE["dt_b"] <= SMALL_COL0 + SMALL_W
SM = {n: SEG[n][1] - SMALL_COL0 for n in ("k_r", "ga_f", "ga_b", "dt_f", "dt_b")}

SUBLANES = 8
SEQ_BLOCK = 256
GLA_CHUNK = 128
SSM_CHUNK = 128
V7X_VMEM_LIMIT = 56 * 1024 * 1024


def _cparams(sem):
    return pltpu.CompilerParams(dimension_semantics=sem, vmem_limit_bytes=V7X_VMEM_LIMIT)


def _silu(x):
    return x * (1.0 / (1.0 + jnp.exp(-x)))


def _sigmoid(x):
    return 1.0 / (1.0 + jnp.exp(-x))


def _softplus(x):
    return jnp.maximum(x, 0.0) + jnp.log(1.0 + jnp.exp(-jnp.abs(x)))


def _dot(a, b):
    return jnp.dot(a, b, preferred_element_type=F32)


def _dot_nt(a, b):
    return lax.dot_general(a, b, (((1,), (1,)), ((), ())), preferred_element_type=F32)


def _split3(x):
    hi = x.astype(BF16)
    r1 = x - hi.astype(F32)
    mid = r1.astype(BF16)
    lo = (r1 - mid.astype(F32)).astype(BF16)
    return hi, mid, lo


def _tri_left(tri, x):
    hi, mid, lo = _split3(x)
    return _dot(tri, hi) + _dot(tri, mid) + _dot(tri, lo)


def _tri_right(x, tri):
    hi, mid, lo = _split3(x)
    return _dot(hi, tri) + _dot(mid, tri) + _dot(lo, tri)


def _drop_alias_refs(refs, n_in, n_alias):
    refs = list(refs)
    return refs[:n_in - n_alias] + refs[n_in:]


def _mm_kernel(*refs, rms, ep, cast_w, w_t):
    it = iter(refs)
    x_ref = next(it)
    w_ref = next(it)
    gain_ref = next(it) if rms else None
    m_ref = next(it) if ep in ("sig", "sigadd") else None
    prev_ref = next(it) if ep == "sigadd" else None
    res_ref = next(it) if ep == "resid" else None
    gate_ref = next(it) if ep == "resid" else None
    o_ref = next(it)
    wbf_ref = next(it) if cast_w else None

    if cast_w:
        @pl.when(pl.program_id(1) == 0)
        def _():
            wbf_ref[...] = (w_ref[0] if w_t else w_ref[...]).astype(BF16)
        w = wbf_ref[...]
    else:
        w = w_ref[...]
    x = x_ref[...]
    if rms:
        xf = x.astype(F32)
        ms = jnp.mean(xf * xf, axis=-1, keepdims=True)
        x = xf * lax.rsqrt(ms + EPS) * gain_ref[...]
    acc = _dot_nt(x.astype(BF16), w) if w_t else _dot(x.astype(BF16), w)
    if ep == "sig":
        acc = _sigmoid(m_ref[...]) * acc
    elif ep == "sigadd":
        acc = prev_ref[...].astype(F32) + _sigmoid(m_ref[...]) * acc
    elif ep == "resid":
        for r in range(gate_ref.shape[0]):
            rows = slice(r * SEQ_BLOCK, (r + 1) * SEQ_BLOCK)
            o_ref[rows, :] = res_ref[rows, :] + gate_ref[r] * acc[rows, :]
        return
    o_ref[...] = acc.astype(o_ref.dtype)


def _matmul(x, w, *, name, layer, n_out, tm, tn, out_dtype, w_col0=0, w_t=False, gain=None, x_col0=0, k=None,
            ep=None, m=None, m_col0=0, prev=None, res=None, gate_rows=None):
    M = x.shape[0]
    K = x.shape[1] if k is None else k
    assert M % tm == 0 and n_out % tn == 0 and m_col0 % tn == 0 and x_col0 % K == 0
    xj = x_col0 // K
    cast_w = w.dtype != BF16
    rms = gain is not None
    mj = m_col0 // tn
    if w_t:
        assert w_col0 % SUBLANES == 0
        w_spec = pl.BlockSpec((pl.Element(1), pl.Element(tn), pl.Element(K)),
                              lambda j, i: (layer, pl.multiple_of(w_col0 + j * tn, SUBLANES), 0))
    else:
        assert w_col0 % tn == 0
        wj = w_col0 // tn
        w_spec = pl.BlockSpec((None, K, tn), lambda j, i: (layer, 0, j + wj))

    in_specs = [pl.BlockSpec((tm, K), lambda j, i: (i, xj)), w_spec]
    args = [x, w]
    if rms:
        in_specs.append(pl.BlockSpec((1, K), lambda j, i: (0, 0)))
        args.append(gain.reshape(1, K))
    if ep in ("sig", "sigadd"):
        in_specs.append(pl.BlockSpec((tm, tn), lambda j, i: (i, j + mj)))
        args.append(m)
    if ep == "sigadd":
        in_specs.append(pl.BlockSpec((tm, tn), lambda j, i: (i, j)))
        args.append(prev)
    if ep == "resid":
        assert tm % SEQ_BLOCK == 0 and out_dtype == res.dtype
        in_specs.append(pl.BlockSpec((tm, tn), lambda j, i: (i, j)))
        args.append(res)
        in_specs.append(pl.BlockSpec((tm // SEQ_BLOCK, 1, tn), lambda j, i: (i, 0, j)))
        args.append(gate_rows)
    scratch = [pltpu.VMEM((tn, K) if w_t else (K, tn), BF16)] if cast_w else []
    return pl.pallas_call(
        functools.partial(_mm_kernel, rms=rms, ep=ep, cast_w=cast_w, w_t=w_t),
        grid=(n_out // tn, M // tm),
        in_specs=in_specs,
        out_specs=pl.BlockSpec((tm, tn), lambda j, i: (i, j)),
        out_shape=jax.ShapeDtypeStruct((M, n_out), out_dtype),
        scratch_shapes=scratch,
        compiler_params=_cparams(("arbitrary", "arbitrary")),
        name=name,
    )(*args)


def _prenorm_kernel(x_ref, nw_ref, mod_ref, o_ref):
    x = x_ref[...]
    ms = jnp.mean(x * x, axis=-1, keepdims=True)
    y = x * lax.rsqrt(ms + EPS) * nw_ref[...]
    shift = mod_ref[:, 0:D_MODEL]
    scale = mod_ref[:, D_MODEL:2 * D_MODEL]
    o_ref[...] = (y * (1.0 + scale) + shift).astype(o_ref.dtype)


def _prenorm(x, norm_w, mod_rows):
    M = x.shape[0]
    return pl.pallas_call(
        _prenorm_kernel,
        grid=(M // SEQ_BLOCK,),
        in_specs=[pl.BlockSpec((SEQ_BLOCK, D_MODEL), lambda i: (i, 0)),
                  pl.BlockSpec((1, D_MODEL), lambda i: (0, 0)),
                  pl.BlockSpec((None, 1, 3 * D_MODEL), lambda i: (i, 0, 0))],
        out_specs=pl.BlockSpec((SEQ_BLOCK, D_MODEL), lambda i: (i, 0)),
        out_shape=jax.ShapeDtypeStruct((M, D_MODEL), BF16),
        compiler_params=_cparams(("arbitrary",)),
        name="prenorm",
    )(x, norm_w.reshape(1, D_MODEL), mod_rows)


def _rmsnorm_kernel(x_ref, w_ref, o_ref):
    x = x_ref[...]
    ms = jnp.mean(x * x, axis=-1, keepdims=True)
    o_ref[...] = (x * lax.rsqrt(ms + EPS) * w_ref[...]).astype(o_ref.dtype)


def _rmsnorm(x, w, *, x_col0=0, width=None, tm=512, out_dtype=F32):
    M = x.shape[0]
    width = x.shape[1] if width is None else width
    assert x_col0 % width == 0 and M % tm == 0
    cj = x_col0 // width
    return pl.pallas_call(
        _rmsnorm_kernel,
        grid=(M // tm,),
        in_specs=[pl.BlockSpec((tm, width), lambda i: (i, cj)),
                  pl.BlockSpec((1, width), lambda i: (0, 0))],
        out_specs=pl.BlockSpec((tm, width), lambda i: (i, 0)),
        out_shape=jax.ShapeDtypeStruct((M, width), out_dtype),
        compiler_params=_cparams(("arbitrary",)),
        name="rmsnorm",
    )(x, w.reshape(1, width))


def _swap_pairs(x):
    n = x.shape[-1]
    nxt = pltpu.roll(x, n - 1, axis=1)
    prv = pltpu.roll(x, 1, axis=1)
    lane = lax.broadcasted_iota(jnp.int32, x.shape, 1)
    return jnp.where((lane & 1) == 0, nxt, prv)


def _mla_kernel(*refs, latent, n_alias):
    refs = _drop_alias_refs(refs, (11 if latent else 5) + n_alias, n_alias)
    it = iter(refs)
    qn_ref = next(it)
    qr_ref = next(it)
    g_ref = next(it)
    kv_ref = next(it)
    sm_ref = next(it)
    if latent:
        kvc_ref = next(it)
        krc_ref = next(it)
        cq_ref = next(it)
        sq_ref = next(it)
        ck_ref = next(it)
        sk_ref = next(it)
    o_ref = next(it)

    scale = float(QK_NOPE + QK_ROPE) ** -0.5
    qr = qr_ref[...]
    sm = sm_ref[...]
    if latent:
        qr = qr * cq_ref[...] + _swap_pairs(qr) * sq_ref[...]
        sm = sm * ck_ref[...] + _swap_pairs(sm) * sk_ref[...]
    kr = sm[:, SM["k_r"]:SM["k_r"] + QK_ROPE].astype(BF16)
    if latent:
        krc = krc_ref[...].astype(BF16)

    for h in range(MLA_HEADS):
        q_h = jnp.concatenate([qn_ref[:, h * QK_NOPE:(h + 1) * QK_NOPE],
                               qr[:, h * QK_ROPE:(h + 1) * QK_ROPE].astype(BF16)], axis=1)
        c0 = h * (QK_NOPE + V_HEAD)
        v_h = kv_ref[:, c0 + QK_NOPE:c0 + QK_NOPE + V_HEAD]
        s_own = _dot_nt(q_h, jnp.concatenate([kv_ref[:, c0:c0 + QK_NOPE], kr], axis=1)) * scale
        mx = jnp.max(s_own, axis=-1, keepdims=True)
        if latent:
            vc_h = kvc_ref[:, c0 + QK_NOPE:c0 + QK_NOPE + V_HEAD]
            s_ctx = _dot_nt(q_h, jnp.concatenate([kvc_ref[:, c0:c0 + QK_NOPE], krc], axis=1)) * scale
            mx = jnp.maximum(mx, jnp.max(s_ctx, axis=-1, keepdims=True))
            p_ctx = jnp.exp(s_ctx - mx)
        p_own = jnp.exp(s_own - mx)
        den = jnp.sum(p_own, axis=-1, keepdims=True)
        if latent:
            den = den + jnp.sum(p_ctx, axis=-1, keepdims=True)
        o_h = _dot(p_own.astype(BF16), v_h)
        if latent:
            o_h = o_h + _dot(p_ctx.astype(BF16), vc_h)
        g_h = g_ref[:, h * V_HEAD:(h + 1) * V_HEAD]
        o_ref[:, h * V_HEAD:(h + 1) * V_HEAD] = (o_h * (1.0 / den) * _silu(g_h)).astype(o_ref.dtype)


def _mla(qn, qr, P, kv, *, n_tok, n_batch, S, row0, latent, kv_ctx_row0=0, krope_ctx=None, tables=None,
         out_prev=None):
    tq = min(S, 256)
    nq = S // tq
    rb = row0 // tq
    sb = row0 // S
    HW = MLA_HEADS * (QK_NOPE + V_HEAD)
    g_reg, g_col = SEG["g_mla"]
    gj = g_col // MLA_WIDTH
    smj = SMALL_COL0 // SMALL_W
    in_specs = [
        pl.BlockSpec((tq, MLA_HEADS * QK_NOPE), lambda b, i: (rb + b * nq + i, 0)),
        pl.BlockSpec((tq, MLA_HEADS * QK_ROPE), lambda b, i: (rb + b * nq + i, 0)),
        pl.BlockSpec((tq, MLA_WIDTH), lambda b, i: (rb + b * nq + i, gj)),
        pl.BlockSpec((S, HW), lambda b, i: (sb + b, 0)),
        pl.BlockSpec((S, SMALL_W), lambda b, i: (sb + b, smj)),
    ]
    args = [qn, qr, P[g_reg], kv, P["s"]]
    if latent:
        Tc = krope_ctx.shape[1]
        cb = kv_ctx_row0 // Tc
        cos_q, sin_q, cos_k, sin_k = tables
        in_specs += [
            pl.BlockSpec((Tc, HW), lambda b, i: (cb + b, 0)),
            pl.BlockSpec((None, Tc, QK_ROPE), lambda b, i: (b, 0, 0)),
            pl.BlockSpec((tq, MLA_HEADS * QK_ROPE), lambda b, i: (i, 0)),
            pl.BlockSpec((tq, MLA_HEADS * QK_ROPE), lambda b, i: (i, 0)),
            pl.BlockSpec((S, SMALL_W), lambda b, i: (0, 0)),
            pl.BlockSpec((S, SMALL_W), lambda b, i: (0, 0)),
        ]
        args += [kv, krope_ctx, cos_q, sin_q, cos_k, sin_k]
    aliases = {}
    if out_prev is not None:
        aliases[len(args)] = 0
        in_specs.append(pl.BlockSpec(memory_space=pl.ANY))
        args.append(out_prev)
    return pl.pallas_call(
        functools.partial(_mla_kernel, latent=latent, n_alias=len(aliases)),
        grid=(n_batch, nq),
        in_specs=in_specs,
        out_specs=pl.BlockSpec((tq, MLA_WIDTH), lambda b, i: (rb + b * nq + i, 0)),
        out_shape=jax.ShapeDtypeStruct((n_tok, MLA_WIDTH), BF16),
        input_output_aliases=aliases,
        compiler_params=_cparams(("arbitrary", "arbitrary")),
        name="mla_latent" if latent else "mla_context",
    )(*args)


def _rope_tables(S):
    rows = jnp.repeat(jnp.arange(S // GRID_W, dtype=F32), GRID_W)
    cols = jnp.tile(jnp.arange(GRID_W, dtype=F32), S // GRID_W)
    n_freq = QK_ROPE // 4
    inv = ROPE_BASE ** (-jnp.arange(n_freq, dtype=F32) / n_freq)
    ang = jnp.concatenate([rows[:, None] * inv, cols[:, None] * inv], axis=-1)
    cos = jnp.repeat(jnp.cos(ang), 2, axis=-1)
    sin = jnp.repeat(jnp.sin(ang), 2, axis=-1)
    sign = jnp.tile(jnp.array([-1.0, 1.0], F32), QK_ROPE // 2)
    sin = sin * sign
    cos_q = jnp.tile(cos, (1, MLA_HEADS))
    sin_q = jnp.tile(sin, (1, MLA_HEADS))
    lo, hi = SM["k_r"], SMALL_W - SM["k_r"] - QK_ROPE
    cos_k = jnp.concatenate([jnp.ones((S, lo), F32), cos, jnp.ones((S, hi), F32)], axis=-1)
    sin_k = jnp.concatenate([jnp.zeros((S, lo), F32), sin, jnp.zeros((S, hi), F32)], axis=-1)
    return cos_q, sin_q, cos_k, sin_k


def _gla_kernel(*refs, S, C, hps, has_state, emit_state, n_alias):
    refs = _drop_alias_refs(refs, 8 + (2 if has_state else 0) + n_alias, n_alias)
    it = iter(refs)
    q_ref = next(it)
    k_ref = next(it)
    v_ref = next(it)
    sm_ref = next(it)
    gg_ref = next(it)
    w2_ref = next(it)
    b2_ref = next(it)
    nw_ref = next(it)
    s0_refs = (next(it), next(it)) if has_state else None
    o_ref = next(it)
    so_refs = (next(it), next(it)) if emit_state else None
    vt_ref = next(it)
    la_ref = next(it)
    acc_ref = next(it)
    st_ref = next(it)

    n = S // C
    DK, DV = GLA_DK, GLA_DV
    scale = float(GLA_DK) ** -0.5
    sm = sm_ref[...]
    for d, name in enumerate(("ga_f", "ga_b")):
        ga = sm[:, SM[name]:SM[name] + GLA_GATE_RANK].astype(BF16)
        xg = _dot(ga, w2_ref[d].astype(BF16)) + b2_ref[d]
        la_ref[d] = -_softplus(-xg) * (1.0 / GLA_GATE_NORM)
    for hh in range(hps):
        for c in range(n):
            vt_ref[hh, c] = v_ref[c * C:(c + 1) * C, hh * DV:(hh + 1) * DV].T.astype(BF16)
        for d in range(2):
            if has_state:
                st_ref[d, hh] = s0_refs[d][hh].T
            else:
                st_ref[d, hh] = jnp.zeros((DV, DK), F32)

    row = lax.broadcasted_iota(jnp.int32, (C, C), 0)
    col = lax.broadcasted_iota(jnp.int32, (C, C), 1)
    lower = row >= col
    upper = row <= col
    tri = (jnp.where(lower, 1.0, 0.0).astype(BF16), jnp.where(upper, 1.0, 0.0).astype(BF16))
    mid = C // 2

    def chunk(c, d, hh):
        r0 = pl.multiple_of(c * C, C)
        kcols = slice(hh * DK, (hh + 1) * DK)
        vcols = slice(hh * DV, (hh + 1) * DV)
        q = q_ref[pl.ds(r0, C), kcols] * scale
        k = k_ref[pl.ds(r0, C), kcols]
        v = v_ref[pl.ds(r0, C), vcols].astype(BF16)
        g = la_ref[d, pl.ds(r0, C), kcols]
        b = _tri_left(tri[d], g)
        tot = b[C - 1:C, :] if d == 0 else b[0:1, :]
        bm = b[mid:mid + 1, :]
        st = st_ref[d, hh]
        o = _dot_nt((q * jnp.exp(b)).astype(BF16), st.astype(BF16))
        qa = (q * jnp.exp(b - bm)).astype(BF16)
        ka = (k * jnp.exp(bm - b)).astype(BF16)
        att = _dot_nt(qa, ka)
        att = jnp.where(lower if d == 0 else upper, att, 0.0)
        o = o + _dot(att.astype(BF16), v)
        kd = (k * jnp.exp(tot - b)).astype(BF16)
        st_ref[d, hh] = st * jnp.exp(tot) + _dot(vt_ref[hh, c], kd)
        acc_ref[d, pl.ds(r0, C), vcols] = o

    def both(i, carry):
        for hh in range(hps):
            chunk(i, 0, hh)
            chunk(n - 1 - i, 1, hh)
        return carry

    lax.fori_loop(0, n, both, 0)

    for hh in range(hps):
        vcols = slice(hh * DV, (hh + 1) * DV)
        o = acc_ref[0, :, vcols] + acc_ref[1, :, vcols]
        ms = jnp.mean(o * o, axis=-1, keepdims=True)
        y = o * lax.rsqrt(ms + EPS) * nw_ref[...]
        o_ref[:, vcols] = (y * _silu(gg_ref[:, vcols])).astype(o_ref.dtype)
        if emit_state:
            for d in range(2):
                so_refs[d][hh] = st_ref[d, hh].T


def _gla(P, w2, b2, nw, *, layer, n_tok, n_batch, S, row0, states_in=None, states_out=None, out_prev=None):
    C = GLA_CHUNK
    sb = row0 // S
    hps = 2 if S <= SEQ_BLOCK else 1
    kw, vw = hps * GLA_DK, hps * GLA_DV
    col = {n: SEG[n][1] for n in ("q_l", "k_l", "v_l", "g_gla")}
    qj, kj, vj, gj = col["q_l"] // kw, col["k_l"] // kw, col["v_l"] // vw, col["g_gla"] // vw
    smj = SMALL_COL0 // SMALL_W
    in_specs = [
        pl.BlockSpec((S, kw), lambda b, h: (sb + b, qj + h)),
        pl.BlockSpec((S, kw), lambda b, h: (sb + b, kj + h)),
        pl.BlockSpec((S, vw), lambda b, h: (sb + b, vj + h)),
        pl.BlockSpec((S, SMALL_W), lambda b, h: (sb + b, smj)),
        pl.BlockSpec((S, vw), lambda b, h: (sb + b, gj + h)),
        pl.BlockSpec((None, 2, GLA_GATE_RANK, kw), lambda b, h: (layer, 0, 0, h)),
        pl.BlockSpec((None, 2, 1, kw), lambda b, h: (layer, 0, 0, h)),
        pl.BlockSpec((None, 1, GLA_DV), lambda b, h: (layer, 0, 0)),
    ]
    args = [P[SEG["q_l"][0]], P[SEG["k_l"][0]], P[SEG["v_l"][0]], P["s"], P[SEG["g_gla"][0]],
            w2, b2.reshape(DEPTH, 2, 1, GLA_KEY), nw.reshape(DEPTH, 1, GLA_DV)]
    has_state = states_in is not None
    st_block = (None, None, hps, GLA_DK, GLA_DV)
    if has_state:
        for s in states_in:
            in_specs.append(pl.BlockSpec(st_block, lambda b, h: (b, layer, h, 0, 0)))
            args.append(s)
    out_specs = [pl.BlockSpec((S, vw), lambda b, h: (sb + b, h))]
    out_shape = [jax.ShapeDtypeStruct((n_tok, GLA_WIDTH), BF16)]
    aliased = [out_prev]
    emit_state = states_out is not None
    if emit_state:
        for s in states_out:
            out_specs.append(pl.BlockSpec(st_block, lambda b, h: (b, layer, h, 0, 0)))
            out_shape.append(jax.ShapeDtypeStruct((n_batch, DEPTH, GLA_HEADS, GLA_DK, GLA_DV), F32))
            aliased.append(s)
    aliases = {}
    for oi, buf in enumerate(aliased):
        if buf is not None:
            aliases[len(args)] = oi
            in_specs.append(pl.BlockSpec(memory_space=pl.ANY))
            args.append(buf)
    n = S // C
    return pl.pallas_call(
        functools.partial(_gla_kernel, S=S, C=C, hps=hps, has_state=has_state, emit_state=emit_state,
                          n_alias=len(aliases)),
        grid=(n_batch, GLA_HEADS // hps),
        in_specs=in_specs,
        out_specs=out_specs,
        out_shape=out_shape,
        input_output_aliases=aliases,
        scratch_shapes=[pltpu.VMEM((hps, n, GLA_DV, C), BF16), pltpu.VMEM((2, S, kw), F32),
                        pltpu.VMEM((2, S, vw), F32), pltpu.VMEM((2, hps, GLA_DV, GLA_DK), F32)],
        compiler_params=_cparams(("arbitrary", "arbitrary")),
        name="gla_latent" if has_state else "gla_context",
    )(*args)


CONV_PAD = SUBLANES
CONV_ROWS = 256


def _conv_silu(dst_ref, src_ref, w_ref, b_ref, xp_ref, S):
    W = src_ref.shape[1]
    left = (SSM_CONV - 1) // 2
    xp_ref[0:CONV_PAD, 0:W] = jnp.zeros((CONV_PAD, W), F32)
    xp_ref[CONV_PAD + S:CONV_PAD + S + CONV_PAD, 0:W] = jnp.zeros((CONV_PAD, W), F32)
    xp_ref[CONV_PAD:CONV_PAD + S, 0:W] = src_ref[...]
    for r in range(0, S, CONV_ROWS):
        acc = b_ref[...] + xp_ref[CONV_PAD - left + r:CONV_PAD - left + r + CONV_ROWS, 0:W] * w_ref[0:1, :]
        for j in range(1, SSM_CONV):
            r0 = CONV_PAD - left + j + r
            acc = acc + xp_ref[r0:r0 + CONV_ROWS, 0:W] * w_ref[j:j + 1, :]
        dst_ref[r:r + CONV_ROWS, :] = _silu(acc)
def _ssd_kernel(*refs, S, C, has_state, emit_state, n_alias):
    refs = _drop_alias_refs(refs, 14 + (2 if has_state else 0) + n_alias, n_alias)
    it = iter(refs)
    xin_ref = next(it)
    bin_ref = next(it)
    cin_ref = next(it)
    cw_refs = (next(it), next(it), next(it))
    cb_refs = (next(it), next(it), next(it))
    z_ref = next(it)
    sm_ref = next(it)
    pc_ref = next(it)
    pr_ref = next(it)
    d_ref = next(it)
    s0_refs = (next(it), next(it)) if has_state else None
    y_ref = next(it)
    so_refs = (next(it), next(it)) if emit_state else None
    bt_ref = next(it)
    acc_ref = next(it)
    st_ref = next(it)
    xp_ref = next(it)
    xh_ref = next(it)
    x_ref = next(it)
    b_ref = next(it)
    c_ref = next(it)
    dtc_ref = next(it)
    dtr_ref = next(it)

    for dst, src, w, b in zip((x_ref, b_ref, c_ref), (xin_ref, bin_ref, cin_ref), cw_refs, cb_refs):
        _conv_silu(dst, src, w, b, xp_ref, S)

    n = S // C
    P = SSM_HEADDIM
    lane_of = lax.broadcasted_iota(jnp.int32, (SMALL_W, SMALL_W), 0)
    want_c = lax.broadcasted_iota(jnp.int32, (SMALL_W, SMALL_W), 1)
    lane_r = lax.broadcasted_iota(jnp.int32, (SSM_HPG, SMALL_W), 1)
    want_r = lax.broadcasted_iota(jnp.int32, (SSM_HPG, SMALL_W), 0)
    for d in range(2):
        lane0 = SM["dt_f"] + d * SSM_HEADS + pl.program_id(1) * SSM_HPG
        sel_c = jnp.where(lane_of == want_c + lane0, 1.0, 0.0).astype(BF16)
        sel_r = jnp.where(lane_r == want_r + lane0, 1.0, 0.0).astype(BF16)
        for c in range(n):
            rows = slice(c * C, (c + 1) * C)
            parts = _split3(sm_ref[rows, :])
            dtc_ref[d, rows, :] = _dot(parts[0], sel_c) + _dot(parts[1], sel_c) + _dot(parts[2], sel_c)
            dtr_ref[d, c] = _dot_nt(sel_r, parts[0]) + _dot_nt(sel_r, parts[1]) + _dot_nt(sel_r, parts[2])
    for r in range(0, S, CONV_ROWS):
        xr = x_ref[r:r + CONV_ROWS, :]
        for j in range(SSM_HPG):
            xh_ref[j, r:r + CONV_ROWS, :] = xr[:, j * P:(j + 1) * P].astype(BF16)
    for c in range(n):
        bt_ref[c] = b_ref[c * C:(c + 1) * C, :].T
    for d in range(2):
        if has_state:
            s0_t = jnp.concatenate([s0_refs[d][j] for j in range(SSM_HPG)], axis=0).T
            for j in range(SSM_HPG):
                st_ref[d, j] = s0_t[:, j * P:(j + 1) * P]
        else:
            st_ref[d] = jnp.zeros((SSM_HPG, SSM_STATE, P), F32)

    row = lax.broadcasted_iota(jnp.int32, (C, C), 0)
    col = lax.broadcasted_iota(jnp.int32, (C, C), 1)
    lower = row >= col
    upper = row <= col
    tri_l = jnp.where(lower, 1.0, 0.0).astype(BF16)
    tri_u = jnp.where(upper, 1.0, 0.0).astype(BF16)

    def chunk(c, d):
        r0 = pl.multiple_of(c * C, C)
        bm = b_ref[pl.ds(r0, C), :].astype(BF16)
        cm = c_ref[pl.ds(r0, C), :]
        bt = bt_ref[c]
        a_c = -jnp.exp(pc_ref[d, 1])
        a_r = -jnp.exp(pr_ref[d, 1])
        dt_c = _softplus(dtc_ref[d, pl.ds(r0, C), 0:SSM_HPG] + pc_ref[d, 0])
        dt_r = _softplus(dtr_ref[d, c] + pr_ref[d, 0])
        if d == 0:
            cum_c = _tri_left(tri_l, dt_c * a_c)
            cum_r = _tri_right(dt_r * a_r, tri_u)
        else:
            cum_c = _tri_left(tri_u, dt_c * a_c)
            cum_r = _tri_right(dt_r * a_r, tri_l)
        mask = lower if d == 0 else upper
        last_r = cum_r[:, C - 1:C] if d == 0 else cum_r[:, 0:1]
        u_r = dt_r * jnp.exp(last_r - cum_r)
        e_last = jnp.exp(last_r)
        cb = _dot_nt(cm.astype(BF16), bm)
        cq2 = cum_c * LOG2_E
        cr2 = cum_r * LOG2_E - jnp.log2(dt_r)
        for j in range(SSM_HPG):
            cum_q = jnp.broadcast_to(cq2[:, j:j + 1], (C, C))
            dec = jnp.where(mask, jnp.exp2(cum_q - cr2[j:j + 1, :]), 0.0)
            lhs = jnp.concatenate([(cb * dec).astype(BF16),
                                   (cm * jnp.exp2(cum_q)).astype(BF16)], axis=1)
            x_j = xh_ref[j, pl.ds(r0, C), :]
            st_j = st_ref[d, j]
            acc_ref[d, j, pl.ds(r0, C), :] = _dot(lhs, jnp.concatenate([x_j, st_j.astype(BF16)], axis=0))
            st_ref[d, j] = st_j * e_last[j:j + 1, :] + _dot((bt * u_r[j:j + 1, :]).astype(BF16), x_j)

    def both(i, carry):
        chunk(i, 0)
        chunk(n - 1 - i, 1)
        return carry

    lax.fori_loop(0, n, both, 0)

    for r in range(0, S, CONV_ROWS):
        rows = slice(r, r + CONV_ROWS)
        y = jnp.concatenate([acc_ref[0, j, rows, :] + acc_ref[1, j, rows, :] for j in range(SSM_HPG)], axis=-1)
        y = y + x_ref[rows, :] * d_ref[...]
        y_ref[rows, :] = y * _silu(z_ref[rows, :])
    if emit_state:
        for d in range(2):
            s_t = jnp.concatenate([st_ref[d, j] for j in range(SSM_HPG)], axis=1).T
            for j in range(SSM_HPG):
                so_refs[d][j] = s_t[j * P:(j + 1) * P, :]


def _ssd(P, cw, cb, p_col, p_row, d_lane, *, layer, n_tok, n_batch, S, row0,
         states_in=None, states_out=None, out_prev=None):
    C = SSM_CHUNK
    sb = row0 // S
    z_reg, z_col = SEG["z"]
    x_reg, x_col = SEG["xbc"]
    zj = z_col // SSM_GW
    xj = x_col // SSM_GW
    bj = SSM_WIDTH // SSM_STATE
    cj = (SSM_WIDTH + SSM_BC) // SSM_STATE
    xbj = x_col // SSM_STATE

    def taps(width, j0):
        return pl.BlockSpec((None, SSM_CONV, width), lambda b, g: (layer, 0, j0 + g))

    def bias(width, j0):
        return pl.BlockSpec((None, 1, width), lambda b, g: (layer, 0, j0 + g))

    in_specs = [
        pl.BlockSpec((S, SSM_GW), lambda b, g: (sb + b, xj + g)),
        pl.BlockSpec((S, SSM_STATE), lambda b, g: (sb + b, xbj + bj + g)),
        pl.BlockSpec((S, SSM_STATE), lambda b, g: (sb + b, xbj + cj + g)),
        taps(SSM_GW, 0), taps(SSM_STATE, bj), taps(SSM_STATE, cj),
        bias(SSM_GW, 0), bias(SSM_STATE, bj), bias(SSM_STATE, cj),
        pl.BlockSpec((S, SSM_GW), lambda b, g: (sb + b, zj + g)),
        pl.BlockSpec((S, SMALL_W), lambda b, g: (sb + b, SMALL_COL0 // SMALL_W)),
        pl.BlockSpec((None, 2, 2, None, 1, SSM_HPG), lambda b, g: (layer, 0, 0, g, 0, 0)),
        pl.BlockSpec((None, 2, 2, None, SSM_HPG, 1), lambda b, g: (layer, 0, 0, g, 0, 0)),
        pl.BlockSpec((None, 1, SSM_GW), lambda b, g: (layer, 0, g)),
    ]
    cb3 = cb.reshape(DEPTH, 1, SSM_CONV_CH)
    args = [P[x_reg], P[x_reg], P[x_reg], cw, cw, cw, cb3, cb3, cb3, P[z_reg], P["s"], p_col, p_row, d_lane]
    has_state = states_in is not None
    st_block = (None, None, SSM_HPG, SSM_HEADDIM, SSM_STATE)
    if has_state:
        for s in states_in:
            in_specs.append(pl.BlockSpec(st_block, lambda b, g: (b, layer, g, 0, 0)))
            args.append(s)
    out_specs = [pl.BlockSpec((S, SSM_GW), lambda b, g: (sb + b, g))]
    out_shape = [jax.ShapeDtypeStruct((n_tok, SSM_WIDTH), F32)]
    aliased = [out_prev]
    emit_state = states_out is not None
    if emit_state:
        for s in states_out:
            out_specs.append(pl.BlockSpec(st_block, lambda b, g: (b, layer, g, 0, 0)))
            out_shape.append(jax.ShapeDtypeStruct((n_batch, DEPTH, SSM_HEADS, SSM_HEADDIM, SSM_STATE), F32))
            aliased.append(s)
    aliases = {}
    for oi, buf in enumerate(aliased):
        if buf is not None:
            aliases[len(args)] = oi
            in_specs.append(pl.BlockSpec(memory_space=pl.ANY))
            args.append(buf)
    n = S // C
    return pl.pallas_call(
        functools.partial(_ssd_kernel, S=S, C=C, has_state=has_state, emit_state=emit_state,
                          n_alias=len(aliases)),
        grid=(n_batch, SSM_GROUPS),
        in_specs=in_specs,
        out_specs=out_specs,
        out_shape=out_shape,
        input_output_aliases=aliases,
        scratch_shapes=[pltpu.VMEM((n, SSM_STATE, C), F32), pltpu.VMEM((2, SSM_HPG, S, SSM_HEADDIM), F32),
                        pltpu.VMEM((2, SSM_HPG, SSM_STATE, SSM_HEADDIM), F32),
                        pltpu.VMEM((S + 2 * CONV_PAD, SSM_GW), F32),
                        pltpu.VMEM((SSM_HPG, S, SSM_HEADDIM), BF16), pltpu.VMEM((S, SSM_GW), F32),
                        pltpu.VMEM((S, SSM_STATE), F32), pltpu.VMEM((S, SSM_STATE), F32),
                        pltpu.VMEM((2, S, SMALL_W), F32), pltpu.VMEM((2, n, SSM_HPG, C), F32)],
        compiler_params=_cparams(("arbitrary", "arbitrary")),
        name="ssd_latent" if has_state else "ssd_context",
    )(*args)


def _gather_small_w_in(w_in_t):
    names = REGIONS["s"]
    rows = [w_in_t[:, _IN_OFF[n]:_IN_OFF[n] + _IN_SIZE[n], :] for n in names]
    used = sum(_IN_SIZE[n] for n in names)
    rows.append(jnp.zeros((DEPTH, REGION_W["s"] - used, D_MODEL), w_in_t.dtype))
    return jnp.concatenate(rows, axis=1)


def kernel(x_prompt, x_sample, cache_mla_ckv, cache_mla_krope, state_gla_fwd, state_gla_bwd,
           state_ssm_fwd, state_ssm_bwd, c, c_ctx, w_mod, b_mod, norm_w, w_in, q_a_norm, w_q_b,
           kv_a_norm, w_kv_b, gla_w_gate2, gla_b_gate, gla_norm, ssm_conv_w, ssm_conv_b,
           ssm_dt_bias, ssm_a_log, ssm_d, ssm_norm, w_br_mla, w_br_gla, w_br_ssm, w_out, final_norm):
    nb_p, s_p, _ = x_prompt.shape
    nb_l, s_l, _ = x_sample.shape
    n_p = nb_p * s_p
    n_l = nb_l * s_l
    n_tok = n_p + n_l
    assert s_p == SEQ_BLOCK and s_l % SEQ_BLOCK == 0 and n_p % s_l == 0
    t_c = cache_mla_ckv.shape[2]
    assert t_c == SEQ_BLOCK

    x = jnp.concatenate([x_prompt.reshape(n_p, D_MODEL), x_sample.reshape(n_l, D_MODEL)], axis=0)

    w_in_t = jnp.swapaxes(w_in, 1, 2)
    w_in_s = _gather_small_w_in(w_in_t)
    wq = w_q_b.reshape(DEPTH, Q_LORA, MLA_HEADS, QK_NOPE + QK_ROPE)
    wq_nope = wq[..., :QK_NOPE].reshape(DEPTH, Q_LORA, MLA_HEADS * QK_NOPE)
    wq_rope = wq[..., QK_NOPE:].reshape(DEPTH, Q_LORA, MLA_HEADS * QK_ROPE)
    tables = _rope_tables(s_l)
    p_ssm = jnp.stack([ssm_dt_bias, ssm_a_log], axis=2).reshape(DEPTH, 2, 2, SSM_GROUPS, SSM_HPG)
    p_col = p_ssm[:, :, :, :, None, :]
    p_row = p_ssm[:, :, :, :, :, None]
    d_lane = jnp.repeat(ssm_d, SSM_HEADDIM, axis=-1).reshape(DEPTH, 1, SSM_WIDTH)
    blk_mod = np.concatenate([np.zeros(n_p // SEQ_BLOCK, np.int32),
                              1 + np.repeat(np.arange(nb_l, dtype=np.int32), s_l // SEQ_BLOCK)])
    cond = jnp.concatenate([c_ctx[None, :], c], axis=0)
    cond = _silu(cond)
    cond = jnp.concatenate([cond, jnp.zeros((8 - cond.shape[0] % 8, D_MODEL), F32)], axis=0)

    gla_f = gla_b = ssm_f = ssm_b = None
    ckv_l, kr_l = [], []
    for l in range(DEPTH):
        mod = _matmul(cond, w_mod, name="mm_mod", layer=l, n_out=3 * D_MODEL, tm=cond.shape[0], tn=1536,
                      out_dtype=F32)
        mod = mod[:1 + nb_l] + b_mod[l][None, :]
        mod_rows = mod[blk_mod][:, None, :]
        gate_rows = mod_rows[:, :, 2 * D_MODEL:]

        h = _prenorm(x, norm_w[l], mod_rows)
        P = {}
        for r in ("a", "b", "c"):
            first = REGIONS[r][0]
            P[r] = _matmul(h, w_in_t, name="mm_in_" + r, layer=l, w_t=True, w_col0=_IN_OFF[first],
                           n_out=REGION_W[r], tm=1536, tn=IN_TN, out_dtype=F32)
        P["s"] = _matmul(h, w_in_s, name="mm_in_s", layer=l, w_t=True, n_out=REGION_W["s"], tm=1536, tn=IN_TN,
                         out_dtype=F32)
        small = P["s"][:, SMALL_COL0:SMALL_COL0 + SMALL_W]

        qa_col, kva_col = SEG["q_a"][1], SEG["kv_a"][1]
        qn = _matmul(P["s"], wq_nope, name="mm_q_nope", x_col0=qa_col, k=Q_LORA, layer=l,
                     n_out=MLA_HEADS * QK_NOPE, tm=1024, tn=1024, out_dtype=BF16, gain=q_a_norm[l])
        qr = _matmul(P["s"], wq_rope, name="mm_q_rope", x_col0=qa_col, k=Q_LORA, layer=l,
                     n_out=MLA_HEADS * QK_ROPE, tm=1024, tn=1024, out_dtype=F32, gain=q_a_norm[l])
        ckv = _rmsnorm(P["s"], kv_a_norm[l], x_col0=kva_col, width=KV_LORA)
        ckv_all = jnp.concatenate([ckv, cache_mla_ckv[:, l].reshape(nb_l * t_c, KV_LORA)], axis=0)
        kv = _matmul(ckv_all, w_kv_b, name="mm_kv", layer=l, n_out=MLA_HEADS * (QK_NOPE + V_HEAD),
                     tm=ckv_all.shape[0] // 4, tn=1024, out_dtype=BF16)
        o_mla = _mla(qn, qr, P, kv, n_tok=n_tok, n_batch=nb_p, S=s_p, row0=0, latent=False)
        o_mla = _mla(qn, qr, P, kv, n_tok=n_tok, n_batch=nb_l, S=s_l, row0=n_p, latent=True,
                     kv_ctx_row0=n_tok, krope_ctx=cache_mla_krope[:, l], tables=tables, out_prev=o_mla)
        ckv_l.append(ckv[:n_p].reshape(nb_p, s_p, KV_LORA))
        kr_l.append(small[:n_p, SM["k_r"]:SM["k_r"] + QK_ROPE].reshape(nb_p, s_p, QK_ROPE))

        o_gla, gla_f, gla_b = _gla(P, gla_w_gate2, gla_b_gate, gla_norm, layer=l, n_tok=n_tok, n_batch=nb_p,
                                   S=s_p, row0=0, states_out=(gla_f, gla_b))
        (o_gla,) = _gla(P, gla_w_gate2, gla_b_gate, gla_norm, layer=l, n_tok=n_tok, n_batch=nb_l, S=s_l,
                        row0=n_p, states_in=(state_gla_fwd, state_gla_bwd), out_prev=o_gla)

        yz, ssm_f, ssm_b = _ssd(P, ssm_conv_w, ssm_conv_b, p_col, p_row, d_lane, layer=l,
                                n_tok=n_tok, n_batch=nb_p, S=s_p, row0=0, states_out=(ssm_f, ssm_b))
        (yz,) = _ssd(P, ssm_conv_w, ssm_conv_b, p_col, p_row, d_lane, layer=l, n_tok=n_tok,
                     n_batch=nb_l, S=s_l, row0=n_p, states_in=(state_ssm_fwd, state_ssm_bwd), out_prev=yz)

        mg = _matmul(o_mla, w_br_mla, name="mm_br_mla", layer=l, n_out=D_MODEL, tm=768, tn=1024, out_dtype=F32,
                     ep="sig", m=P["c"], m_col0=SEG["m_mla"][1])
        mg = _matmul(o_gla, w_br_gla, name="mm_br_gla", layer=l, n_out=D_MODEL, tm=768, tn=1024, out_dtype=F32,
                     ep="sigadd", m=P["c"], m_col0=SEG["m_gla"][1], prev=mg)
        mg = _matmul(yz, w_br_ssm, name="mm_br_ssm", layer=l, n_out=D_MODEL, tm=512, tn=512, out_dtype=BF16,
                     gain=ssm_norm[l], ep="sigadd", m=P["c"], m_col0=SEG["m_ssm"][1], prev=mg)
        x = _matmul(mg, w_out, name="mm_out", layer=l, n_out=D_MODEL, tm=768, tn=1024, out_dtype=F32,
                    ep="resid", res=x, gate_rows=gate_rows)

    y = _rmsnorm(x, final_norm)
    y_prompt = y[:n_p].reshape(nb_p, s_p, D_MODEL)
    y_sample = y[n_p:].reshape(nb_l, s_l, D_MODEL)
    return (y_prompt, y_sample, jnp.stack(ckv_l, axis=1), jnp.stack(kr_l, axis=1),
            gla_f, gla_b, ssm_f, ssm_b)
```

```python
import functools

import jax
import jax.numpy as jnp
import numpy as np
from jax import lax
from jax.experimental import pallas as pl
from jax.experimental.pallas import tpu as pltpu

F32 = jnp.float32
BF16 = jnp.bfloat16

D_MODEL = 2048
DEPTH = 4
GRID_W = 64
EPS = 1e-6
LOG2_E = 1.4426950408889634
MLA_HEADS = 16
QK_NOPE = 128
QK_ROPE = 64
V_HEAD = 128
Q_LORA = 512
KV_LORA = 256
MLA_WIDTH = MLA_HEADS * V_HEAD
ROPE_BASE = 10000.0
GLA_HEADS = 4
GLA_DK = 256
GLA_DV = 512
GLA_KEY = GLA_HEADS * GLA_DK
GLA_WIDTH = GLA_HEADS * GLA_DV
GLA_GATE_RANK = 16
GLA_GATE_NORM = 16.0
SSM_HEADS = 64
SSM_HEADDIM = 64
SSM_WIDTH = SSM_HEADS * SSM_HEADDIM
SSM_GROUPS = 8
SSM_HPG = SSM_HEADS // SSM_GROUPS
SSM_GW = SSM_HPG * SSM_HEADDIM
SSM_STATE = 128
SSM_CONV = 5
SSM_BC = SSM_GROUPS * SSM_STATE
SSM_CONV_CH = SSM_WIDTH + 2 * SSM_BC

IN_SIZES = (Q_LORA, KV_LORA, QK_ROPE, MLA_WIDTH,
            GLA_KEY, GLA_KEY, GLA_WIDTH, GLA_GATE_RANK, GLA_GATE_RANK, GLA_WIDTH,
            SSM_WIDTH, SSM_CONV_CH, SSM_HEADS, SSM_HEADS,
            D_MODEL, D_MODEL, D_MODEL)
IN_NAMES = ("q_a", "kv_a", "k_r", "g_mla", "q_l", "k_l", "v_l", "ga_f", "ga_b", "g_gla",
            "z", "xbc", "dt_f", "dt_b", "m_mla", "m_gla", "m_ssm")
_IN_OFF = dict(zip(IN_NAMES, np.cumsum((0,) + IN_SIZES[:-1]).tolist()))
_IN_SIZE = dict(zip(IN_NAMES, IN_SIZES))

REGIONS = {
    "a": ("g_mla", "q_l", "k_l", "v_l"),
    "b": ("g_gla", "z", "xbc"),
    "c": ("m_mla", "m_gla", "m_ssm"),
    "s": ("q_a", "kv_a", "k_r", "ga_f", "ga_b", "dt_f", "dt_b"),
}
IN_TN = 1024
SEG = {}
REGION_W = {}
for _r, _names in REGIONS.items():
    _o = 0
    for _n in _names:
        SEG[_n] = (_r, _o)
        _o += _IN_SIZE[_n]
    REGION_W[_r] = -(-_o // IN_TN) * IN_TN
    if _r != "s":
        assert _o % IN_TN == 0
        assert all(_IN_OFF[_names[i]] + _IN_SIZE[_names[i]] == _IN_OFF[_names[i + 1]]
                   for i in range(len(_names) - 1))
SMALL_W = 256
SMALL_COL0 = SEG["k_r"][1]
assert SMALL_COL0 % SMALL_W == 0 and SEG["dt_b"][1] + _IN_SIZE["dt_b"] <= SMALL_COL0 + SMALL_W
SM = {n: SEG[n][1] - SMALL_COL0 for n in ("k_r", "ga_f", "ga_b", "dt_f", "dt_b")}

SUBLANES = 8
SEQ_BLOCK = 256
GLA_CHUNK = 128
SSM_CHUNK = 128
V7X_VMEM_LIMIT = 56 * 1024 * 1024


def _cparams(sem):
    return pltpu.CompilerParams(dimension_semantics=sem, vmem_limit_bytes=V7X_VMEM_LIMIT)


def _silu(x):
    return x * (1.0 / (1.0 + jnp.exp(-x)))


def _sigmoid(x):
    return 1.0 / (1.0 + jnp.exp(-x))


def _softplus(x):
    return jnp.maximum(x, 0.0) + jnp.log(1.0 + jnp.exp(-jnp.abs(x)))


def _dot(a, b):
    return jnp.dot(a, b, preferred_element_type=F32)


def _dot_nt(a, b):
    return lax.dot_general(a, b, (((1,), (1,)), ((), ())), preferred_element_type=F32)


def _split3(x):
    hi = x.astype(BF16)
    r1 = x - hi.astype(F32)
    mid = r1.astype(BF16)
    lo = (r1 - mid.astype(F32)).astype(BF16)
    return hi, mid, lo


def _tri_left(tri, x):
    hi, mid, lo = _split3(x)
    return _dot(tri, hi) + _dot(tri, mid) + _dot(tri, lo)


def _tri_right(x, tri):
    hi, mid, lo = _split3(x)
    return _dot(hi, tri) + _dot(mid, tri) + _dot(lo, tri)


def _drop_alias_refs(refs, n_in, n_alias):
    refs = list(refs)
    return refs[:n_in - n_alias] + refs[n_in:]


def _mm_kernel(*refs, rms, ep, cast_w, w_t):
    it = iter(refs)
    x_ref = next(it)
    w_ref = next(it)
    gain_ref = next(it) if rms else None
    m_ref = next(it) if ep in ("sig", "sigadd") else None
    prev_ref = next(it) if ep == "sigadd" else None
    res_ref = next(it) if ep == "resid" else None
    gate_ref = next(it) if ep == "resid" else None
    o_ref = next(it)
    wbf_ref = next(it) if cast_w else None

    if cast_w:
        @pl.when(pl.program_id(1) == 0)
        def _():
            wbf_ref[...] = (w_ref[0] if w_t else w_ref[...]).astype(BF16)
        w = wbf_ref[...]
    else:
        w = w_ref[...]
    x = x_ref[...]
    if rms:
        xf = x.astype(F32)
        ms = jnp.mean(xf * xf, axis=-1, keepdims=True)
        x = xf * lax.rsqrt(ms + EPS) * gain_ref[...]
    acc = _dot_nt(x.astype(BF16), w) if w_t else _dot(x.astype(BF16), w)
    if ep == "sig":
        acc = _sigmoid(m_ref[...]) * acc
    elif ep == "sigadd":
        acc = prev_ref[...].astype(F32) + _sigmoid(m_ref[...]) * acc
    elif ep == "resid":
        for r in range(gate_ref.shape[0]):
            rows = slice(r * SEQ_BLOCK, (r + 1) * SEQ_BLOCK)
            o_ref[rows, :] = res_ref[rows, :] + gate_ref[r] * acc[rows, :]
        return
    o_ref[...] = acc.astype(o_ref.dtype)


def _matmul(x, w, *, name, layer, n_out, tm, tn, out_dtype, w_col0=0, w_t=False, gain=None, x_col0=0, k=None,
            ep=None, m=None, m_col0=0, prev=None, res=None, gate_rows=None, w_single_buffer=False):
    M = x.shape[0]
    K = x.shape[1] if k is None else k
    assert M % tm == 0 and n_out % tn == 0 and m_col0 % tn == 0 and x_col0 % K == 0
    xj = x_col0 // K
    cast_w = w.dtype != BF16
    rms = gain is not None
    mj = m_col0 // tn
    if w_t:
        assert w_col0 % SUBLANES == 0
        w_spec = pl.BlockSpec((pl.Element(1), pl.Element(tn), pl.Element(K)),
                              lambda j, i: (layer, pl.multiple_of(w_col0 + j * tn, SUBLANES), 0))
    else:
        assert w_col0 % tn == 0
        wj = w_col0 // tn
        mode = dict(pipeline_mode=pl.Buffered(1)) if w_single_buffer else {}
        w_spec = pl.BlockSpec((None, K, tn), lambda j, i: (layer, 0, j + wj), **mode)

    in_specs = [pl.BlockSpec((tm, K), lambda j, i: (i, xj)), w_spec]
    args = [x, w]
    if rms:
        in_specs.append(pl.BlockSpec((1, K), lambda j, i: (0, 0)))
        args.append(gain.reshape(1, K))
    if ep in ("sig", "sigadd"):
        in_specs.append(pl.BlockSpec((tm, tn), lambda j, i: (i, j + mj)))
        args.append(m)
    if ep == "sigadd":
        in_specs.append(pl.BlockSpec((tm, tn), lambda j, i: (i, j)))
        args.append(prev)
    if ep == "resid":
        assert tm % SEQ_BLOCK == 0 and out_dtype == res.dtype
        in_specs.append(pl.BlockSpec((tm, tn), lambda j, i: (i, j)))
        args.append(res)
        in_specs.append(pl.BlockSpec((tm // SEQ_BLOCK, 1, tn), lambda j, i: (i, 0, j)))
        args.append(gate_rows)
    scratch = [pltpu.VMEM((tn, K) if w_t else (K, tn), BF16)] if cast_w else []
    return pl.pallas_call(
        functools.partial(_mm_kernel, rms=rms, ep=ep, cast_w=cast_w, w_t=w_t),
        grid=(n_out // tn, M // tm),
        in_specs=in_specs,
        out_specs=pl.BlockSpec((tm, tn), lambda j, i: (i, j)),
        out_shape=jax.ShapeDtypeStruct((M, n_out), out_dtype),
        scratch_shapes=scratch,
        compiler_params=_cparams(("arbitrary", "arbitrary")),
        name=name,
    )(*args)


def _prenorm_kernel(x_ref, nw_ref, mod_ref, o_ref):
    x = x_ref[...]
    ms = jnp.mean(x * x, axis=-1, keepdims=True)
    y = x * lax.rsqrt(ms + EPS) * nw_ref[...]
    shift = mod_ref[:, 0:D_MODEL]
    scale = mod_ref[:, D_MODEL:2 * D_MODEL]
    o_ref[...] = (y * (1.0 + scale) + shift).astype(o_ref.dtype)


def _prenorm(x, norm_w, mod_rows):
    M = x.shape[0]
    return pl.pallas_call(
        _prenorm_kernel,
        grid=(M // SEQ_BLOCK,),
        in_specs=[pl.BlockSpec((SEQ_BLOCK, D_MODEL), lambda i: (i, 0)),
                  pl.BlockSpec((1, D_MODEL), lambda i: (0, 0)),
                  pl.BlockSpec((None, 1, 3 * D_MODEL), lambda i: (i, 0, 0))],
        out_specs=pl.BlockSpec((SEQ_BLOCK, D_MODEL), lambda i: (i, 0)),
        out_shape=jax.ShapeDtypeStruct((M, D_MODEL), BF16),
        compiler_params=_cparams(("arbitrary",)),
        name="prenorm",
    )(x, norm_w.reshape(1, D_MODEL), mod_rows)


def _rmsnorm_kernel(x_ref, w_ref, o_ref):
    x = x_ref[...]
    ms = jnp.mean(x * x, axis=-1, keepdims=True)
    o_ref[...] = (x * lax.rsqrt(ms + EPS) * w_ref[...]).astype(o_ref.dtype)


def _rmsnorm(x, w, *, x_col0=0, width=None, tm=512, out_dtype=F32):
    M = x.shape[0]
    width = x.shape[1] if width is None else width
    assert x_col0 % width == 0 and M % tm == 0
    cj = x_col0 // width
    return pl.pallas_call(
        _rmsnorm_kernel,
        grid=(M // tm,),
        in_specs=[pl.BlockSpec((tm, width), lambda i: (i, cj)),
                  pl.BlockSpec((1, width), lambda i: (0, 0))],
        out_specs=pl.BlockSpec((tm, width), lambda i: (i, 0)),
        out_shape=jax.ShapeDtypeStruct((M, width), out_dtype),
        compiler_params=_cparams(("arbitrary",)),
        name="rmsnorm",
    )(x, w.reshape(1, width))


def _swap_pairs(x):
    n = x.shape[-1]
    nxt = pltpu.roll(x, n - 1, axis=1)
    prv = pltpu.roll(x, 1, axis=1)
    lane = lax.broadcasted_iota(jnp.int32, x.shape, 1)
    return jnp.where((lane & 1) == 0, nxt, prv)


def _mla_kernel(*refs, latent, n_alias):
    refs = _drop_alias_refs(refs, (11 if latent else 5) + n_alias, n_alias)
    it = iter(refs)
    qn_ref = next(it)
    qr_ref = next(it)
    g_ref = next(it)
    kv_ref = next(it)
    sm_ref = next(it)
    if latent:
        kvc_ref = next(it)
        krc_ref = next(it)
        cq_ref = next(it)
        sq_ref = next(it)
        ck_ref = next(it)
        sk_ref = next(it)
    o_ref = next(it)

    scale = float(QK_NOPE + QK_ROPE) ** -0.5
    qr = qr_ref[...]
    sm = sm_ref[...]
    if latent:
        qr = qr * cq_ref[...] + _swap_pairs(qr) * sq_ref[...]
        sm = sm * ck_ref[...] + _swap_pairs(sm) * sk_ref[...]
    kr = sm[:, SM["k_r"]:SM["k_r"] + QK_ROPE].astype(BF16)
    if latent:
        krc = krc_ref[...].astype(BF16)

    for h in range(MLA_HEADS):
        q_h = jnp.concatenate([qn_ref[:, h * QK_NOPE:(h + 1) * QK_NOPE],
                               qr[:, h * QK_ROPE:(h + 1) * QK_ROPE].astype(BF16)], axis=1)
        c0 = h * (QK_NOPE + V_HEAD)
        v_h = kv_ref[:, c0 + QK_NOPE:c0 + QK_NOPE + V_HEAD]
        s_own = _dot_nt(q_h, jnp.concatenate([kv_ref[:, c0:c0 + QK_NOPE], kr], axis=1)) * scale
        mx = jnp.max(s_own, axis=-1, keepdims=True)
        if latent:
            vc_h = kvc_ref[:, c0 + QK_NOPE:c0 + QK_NOPE + V_HEAD]
            s_ctx = _dot_nt(q_h, jnp.concatenate([kvc_ref[:, c0:c0 + QK_NOPE], krc], axis=1)) * scale
            mx = jnp.maximum(mx, jnp.max(s_ctx, axis=-1, keepdims=True))
            p_ctx = jnp.exp(s_ctx - mx)
        p_own = jnp.exp(s_own - mx)
        den = jnp.sum(p_own, axis=-1, keepdims=True)
        if latent:
            den = den + jnp.sum(p_ctx, axis=-1, keepdims=True)
        o_h = _dot(p_own.astype(BF16), v_h)
        if latent:
            o_h = o_h + _dot(p_ctx.astype(BF16), vc_h)
        g_h = g_ref[:, h * V_HEAD:(h + 1) * V_HEAD]
        o_ref[:, h * V_HEAD:(h + 1) * V_HEAD] = (o_h * (1.0 / den) * _silu(g_h)).astype(o_ref.dtype)


def _mla(qn, qr, P, kv, *, n_tok, n_batch, S, row0, latent, kv_ctx_row0=0, krope_ctx=None, tables=None,
         out_prev=None):
    tq = min(S, 256)
    nq = S // tq
    rb = row0 // tq
    sb = row0 // S
    HW = MLA_HEADS * (QK_NOPE + V_HEAD)
    g_reg, g_col = SEG["g_mla"]
    gj = g_col // MLA_WIDTH
    smj = SMALL_COL0 // SMALL_W
    in_specs = [
        pl.BlockSpec((tq, MLA_HEADS * QK_NOPE), lambda b, i: (rb + b * nq + i, 0)),
        pl.BlockSpec((tq, MLA_HEADS * QK_ROPE), lambda b, i: (rb + b * nq + i, 0)),
        pl.BlockSpec((tq, MLA_WIDTH), lambda b, i: (rb + b * nq + i, gj)),
        pl.BlockSpec((S, HW), lambda b, i: (sb + b, 0)),
        pl.BlockSpec((S, SMALL_W), lambda b, i: (sb + b, smj)),
    ]
    args = [qn, qr, P[g_reg], kv, P["s"]]
    if latent:
        Tc = krope_ctx.shape[1]
        cb = kv_ctx_row0 // Tc
        cos_q, sin_q, cos_k, sin_k = tables
        in_specs += [
            pl.BlockSpec((Tc, HW), lambda b, i: (cb + b, 0)),
            pl.BlockSpec((None, Tc, QK_ROPE), lambda b, i: (b, 0, 0)),
            pl.BlockSpec((tq, MLA_HEADS * QK_ROPE), lambda b, i: (i, 0)),
            pl.BlockSpec((tq, MLA_HEADS * QK_ROPE), lambda b, i: (i, 0)),
            pl.BlockSpec((S, SMALL_W), lambda b, i: (0, 0)),
            pl.BlockSpec((S, SMALL_W), lambda b, i: (0, 0)),
        ]
        args += [kv, krope_ctx, cos_q, sin_q, cos_k, sin_k]
    aliases = {}
    if out_prev is not None:
        aliases[len(args)] = 0
        in_specs.append(pl.BlockSpec(memory_space=pl.ANY))
        args.append(out_prev)
    return pl.pallas_call(
        functools.partial(_mla_kernel, latent=latent, n_alias=len(aliases)),
        grid=(n_batch, nq),
        in_specs=in_specs,
        out_specs=pl.BlockSpec((tq, MLA_WIDTH), lambda b, i: (rb + b * nq + i, 0)),
        out_shape=jax.ShapeDtypeStruct((n_tok, MLA_WIDTH), BF16),
        input_output_aliases=aliases,
        compiler_params=_cparams(("arbitrary", "arbitrary")),
        name="mla_latent" if latent else "mla_context",
    )(*args)


def _rope_tables(S):
    rows = jnp.repeat(jnp.arange(S // GRID_W, dtype=F32), GRID_W)
    cols = jnp.tile(jnp.arange(GRID_W, dtype=F32), S // GRID_W)
    n_freq = QK_ROPE // 4
    inv = ROPE_BASE ** (-jnp.arange(n_freq, dtype=F32) / n_freq)
    ang = jnp.concatenate([rows[:, None] * inv, cols[:, None] * inv], axis=-1)
    cos = jnp.repeat(jnp.cos(ang), 2, axis=-1)
    sin = jnp.repeat(jnp.sin(ang), 2, axis=-1)
    sign = jnp.tile(jnp.array([-1.0, 1.0], F32), QK_ROPE // 2)
    sin = sin * sign
    cos_q = jnp.tile(cos, (1, MLA_HEADS))
    sin_q = jnp.tile(sin, (1, MLA_HEADS))
    lo, hi = SM["k_r"], SMALL_W - SM["k_r"] - QK_ROPE
    cos_k = jnp.concatenate([jnp.ones((S, lo), F32), cos, jnp.ones((S, hi), F32)], axis=-1)
    sin_k = jnp.concatenate([jnp.zeros((S, lo), F32), sin, jnp.zeros((S, hi), F32)], axis=-1)
    return cos_q, sin_q, cos_k, sin_k


def _gla_kernel(*refs, S, C, hps, has_state, emit_state, n_alias):
    refs = _drop_alias_refs(refs, 8 + (2 if has_state else 0) + n_alias, n_alias)
    it = iter(refs)
    q_ref = next(it)
    k_ref = next(it)
    v_ref = next(it)
    sm_ref = next(it)
    gg_ref = next(it)
    w2_ref = next(it)
    b2_ref = next(it)
    nw_ref = next(it)
    s0_refs = (next(it), next(it)) if has_state else None
    o_ref = next(it)
    so_refs = (next(it), next(it)) if emit_state else None
    vt_ref = next(it)
    la_ref = next(it)
    acc_ref = next(it)
    st_ref = next(it)

    n = S // C
    DK, DV = GLA_DK, GLA_DV
    scale = float(GLA_DK) ** -0.5
    sm = sm_ref[...]
    for d, name in enumerate(("ga_f", "ga_b")):
        ga = sm[:, SM[name]:SM[name] + GLA_GATE_RANK].astype(BF16)
        xg = _dot(ga, w2_ref[d].astype(BF16)) + b2_ref[d]
        la_ref[d] = -_softplus(-xg) * (1.0 / GLA_GATE_NORM)
    for hh in range(hps):
        for c in range(n):
            vt_ref[hh, c] = v_ref[c * C:(c + 1) * C, hh * DV:(hh + 1) * DV].T.astype(BF16)
        for d in range(2):
            if has_state:
                st_ref[d, hh] = s0_refs[d][hh].T
            else:
                st_ref[d, hh] = jnp.zeros((DV, DK), F32)

    row = lax.broadcasted_iota(jnp.int32, (C, C), 0)
    col = lax.broadcasted_iota(jnp.int32, (C, C), 1)
    lower = row >= col
    upper = row <= col
    tri = (jnp.where(lower, 1.0, 0.0).astype(BF16), jnp.where(upper, 1.0, 0.0).astype(BF16))
    mid = C // 2

    def chunk(c, d, hh):
        r0 = pl.multiple_of(c * C, C)
        kcols = slice(hh * DK, (hh + 1) * DK)
        vcols = slice(hh * DV, (hh + 1) * DV)
        q = q_ref[pl.ds(r0, C), kcols] * scale
        k = k_ref[pl.ds(r0, C), kcols]
        v = v_ref[pl.ds(r0, C), vcols].astype(BF16)
        g = la_ref[d, pl.ds(r0, C), kcols]
        b = _tri_left(tri[d], g)
        tot = b[C - 1:C, :] if d == 0 else b[0:1, :]
        bm = b[mid:mid + 1, :]
        st = st_ref[d, hh]
        o = _dot_nt((q * jnp.exp(b)).astype(BF16), st.astype(BF16))
        qa = (q * jnp.exp(b - bm)).astype(BF16)
        ka = (k * jnp.exp(bm - b)).astype(BF16)
        att = _dot_nt(qa, ka)
        att = jnp.where(lower if d == 0 else upper, att, 0.0)
        o = o + _dot(att.astype(BF16), v)
        kd = (k * jnp.exp(tot - b)).astype(BF16)
        st_ref[d, hh] = st * jnp.exp(tot) + _dot(vt_ref[hh, c], kd)
        acc_ref[d, pl.ds(r0, C), vcols] = o

    def both(i, carry):
        for hh in range(hps):
            chunk(i, 0, hh)
            chunk(n - 1 - i, 1, hh)
        return carry

    lax.fori_loop(0, n, both, 0)

    for hh in range(hps):
        vcols = slice(hh * DV, (hh + 1) * DV)
        o = acc_ref[0, :, vcols] + acc_ref[1, :, vcols]
        ms = jnp.mean(o * o, axis=-1, keepdims=True)
        y = o * lax.rsqrt(ms + EPS) * nw_ref[...]
        o_ref[:, vcols] = (y * _silu(gg_ref[:, vcols])).astype(o_ref.dtype)
        if emit_state:
            for d in range(2):
                so_refs[d][hh] = st_ref[d, hh].T


def _gla(P, w2, b2, nw, *, layer, n_tok, n_batch, S, row0, states_in=None, states_out=None, out_prev=None):
    C = GLA_CHUNK
    sb = row0 // S
    hps = 2 if S <= SEQ_BLOCK else 1
    kw, vw = hps * GLA_DK, hps * GLA_DV
    col = {n: SEG[n][1] for n in ("q_l", "k_l", "v_l", "g_gla")}
    qj, kj, vj, gj = col["q_l"] // kw, col["k_l"] // kw, col["v_l"] // vw, col["g_gla"] // vw
    smj = SMALL_COL0 // SMALL_W
    in_specs = [
        pl.BlockSpec((S, kw), lambda b, h: (sb + b, qj + h)),
        pl.BlockSpec((S, kw), lambda b, h: (sb + b, kj + h)),
        pl.BlockSpec((S, vw), lambda b, h: (sb + b, vj + h)),
        pl.BlockSpec((S, SMALL_W), lambda b, h: (sb + b, smj)),
        pl.BlockSpec((S, vw), lambda b, h: (sb + b, gj + h)),
        pl.BlockSpec((None, 2, GLA_GATE_RANK, kw), lambda b, h: (layer, 0, 0, h)),
        pl.BlockSpec((None, 2, 1, kw), lambda b, h: (layer, 0, 0, h)),
        pl.BlockSpec((None, 1, GLA_DV), lambda b, h: (layer, 0, 0)),
    ]
    args = [P[SEG["q_l"][0]], P[SEG["k_l"][0]], P[SEG["v_l"][0]], P["s"], P[SEG["g_gla"][0]],
            w2, b2.reshape(DEPTH, 2, 1, GLA_KEY), nw.reshape(DEPTH, 1, GLA_DV)]
    has_state = states_in is not None
    st_block = (None, None, hps, GLA_DK, GLA_DV)
    if has_state:
        for s in states_in:
            in_specs.append(pl.BlockSpec(st_block, lambda b, h: (b, layer, h, 0, 0)))
            args.append(s)
    out_specs = [pl.BlockSpec((S, vw), lambda b, h: (sb + b, h))]
    out_shape = [jax.ShapeDtypeStruct((n_tok, GLA_WIDTH), BF16)]
    aliased = [out_prev]
    emit_state = states_out is not None
    if emit_state:
        for s in states_out:
            out_specs.append(pl.BlockSpec(st_block, lambda b, h: (b, layer, h, 0, 0)))
            out_shape.append(jax.ShapeDtypeStruct((n_batch, DEPTH, GLA_HEADS, GLA_DK, GLA_DV), F32))
            aliased.append(s)
    aliases = {}
    for oi, buf in enumerate(aliased):
        if buf is not None:
            aliases[len(args)] = oi
            in_specs.append(pl.BlockSpec(memory_space=pl.ANY))
            args.append(buf)
    n = S // C
    return pl.pallas_call(
        functools.partial(_gla_kernel, S=S, C=C, hps=hps, has_state=has_state, emit_state=emit_state,
                          n_alias=len(aliases)),
        grid=(n_batch, GLA_HEADS // hps),
        in_specs=in_specs,
        out_specs=out_specs,
        out_shape=out_shape,
        input_output_aliases=aliases,
        scratch_shapes=[pltpu.VMEM((hps, n, GLA_DV, C), BF16), pltpu.VMEM((2, S, kw), F32),
                        pltpu.VMEM((2, S, vw), F32), pltpu.VMEM((2, hps, GLA_DV, GLA_DK), F32)],
        compiler_params=_cparams(("arbitrary", "arbitrary")),
        name="gla_latent" if has_state else "gla_context",
    )(*args)


CONV_PAD = SUBLANES
CONV_ROWS = 256


def _conv_silu(dst_ref, src_ref, w_ref, b_ref, xp_ref, S):
    W = src_ref.shape[1]
    left = (SSM_CONV - 1) // 2
    xp_ref[0:CONV_PAD, 0:W] = jnp.zeros((CONV_PAD, W), F32)
    xp_ref[CONV_PAD + S:CONV_PAD + S + CONV_PAD, 0:W] = jnp.zeros((CONV_PAD, W), F32)
    xp_ref[CONV_PAD:CONV_PAD + S, 0:W] = src_ref[...]
    for r in range(0, S, CONV_ROWS):
        acc = b_ref[...] + xp_ref[CONV_PAD - left + r:CONV_PAD - left + r + CONV_ROWS, 0:W] * w_ref[0:1, :]
        for j in range(1, SSM_CONV):
            r0 = CONV_PAD - left + j + r
            acc = acc + xp_ref[r0:r0 + CONV_ROWS, 0:W] * w_ref[j:j + 1, :]
        dst_ref[r:r + CONV_ROWS, :] = _silu(acc)
def _ssd_kernel(*refs, S, C, has_state, emit_state, n_alias):
    refs = _drop_alias_refs(refs, 15 + (2 if has_state else 0) + n_alias, n_alias)
    it = iter(refs)
    xin_ref = next(it)
    bin_ref = next(it)
    cin_ref = next(it)
    cw_refs = (next(it), next(it), next(it))
    cb_refs = (next(it), next(it), next(it))
    z_ref = next(it)
    dtc_ref = next(it)
    dtr_ref = next(it)
    pc_ref = next(it)
    pr_ref = next(it)
    d_ref = next(it)
    s0_refs = (next(it), next(it)) if has_state else None
    y_ref = next(it)
    so_refs = (next(it), next(it)) if emit_state else None
    bt_ref = next(it)
    acc_ref = next(it)
    st_ref = next(it)
    xp_ref = next(it)
    xh_ref = next(it)
    x_ref = next(it)
    b_ref = next(it)
    c_ref = next(it)

    for dst, src, w, b in zip((x_ref, b_ref, c_ref), (xin_ref, bin_ref, cin_ref), cw_refs, cb_refs):
        _conv_silu(dst, src, w, b, xp_ref, S)

    n = S // C
    P = SSM_HEADDIM
    for r in range(0, S, CONV_ROWS):
        xr = x_ref[r:r + CONV_ROWS, :]
        for j in range(SSM_HPG):
            xh_ref[j, r:r + CONV_ROWS, :] = xr[:, j * P:(j + 1) * P].astype(BF16)
    for c in range(n):
        bt_ref[c] = b_ref[c * C:(c + 1) * C, :].T
    for d in range(2):
        if has_state:
            s0_t = jnp.concatenate([s0_refs[d][j] for j in range(SSM_HPG)], axis=0).T
            for j in range(SSM_HPG):
                st_ref[d, j] = s0_t[:, j * P:(j + 1) * P]
        else:
            st_ref[d] = jnp.zeros((SSM_HPG, SSM_STATE, P), F32)

    row = lax.broadcasted_iota(jnp.int32, (C, C), 0)
    col = lax.broadcasted_iota(jnp.int32, (C, C), 1)
    lower = row >= col
    upper = row <= col
    tri_l = jnp.where(lower, 1.0, 0.0).astype(BF16)
    tri_u = jnp.where(upper, 1.0, 0.0).astype(BF16)

    def chunk(c, d):
        r0 = pl.multiple_of(c * C, C)
        bm = b_ref[pl.ds(r0, C), :].astype(BF16)
        cm = c_ref[pl.ds(r0, C), :]
        bt = bt_ref[c]
        a_c = -jnp.exp(pc_ref[d, 1])
        a_r = -jnp.exp(pr_ref[d, 1])
        dt_c = _softplus(dtc_ref[d, pl.ds(r0, C), :] + pc_ref[d, 0])
        dt_r = _softplus(dtr_ref[d, c] + pr_ref[d, 0])
        if d == 0:
            cum_c = _tri_left(tri_l, dt_c * a_c)
            cum_r = _tri_right(dt_r * a_r, tri_u)
        else:
            cum_c = _tri_left(tri_u, dt_c * a_c)
            cum_r = _tri_right(dt_r * a_r, tri_l)
        mask = lower if d == 0 else upper
        last_r = cum_r[:, C - 1:C] if d == 0 else cum_r[:, 0:1]
        u_r = dt_r * jnp.exp(last_r - cum_r)
        e_last = jnp.exp(last_r)
        cb = _dot_nt(cm.astype(BF16), bm)
        cq2 = cum_c * LOG2_E
        cr2 = cum_r * LOG2_E - jnp.log2(dt_r)
        for j in range(SSM_HPG):
            cum_q = jnp.broadcast_to(cq2[:, j:j + 1], (C, C))
            dec = jnp.where(mask, jnp.exp2(cum_q - cr2[j:j + 1, :]), 0.0)
            lhs = jnp.concatenate([(cb * dec).astype(BF16),
                                   (cm * jnp.exp2(cum_q)).astype(BF16)], axis=1)
            x_j = xh_ref[j, pl.ds(r0, C), :]
            st_j = st_ref[d, j]
            acc_ref[d, j, pl.ds(r0, C), :] = _dot(lhs, jnp.concatenate([x_j, st_j.astype(BF16)], axis=0))
            st_ref[d, j] = st_j * e_last[j:j + 1, :] + _dot((bt * u_r[j:j + 1, :]).astype(BF16), x_j)

    def both(i, carry):
        chunk(i, 0)
        chunk(n - 1 - i, 1)
        return carry

    lax.fori_loop(0, n, both, 0)

    for r in range(0, S, CONV_ROWS):
        rows = slice(r, r + CONV_ROWS)
        y = jnp.concatenate([acc_ref[0, j, rows, :] + acc_ref[1, j, rows, :] for j in range(SSM_HPG)], axis=-1)
        y = y + x_ref[rows, :] * d_ref[...]
        y_ref[rows, :] = y * _silu(z_ref[rows, :])
    if emit_state:
        for d in range(2):
            s_t = jnp.concatenate([st_ref[d, j] for j in range(SSM_HPG)], axis=1).T
            for j in range(SSM_HPG):
                so_refs[d][j] = s_t[j * P:(j + 1) * P, :]


def _ssd(P, cw, cb, dt_col, dt_row, p_col, p_row, d_lane, *, layer, n_tok, n_batch, S, row0,
         states_in=None, states_out=None, out_prev=None):
    C = SSM_CHUNK
    sb = row0 // S
    z_reg, z_col = SEG["z"]
    x_reg, x_col = SEG["xbc"]
    zj = z_col // SSM_GW
    xj = x_col // SSM_GW
    bj = SSM_WIDTH // SSM_STATE
    cj = (SSM_WIDTH + SSM_BC) // SSM_STATE
    xbj = x_col // SSM_STATE

    def taps(width, j0):
        return pl.BlockSpec((None, SSM_CONV, width), lambda b, g: (layer, 0, j0 + g))

    def bias(width, j0):
        return pl.BlockSpec((None, 1, width), lambda b, g: (layer, 0, j0 + g))

    in_specs = [
        pl.BlockSpec((S, SSM_GW), lambda b, g: (sb + b, xj + g)),
        pl.BlockSpec((S, SSM_STATE), lambda b, g: (sb + b, xbj + bj + g)),
        pl.BlockSpec((S, SSM_STATE), lambda b, g: (sb + b, xbj + cj + g)),
        taps(SSM_GW, 0), taps(SSM_STATE, bj), taps(SSM_STATE, cj),
        bias(SSM_GW, 0), bias(SSM_STATE, bj), bias(SSM_STATE, cj),
        pl.BlockSpec((S, SSM_GW), lambda b, g: (sb + b, zj + g)),
        pl.BlockSpec((2, None, S, SSM_HPG), lambda b, g: (0, g, sb + b, 0)),
        pl.BlockSpec((2, None, S // C, SSM_HPG, C), lambda b, g: (0, g, sb + b, 0, 0)),
        pl.BlockSpec((None, 2, 2, None, 1, SSM_HPG), lambda b, g: (layer, 0, 0, g, 0, 0)),
        pl.BlockSpec((None, 2, 2, None, SSM_HPG, 1), lambda b, g: (layer, 0, 0, g, 0, 0)),
        pl.BlockSpec((None, 1, SSM_GW), lambda b, g: (layer, 0, g)),
    ]
    cb3 = cb.reshape(DEPTH, 1, SSM_CONV_CH)
    args = [P[x_reg], P[x_reg], P[x_reg], cw, cw, cw, cb3, cb3, cb3, P[z_reg], dt_col, dt_row, p_col, p_row,
            d_lane]
    has_state = states_in is not None
    st_block = (None, None, SSM_HPG, SSM_HEADDIM, SSM_STATE)
    if has_state:
        for s in states_in:
            in_specs.append(pl.BlockSpec(st_block, lambda b, g: (b, layer, g, 0, 0)))
            args.append(s)
    out_specs = [pl.BlockSpec((S, SSM_GW), lambda b, g: (sb + b, g))]
    out_shape = [jax.ShapeDtypeStruct((n_tok, SSM_WIDTH), F32)]
    aliased = [out_prev]
    emit_state = states_out is not None
    if emit_state:
        for s in states_out:
            out_specs.append(pl.BlockSpec(st_block, lambda b, g: (b, layer, g, 0, 0)))
            out_shape.append(jax.ShapeDtypeStruct((n_batch, DEPTH, SSM_HEADS, SSM_HEADDIM, SSM_STATE), F32))
            aliased.append(s)
    aliases = {}
    for oi, buf in enumerate(aliased):
        if buf is not None:
            aliases[len(args)] = oi
            in_specs.append(pl.BlockSpec(memory_space=pl.ANY))
            args.append(buf)
    n = S // C
    return pl.pallas_call(
        functools.partial(_ssd_kernel, S=S, C=C, has_state=has_state, emit_state=emit_state,
                          n_alias=len(aliases)),
        grid=(n_batch, SSM_GROUPS),
        in_specs=in_specs,
        out_specs=out_specs,
        out_shape=out_shape,
        input_output_aliases=aliases,
        scratch_shapes=[pltpu.VMEM((n, SSM_STATE, C), F32), pltpu.VMEM((2, SSM_HPG, S, SSM_HEADDIM), F32),
                        pltpu.VMEM((2, SSM_HPG, SSM_STATE, SSM_HEADDIM), F32),
                        pltpu.VMEM((S + 2 * CONV_PAD, SSM_GW), F32),
                        pltpu.VMEM((SSM_HPG, S, SSM_HEADDIM), BF16), pltpu.VMEM((S, SSM_GW), F32),
                        pltpu.VMEM((S, SSM_STATE), F32), pltpu.VMEM((S, SSM_STATE), F32)],
        compiler_params=_cparams(("arbitrary", "arbitrary")),
        name="ssd_latent" if has_state else "ssd_context",
    )(*args)


def _gather_small_w_in(w_in_t):
    names = REGIONS["s"]
    rows = [w_in_t[:, _IN_OFF[n]:_IN_OFF[n] + _IN_SIZE[n], :] for n in names]
    used = sum(_IN_SIZE[n] for n in names)
    rows.append(jnp.zeros((DEPTH, REGION_W["s"] - used, D_MODEL), w_in_t.dtype))
    return jnp.concatenate(rows, axis=1)


def kernel(x_prompt, x_sample, cache_mla_ckv, cache_mla_krope, state_gla_fwd, state_gla_bwd,
           state_ssm_fwd, state_ssm_bwd, c, c_ctx, w_mod, b_mod, norm_w, w_in, q_a_norm, w_q_b,
           kv_a_norm, w_kv_b, gla_w_gate2, gla_b_gate, gla_norm, ssm_conv_w, ssm_conv_b,
           ssm_dt_bias, ssm_a_log, ssm_d, ssm_norm, w_br_mla, w_br_gla, w_br_ssm, w_out, final_norm):
    nb_p, s_p, _ = x_prompt.shape
    nb_l, s_l, _ = x_sample.shape
    n_p = nb_p * s_p
    n_l = nb_l * s_l
    n_tok = n_p + n_l
    assert s_p == SEQ_BLOCK and s_l % SEQ_BLOCK == 0 and n_p % s_l == 0
    t_c = cache_mla_ckv.shape[2]
    assert t_c == SEQ_BLOCK

    x = jnp.concatenate([x_prompt.reshape(n_p, D_MODEL), x_sample.reshape(n_l, D_MODEL)], axis=0)

    w_in_t = jnp.swapaxes(w_in, 1, 2)
    w_in_s = _gather_small_w_in(w_in_t)
    wq = w_q_b.reshape(DEPTH, Q_LORA, MLA_HEADS, QK_NOPE + QK_ROPE)
    wq_nope = wq[..., :QK_NOPE].reshape(DEPTH, Q_LORA, MLA_HEADS * QK_NOPE)
    wq_rope = wq[..., QK_NOPE:].reshape(DEPTH, Q_LORA, MLA_HEADS * QK_ROPE)
    tables = _rope_tables(s_l)
    p_ssm = jnp.stack([ssm_dt_bias, ssm_a_log], axis=2).reshape(DEPTH, 2, 2, SSM_GROUPS, SSM_HPG)
    p_col = p_ssm[:, :, :, :, None, :]
    p_row = p_ssm[:, :, :, :, :, None]
    d_lane = jnp.repeat(ssm_d, SSM_HEADDIM, axis=-1).reshape(DEPTH, 1, SSM_WIDTH)
    blk_mod = np.concatenate([np.zeros(n_p // SEQ_BLOCK, np.int32),
                              1 + np.repeat(np.arange(nb_l, dtype=np.int32), s_l // SEQ_BLOCK)])
    cond = jnp.concatenate([c_ctx[None, :], c], axis=0)
    cond = _silu(cond)
    cond = jnp.concatenate([cond, jnp.zeros((8 - cond.shape[0] % 8, D_MODEL), F32)], axis=0)

    gla_f = gla_b = ssm_f = ssm_b = None
    ckv_l, kr_l = [], []
    for l in range(DEPTH):
        mod = _matmul(cond, w_mod, name="mm_mod", layer=l, n_out=3 * D_MODEL, tm=cond.shape[0], tn=1536,
                      out_dtype=F32)
        mod = mod[:1 + nb_l] + b_mod[l][None, :]
        mod_rows = mod[blk_mod][:, None, :]
        gate_rows = mod_rows[:, :, 2 * D_MODEL:]

        h = _prenorm(x, norm_w[l], mod_rows)
        P = {}
        for r in ("a", "b", "c"):
            first = REGIONS[r][0]
            P[r] = _matmul(h, w_in_t, name="mm_in_" + r, layer=l, w_t=True, w_col0=_IN_OFF[first],
                           n_out=REGION_W[r], tm=1536, tn=IN_TN, out_dtype=F32)
        P["s"] = _matmul(h, w_in_s, name="mm_in_s", layer=l, w_t=True, n_out=REGION_W["s"], tm=1536, tn=IN_TN,
                         out_dtype=F32)
        small = P["s"][:, SMALL_COL0:SMALL_COL0 + SMALL_W]

        qa_col, kva_col = SEG["q_a"][1], SEG["kv_a"][1]
        qn = _matmul(P["s"], wq_nope, name="mm_q_nope", x_col0=qa_col, k=Q_LORA, layer=l,
                     n_out=MLA_HEADS * QK_NOPE, tm=1024, tn=1024, out_dtype=BF16, gain=q_a_norm[l])
        qr = _matmul(P["s"], wq_rope, name="mm_q_rope", x_col0=qa_col, k=Q_LORA, layer=l,
                     n_out=MLA_HEADS * QK_ROPE, tm=1024, tn=1024, out_dtype=F32, gain=q_a_norm[l])
        ckv = _rmsnorm(P["s"], kv_a_norm[l], x_col0=kva_col, width=KV_LORA)
        ckv_all = jnp.concatenate([ckv, cache_mla_ckv[:, l].reshape(nb_l * t_c, KV_LORA)], axis=0)
        kv = _matmul(ckv_all, w_kv_b, name="mm_kv", layer=l, n_out=MLA_HEADS * (QK_NOPE + V_HEAD),
                     tm=ckv_all.shape[0] // 4, tn=1024, out_dtype=BF16)
        o_mla = _mla(qn, qr, P, kv, n_tok=n_tok, n_batch=nb_p, S=s_p, row0=0, latent=False)
        o_mla = _mla(qn, qr, P, kv, n_tok=n_tok, n_batch=nb_l, S=s_l, row0=n_p, latent=True,
                     kv_ctx_row0=n_tok, krope_ctx=cache_mla_krope[:, l], tables=tables, out_prev=o_mla)
        ckv_l.append(ckv[:n_p].reshape(nb_p, s_p, KV_LORA))
        kr_l.append(small[:n_p, SM["k_r"]:SM["k_r"] + QK_ROPE].reshape(nb_p, s_p, QK_ROPE))

        o_gla, gla_f, gla_b = _gla(P, gla_w_gate2, gla_b_gate, gla_norm, layer=l, n_tok=n_tok, n_batch=nb_p,
                                   S=s_p, row0=0, states_out=(gla_f, gla_b))
        (o_gla,) = _gla(P, gla_w_gate2, gla_b_gate, gla_norm, layer=l, n_tok=n_tok, n_batch=nb_l, S=s_l,
                        row0=n_p, states_in=(state_gla_fwd, state_gla_bwd), out_prev=o_gla)

        dt = small[:, SM["dt_f"]:SM["dt_f"] + 2 * SSM_HEADS].reshape(n_tok, 2, SSM_GROUPS, SSM_HPG)
        dt_col = jnp.transpose(dt, (1, 2, 0, 3))
        dt_row = jnp.transpose(dt.reshape(n_tok // SSM_CHUNK, SSM_CHUNK, 2, SSM_GROUPS, SSM_HPG),
                               (2, 3, 0, 4, 1))
        yz, ssm_f, ssm_b = _ssd(P, ssm_conv_w, ssm_conv_b, dt_col, dt_row, p_col, p_row, d_lane, layer=l,
                                n_tok=n_tok, n_batch=nb_p, S=s_p, row0=0, states_out=(ssm_f, ssm_b))
        (yz,) = _ssd(P, ssm_conv_w, ssm_conv_b, dt_col, dt_row, p_col, p_row, d_lane, layer=l, n_tok=n_tok,
                     n_batch=nb_l, S=s_l, row0=n_p, states_in=(state_ssm_fwd, state_ssm_bwd), out_prev=yz)

        mg = _matmul(o_mla, w_br_mla, name="mm_br_mla", layer=l, n_out=D_MODEL, tm=768, tn=1024, out_dtype=F32,
                     ep="sig", m=P["c"], m_col0=SEG["m_mla"][1])
        mg = _matmul(o_gla, w_br_gla, name="mm_br_gla", layer=l, n_out=D_MODEL, tm=768, tn=1024, out_dtype=F32,
                     ep="sigadd", m=P["c"], m_col0=SEG["m_gla"][1], prev=mg)
        mg = _matmul(yz, w_br_ssm, name="mm_br_ssm", layer=l, n_out=D_MODEL, tm=256, tn=1024, out_dtype=BF16,
                     gain=ssm_norm[l], ep="sigadd", m=P["c"], m_col0=SEG["m_ssm"][1], prev=mg,
                     w_single_buffer=True)
        x = _matmul(mg, w_out, name="mm_out", layer=l, n_out=D_MODEL, tm=768, tn=1024, out_dtype=F32,
                    ep="resid", res=x, gate_rows=gate_rows)

    y = _rmsnorm(x, final_norm)
    y_prompt = y[:n_p].reshape(nb_p, s_p, D_MODEL)
    y_sample = y[n_p:].reshape(nb_l, s_l, D_MODEL)
    return (y_prompt, y_sample, jnp.stack(ckv_l, axis=1), jnp.stack(kr_l, axis=1),
            gla_f, gla_b, ssm_f, ssm_b)
```

```python
import functools

import jax
import jax.numpy as jnp
import numpy as np
from jax import lax
from jax.experimental import pallas as pl
from jax.experimental.pallas import tpu as pltpu

F32 = jnp.float32
BF16 = jnp.bfloat16

D_MODEL = 2048
DEPTH = 4
GRID_W = 64
EPS = 1e-6
LOG2_E = 1.4426950408889634
MLA_HEADS = 16
QK_NOPE = 128
QK_ROPE = 64
V_HEAD = 128
Q_LORA = 512
KV_LORA = 256
MLA_WIDTH = MLA_HEADS * V_HEAD
ROPE_BASE = 10000.0
GLA_HEADS = 4
GLA_DK = 256
GLA_DV = 512
GLA_KEY = GLA_HEADS * GLA_DK
GLA_WIDTH = GLA_HEADS * GLA_DV
GLA_GATE_RANK = 16
GLA_GATE_NORM = 16.0
SSM_HEADS = 64
SSM_HEADDIM = 64
SSM_WIDTH = SSM_HEADS * SSM_HEADDIM
SSM_GROUPS = 8
SSM_HPG = SSM_HEADS // SSM_GROUPS
SSM_GW = SSM_HPG * SSM_HEADDIM
SSM_STATE = 128
SSM_CONV = 5
SSM_BC = SSM_GROUPS * SSM_STATE
SSM_CONV_CH = SSM_WIDTH + 2 * SSM_BC

IN_SIZES = (Q_LORA, KV_LORA, QK_ROPE, MLA_WIDTH,
            GLA_KEY, GLA_KEY, GLA_WIDTH, GLA_GATE_RANK, GLA_GATE_RANK, GLA_WIDTH,
            SSM_WIDTH, SSM_CONV_CH, SSM_HEADS, SSM_HEADS,
            D_MODEL, D_MODEL, D_MODEL)
IN_NAMES = ("q_a", "kv_a", "k_r", "g_mla", "q_l", "k_l", "v_l", "ga_f", "ga_b", "g_gla",
            "z", "xbc", "dt_f", "dt_b", "m_mla", "m_gla", "m_ssm")
_IN_OFF = dict(zip(IN_NAMES, np.cumsum((0,) + IN_SIZES[:-1]).tolist()))
_IN_SIZE = dict(zip(IN_NAMES, IN_SIZES))

REGIONS = {
    "a": ("g_mla", "q_l", "k_l", "v_l"),
    "b": ("g_gla", "z", "xbc"),
    "c": ("m_mla", "m_gla", "m_ssm"),
    "s": ("q_a", "kv_a", "k_r", "ga_f", "ga_b", "dt_f", "dt_b"),
}
IN_TN = 1024
SEG = {}
REGION_W = {}
for _r, _names in REGIONS.items():
    _o = 0
    for _n in _names:
        SEG[_n] = (_r, _o)
        _o += _IN_SIZE[_n]
    REGION_W[_r] = -(-_o // IN_TN) * IN_TN
    if _r != "s":
        assert _o % IN_TN == 0
        assert all(_IN_OFF[_names[i]] + _IN_SIZE[_names[i]] == _IN_OFF[_names[i + 1]]
                   for i in range(len(_names) - 1))
SMALL_W = 256
SMALL_COL0 = SEG["k_r"][1]
assert SMALL_COL0 % SMALL_W == 0 and SEG["dt_b"][1] + _IN_SIZE["dt_b"] <= SMALL_COL0 + SMALL_W
SM = {n: SEG[n][1] - SMALL_COL0 for n in ("k_r", "ga_f", "ga_b", "dt_f", "dt_b")}

SUBLANES = 8
SEQ_BLOCK = 256
GLA_CHUNK = 128
SSM_CHUNK = 128
V7X_VMEM_LIMIT = 56 * 1024 * 1024


def _cparams(sem):
    return pltpu.CompilerParams(dimension_semantics=sem, vmem_limit_bytes=V7X_VMEM_LIMIT)


def _silu(x):
    return x * (1.0 / (1.0 + jnp.exp(-x)))


def _sigmoid(x):
    return 1.0 / (1.0 + jnp.exp(-x))


def _softplus(x):
    return jnp.maximum(x, 0.0) + jnp.log(1.0 + jnp.exp(-jnp.abs(x)))


def _dot(a, b):
    return jnp.dot(a, b, preferred_element_type=F32)


def _dot_nt(a, b):
    return lax.dot_general(a, b, (((1,), (1,)), ((), ())), preferred_element_type=F32)


def _split3(x):
    hi = x.astype(BF16)
    r1 = x - hi.astype(F32)
    mid = r1.astype(BF16)
    lo = (r1 - mid.astype(F32)).astype(BF16)
    return hi, mid, lo


def _tri_left(tri, x):
    hi, mid, lo = _split3(x)
    return _dot(tri, hi) + _dot(tri, mid) + _dot(tri, lo)


def _tri_right(x, tri):
    hi, mid, lo = _split3(x)
    return _dot(hi, tri) + _dot(mid, tri) + _dot(lo, tri)


def _drop_alias_refs(refs, n_in, n_alias):
    refs = list(refs)
    return refs[:n_in - n_alias] + refs[n_in:]


def _mm_kernel(*refs, rms, ep, cast_w, w_t):
    it = iter(refs)
    x_ref = next(it)
    w_ref = next(it)
    gain_ref = next(it) if rms else None
    m_ref = next(it) if ep in ("sig", "sigadd") else None
    prev_ref = next(it) if ep == "sigadd" else None
    res_ref = next(it) if ep == "resid" else None
    gate_ref = next(it) if ep == "resid" else None
    o_ref = next(it)
    wbf_ref = next(it) if cast_w else None

    if cast_w:
        @pl.when(pl.program_id(1) == 0)
        def _():
            wbf_ref[...] = (w_ref[0] if w_t else w_ref[...]).astype(BF16)
        w = wbf_ref[...]
    else:
        w = w_ref[...]
    x = x_ref[...]
    if rms:
        xf = x.astype(F32)
        ms = jnp.mean(xf * xf, axis=-1, keepdims=True)
        x = xf * lax.rsqrt(ms + EPS) * gain_ref[...]
    acc = _dot_nt(x.astype(BF16), w) if w_t else _dot(x.astype(BF16), w)
    if ep == "sig":
        acc = _sigmoid(m_ref[...]) * acc
    elif ep == "sigadd":
        acc = prev_ref[...].astype(F32) + _sigmoid(m_ref[...]) * acc
    elif ep == "resid":
        for r in range(gate_ref.shape[0]):
            rows = slice(r * SEQ_BLOCK, (r + 1) * SEQ_BLOCK)
            o_ref[rows, :] = res_ref[rows, :] + gate_ref[r] * acc[rows, :]
        return
    o_ref[...] = acc.astype(o_ref.dtype)


def _matmul(x, w, *, name, layer, n_out, tm, tn, out_dtype, w_col0=0, w_t=False, gain=None, x_col0=0, k=None,
            ep=None, m=None, m_col0=0, prev=None, res=None, gate_rows=None, w_single_buffer=False):
    M = x.shape[0]
    K = x.shape[1] if k is None else k
    assert M % tm == 0 and n_out % tn == 0 and m_col0 % tn == 0 and x_col0 % K == 0
    xj = x_col0 // K
    cast_w = w.dtype != BF16
    rms = gain is not None
    mj = m_col0 // tn
    mode = dict(pipeline_mode=pl.Buffered(1)) if w_single_buffer else {}
    if w_t:
        assert w_col0 % SUBLANES == 0
        w_spec = pl.BlockSpec((pl.Element(1), pl.Element(tn), pl.Element(K)),
                              lambda j, i: (layer, pl.multiple_of(w_col0 + j * tn, SUBLANES), 0), **mode)
    else:
        assert w_col0 % tn == 0
        wj = w_col0 // tn
        w_spec = pl.BlockSpec((None, K, tn), lambda j, i: (layer, 0, j + wj), **mode)

    in_specs = [pl.BlockSpec((tm, K), lambda j, i: (i, xj)), w_spec]
    args = [x, w]
    if rms:
        in_specs.append(pl.BlockSpec((1, K), lambda j, i: (0, 0)))
        args.append(gain.reshape(1, K))
    if ep in ("sig", "sigadd"):
        in_specs.append(pl.BlockSpec((tm, tn), lambda j, i: (i, j + mj)))
        args.append(m)
    if ep == "sigadd":
        in_specs.append(pl.BlockSpec((tm, tn), lambda j, i: (i, j)))
        args.append(prev)
    if ep == "resid":
        assert tm % SEQ_BLOCK == 0 and out_dtype == res.dtype
        in_specs.append(pl.BlockSpec((tm, tn), lambda j, i: (i, j)))
        args.append(res)
        in_specs.append(pl.BlockSpec((tm // SEQ_BLOCK, 1, tn), lambda j, i: (i, 0, j)))
        args.append(gate_rows)
    scratch = [pltpu.VMEM((tn, K) if w_t else (K, tn), BF16)] if cast_w else []
    return pl.pallas_call(
        functools.partial(_mm_kernel, rms=rms, ep=ep, cast_w=cast_w, w_t=w_t),
        grid=(n_out // tn, M // tm),
        in_specs=in_specs,
        out_specs=pl.BlockSpec((tm, tn), lambda j, i: (i, j)),
        out_shape=jax.ShapeDtypeStruct((M, n_out), out_dtype),
        scratch_shapes=scratch,
        compiler_params=_cparams(("arbitrary", "arbitrary")),
        name=name,
    )(*args)


def _prenorm_kernel(x_ref, nw_ref, mod_ref, o_ref):
    x = x_ref[...]
    ms = jnp.mean(x * x, axis=-1, keepdims=True)
    y = x * lax.rsqrt(ms + EPS) * nw_ref[...]
    shift = mod_ref[:, 0:D_MODEL]
    scale = mod_ref[:, D_MODEL:2 * D_MODEL]
    o_ref[...] = (y * (1.0 + scale) + shift).astype(o_ref.dtype)


def _prenorm(x, norm_w, mod_rows):
    M = x.shape[0]
    return pl.pallas_call(
        _prenorm_kernel,
        grid=(M // SEQ_BLOCK,),
        in_specs=[pl.BlockSpec((SEQ_BLOCK, D_MODEL), lambda i: (i, 0)),
                  pl.BlockSpec((1, D_MODEL), lambda i: (0, 0)),
                  pl.BlockSpec((None, 1, 3 * D_MODEL), lambda i: (i, 0, 0))],
        out_specs=pl.BlockSpec((SEQ_BLOCK, D_MODEL), lambda i: (i, 0)),
        out_shape=jax.ShapeDtypeStruct((M, D_MODEL), BF16),
        compiler_params=_cparams(("arbitrary",)),
        name="prenorm",
    )(x, norm_w.reshape(1, D_MODEL), mod_rows)


def _rmsnorm_kernel(x_ref, w_ref, o_ref):
    x = x_ref[...]
    ms = jnp.mean(x * x, axis=-1, keepdims=True)
    o_ref[...] = (x * lax.rsqrt(ms + EPS) * w_ref[...]).astype(o_ref.dtype)


def _rmsnorm(x, w, *, x_col0=0, width=None, tm=512, out_dtype=F32):
    M = x.shape[0]
    width = x.shape[1] if width is None else width
    assert x_col0 % width == 0 and M % tm == 0
    cj = x_col0 // width
    return pl.pallas_call(
        _rmsnorm_kernel,
        grid=(M // tm,),
        in_specs=[pl.BlockSpec((tm, width), lambda i: (i, cj)),
                  pl.BlockSpec((1, width), lambda i: (0, 0))],
        out_specs=pl.BlockSpec((tm, width), lambda i: (i, 0)),
        out_shape=jax.ShapeDtypeStruct((M, width), out_dtype),
        compiler_params=_cparams(("arbitrary",)),
        name="rmsnorm",
    )(x, w.reshape(1, width))


def _swap_pairs(x):
    n = x.shape[-1]
    nxt = pltpu.roll(x, n - 1, axis=1)
    prv = pltpu.roll(x, 1, axis=1)
    lane = lax.broadcasted_iota(jnp.int32, x.shape, 1)
    return jnp.where((lane & 1) == 0, nxt, prv)


def _mla_kernel(*refs, latent, n_alias):
    refs = _drop_alias_refs(refs, (11 if latent else 5) + n_alias, n_alias)
    it = iter(refs)
    qn_ref = next(it)
    qr_ref = next(it)
    g_ref = next(it)
    kv_ref = next(it)
    sm_ref = next(it)
    if latent:
        kvc_ref = next(it)
        krc_ref = next(it)
        cq_ref = next(it)
        sq_ref = next(it)
        ck_ref = next(it)
        sk_ref = next(it)
    o_ref = next(it)

    scale = float(QK_NOPE + QK_ROPE) ** -0.5
    qr = qr_ref[...]
    sm = sm_ref[...]
    if latent:
        qr = qr * cq_ref[...] + _swap_pairs(qr) * sq_ref[...]
        sm = sm * ck_ref[...] + _swap_pairs(sm) * sk_ref[...]
    kr = sm[:, SM["k_r"]:SM["k_r"] + QK_ROPE].astype(BF16)
    if latent:
        krc = krc_ref[...].astype(BF16)

    for h in range(MLA_HEADS):
        q_h = jnp.concatenate([qn_ref[:, h * QK_NOPE:(h + 1) * QK_NOPE],
                               qr[:, h * QK_ROPE:(h + 1) * QK_ROPE].astype(BF16)], axis=1)
        c0 = h * (QK_NOPE + V_HEAD)
        v_h = kv_ref[:, c0 + QK_NOPE:c0 + QK_NOPE + V_HEAD]
        s_own = _dot_nt(q_h, jnp.concatenate([kv_ref[:, c0:c0 + QK_NOPE], kr], axis=1)) * scale
        mx = jnp.max(s_own, axis=-1, keepdims=True)
        if latent:
            vc_h = kvc_ref[:, c0 + QK_NOPE:c0 + QK_NOPE + V_HEAD]
            s_ctx = _dot_nt(q_h, jnp.concatenate([kvc_ref[:, c0:c0 + QK_NOPE], krc], axis=1)) * scale
            mx = jnp.maximum(mx, jnp.max(s_ctx, axis=-1, keepdims=True))
            p_ctx = jnp.exp(s_ctx - mx)
        p_own = jnp.exp(s_own - mx)
        den = jnp.sum(p_own, axis=-1, keepdims=True)
        if latent:
            den = den + jnp.sum(p_ctx, axis=-1, keepdims=True)
        o_h = _dot(p_own.astype(BF16), v_h)
        if latent:
            o_h = o_h + _dot(p_ctx.astype(BF16), vc_h)
        g_h = g_ref[:, h * V_HEAD:(h + 1) * V_HEAD]
        o_ref[:, h * V_HEAD:(h + 1) * V_HEAD] = (o_h * (1.0 / den) * _silu(g_h)).astype(o_ref.dtype)


def _mla(qn, qr, P, kv, *, n_tok, n_batch, S, row0, latent, kv_ctx_row0=0, krope_ctx=None, tables=None,
         out_prev=None):
    tq = min(S, 256)
    nq = S // tq
    rb = row0 // tq
    sb = row0 // S
    HW = MLA_HEADS * (QK_NOPE + V_HEAD)
    g_reg, g_col = SEG["g_mla"]
    gj = g_col // MLA_WIDTH
    smj = SMALL_COL0 // SMALL_W
    in_specs = [
        pl.BlockSpec((tq, MLA_HEADS * QK_NOPE), lambda b, i: (rb + b * nq + i, 0)),
        pl.BlockSpec((tq, MLA_HEADS * QK_ROPE), lambda b, i: (rb + b * nq + i, 0)),
        pl.BlockSpec((tq, MLA_WIDTH), lambda b, i: (rb + b * nq + i, gj)),
        pl.BlockSpec((S, HW), lambda b, i: (sb + b, 0)),
        pl.BlockSpec((S, SMALL_W), lambda b, i: (sb + b, smj)),
    ]
    args = [qn, qr, P[g_reg], kv, P["s"]]
    if latent:
        Tc = krope_ctx.shape[1]
        cb = kv_ctx_row0 // Tc
        cos_q, sin_q, cos_k, sin_k = tables
        in_specs += [
            pl.BlockSpec((Tc, HW), lambda b, i: (cb + b, 0)),
            pl.BlockSpec((None, Tc, QK_ROPE), lambda b, i: (b, 0, 0)),
            pl.BlockSpec((tq, MLA_HEADS * QK_ROPE), lambda b, i: (i, 0)),
            pl.BlockSpec((tq, MLA_HEADS * QK_ROPE), lambda b, i: (i, 0)),
            pl.BlockSpec((S, SMALL_W), lambda b, i: (0, 0)),
            pl.BlockSpec((S, SMALL_W), lambda b, i: (0, 0)),
        ]
        args += [kv, krope_ctx, cos_q, sin_q, cos_k, sin_k]
    aliases = {}
    if out_prev is not None:
        aliases[len(args)] = 0
        in_specs.append(pl.BlockSpec(memory_space=pl.ANY))
        args.append(out_prev)
    return pl.pallas_call(
        functools.partial(_mla_kernel, latent=latent, n_alias=len(aliases)),
        grid=(n_batch, nq),
        in_specs=in_specs,
        out_specs=pl.BlockSpec((tq, MLA_WIDTH), lambda b, i: (rb + b * nq + i, 0)),
        out_shape=jax.ShapeDtypeStruct((n_tok, MLA_WIDTH), BF16),
        input_output_aliases=aliases,
        compiler_params=_cparams(("arbitrary", "arbitrary")),
        name="mla_latent" if latent else "mla_context",
    )(*args)


def _rope_tables(S):
    rows = jnp.repeat(jnp.arange(S // GRID_W, dtype=F32), GRID_W)
    cols = jnp.tile(jnp.arange(GRID_W, dtype=F32), S // GRID_W)
    n_freq = QK_ROPE // 4
    inv = ROPE_BASE ** (-jnp.arange(n_freq, dtype=F32) / n_freq)
    ang = jnp.concatenate([rows[:, None] * inv, cols[:, None] * inv], axis=-1)
    cos = jnp.repeat(jnp.cos(ang), 2, axis=-1)
    sin = jnp.repeat(jnp.sin(ang), 2, axis=-1)
    sign = jnp.tile(jnp.array([-1.0, 1.0], F32), QK_ROPE // 2)
    sin = sin * sign
    cos_q = jnp.tile(cos, (1, MLA_HEADS))
    sin_q = jnp.tile(sin, (1, MLA_HEADS))
    lo, hi = SM["k_r"], SMALL_W - SM["k_r"] - QK_ROPE
    cos_k = jnp.concatenate([jnp.ones((S, lo), F32), cos, jnp.ones((S, hi), F32)], axis=-1)
    sin_k = jnp.concatenate([jnp.zeros((S, lo), F32), sin, jnp.zeros((S, hi), F32)], axis=-1)
    return cos_q, sin_q, cos_k, sin_k


def _gla_kernel(*refs, S, C, hps, has_state, emit_state, n_alias):
    refs = _drop_alias_refs(refs, 8 + (2 if has_state else 0) + n_alias, n_alias)
    it = iter(refs)
    q_ref = next(it)
    k_ref = next(it)
    v_ref = next(it)
    sm_ref = next(it)
    gg_ref = next(it)
    w2_ref = next(it)
    b2_ref = next(it)
    nw_ref = next(it)
    s0_refs = (next(it), next(it)) if has_state else None
    o_ref = next(it)
    so_refs = (next(it), next(it)) if emit_state else None
    vt_ref = next(it)
    la_ref = next(it)
    acc_ref = next(it)
    st_ref = next(it)

    n = S // C
    DK, DV = GLA_DK, GLA_DV
    scale = float(GLA_DK) ** -0.5
    sm = sm_ref[...]
    for d, name in enumerate(("ga_f", "ga_b")):
        ga = sm[:, SM[name]:SM[name] + GLA_GATE_RANK].astype(BF16)
        xg = _dot(ga, w2_ref[d].astype(BF16)) + b2_ref[d]
        la_ref[d] = -_softplus(-xg) * (1.0 / GLA_GATE_NORM)
    for hh in range(hps):
        for c in range(n):
            vt_ref[hh, c] = v_ref[c * C:(c + 1) * C, hh * DV:(hh + 1) * DV].T.astype(BF16)
        for d in range(2):
            if has_state:
                st_ref[d, hh] = s0_refs[d][hh].T
            else:
                st_ref[d, hh] = jnp.zeros((DV, DK), F32)

    row = lax.broadcasted_iota(jnp.int32, (C, C), 0)
    col = lax.broadcasted_iota(jnp.int32, (C, C), 1)
    lower = row >= col
    upper = row <= col
    tri = (jnp.where(lower, 1.0, 0.0).astype(BF16), jnp.where(upper, 1.0, 0.0).astype(BF16))
    mid = C // 2

    def chunk(c, d, hh):
        r0 = pl.multiple_of(c * C, C)
        kcols = slice(hh * DK, (hh + 1) * DK)
        vcols = slice(hh * DV, (hh + 1) * DV)
        q = q_ref[pl.ds(r0, C), kcols] * scale
        k = k_ref[pl.ds(r0, C), kcols]
        v = v_ref[pl.ds(r0, C), vcols].astype(BF16)
        g = la_ref[d, pl.ds(r0, C), kcols]
        b = _tri_left(tri[d], g)
        tot = b[C - 1:C, :] if d == 0 else b[0:1, :]
        bm = b[mid:mid + 1, :]
        st = st_ref[d, hh]
        o = _dot_nt((q * jnp.exp(b)).astype(BF16), st.astype(BF16))
        qa = (q * jnp.exp(b - bm)).astype(BF16)
        ka = (k * jnp.exp(bm - b)).astype(BF16)
        att = _dot_nt(qa, ka)
        att = jnp.where(lower if d == 0 else upper, att, 0.0)
        o = o + _dot(att.astype(BF16), v)
        kd = (k * jnp.exp(tot - b)).astype(BF16)
        st_ref[d, hh] = st * jnp.exp(tot) + _dot(vt_ref[hh, c], kd)
        acc_ref[d, pl.ds(r0, C), vcols] = o

    def both(i, carry):
        for hh in range(hps):
            chunk(i, 0, hh)
            chunk(n - 1 - i, 1, hh)
        return carry

    lax.fori_loop(0, n, both, 0)

    for hh in range(hps):
        vcols = slice(hh * DV, (hh + 1) * DV)
        o = acc_ref[0, :, vcols] + acc_ref[1, :, vcols]
        ms = jnp.mean(o * o, axis=-1, keepdims=True)
        y = o * lax.rsqrt(ms + EPS) * nw_ref[...]
        o_ref[:, vcols] = (y * _silu(gg_ref[:, vcols])).astype(o_ref.dtype)
        if emit_state:
            for d in range(2):
                so_refs[d][hh] = st_ref[d, hh].T


def _gla(P, w2, b2, nw, *, layer, n_tok, n_batch, S, row0, states_in=None, states_out=None, out_prev=None):
    C = GLA_CHUNK
    sb = row0 // S
    hps = 2 if S <= SEQ_BLOCK else 1
    kw, vw = hps * GLA_DK, hps * GLA_DV
    col = {n: SEG[n][1] for n in ("q_l", "k_l", "v_l", "g_gla")}
    qj, kj, vj, gj = col["q_l"] // kw, col["k_l"] // kw, col["v_l"] // vw, col["g_gla"] // vw
    smj = SMALL_COL0 // SMALL_W
    in_specs = [
        pl.BlockSpec((S, kw), lambda b, h: (sb + b, qj + h)),
        pl.BlockSpec((S, kw), lambda b, h: (sb + b, kj + h)),
        pl.BlockSpec((S, vw), lambda b, h: (sb + b, vj + h)),
        pl.BlockSpec((S, SMALL_W), lambda b, h: (sb + b, smj)),
        pl.BlockSpec((S, vw), lambda b, h: (sb + b, gj + h)),
        pl.BlockSpec((None, 2, GLA_GATE_RANK, kw), lambda b, h: (layer, 0, 0, h)),
        pl.BlockSpec((None, 2, 1, kw), lambda b, h: (layer, 0, 0, h)),
        pl.BlockSpec((None, 1, GLA_DV), lambda b, h: (layer, 0, 0)),
    ]
    args = [P[SEG["q_l"][0]], P[SEG["k_l"][0]], P[SEG["v_l"][0]], P["s"], P[SEG["g_gla"][0]],
            w2, b2.reshape(DEPTH, 2, 1, GLA_KEY), nw.reshape(DEPTH, 1, GLA_DV)]
    has_state = states_in is not None
    st_block = (None, None, hps, GLA_DK, GLA_DV)
    if has_state:
        for s in states_in:
            in_specs.append(pl.BlockSpec(st_block, lambda b, h: (b, layer, h, 0, 0)))
            args.append(s)
    out_specs = [pl.BlockSpec((S, vw), lambda b, h: (sb + b, h))]
    out_shape = [jax.ShapeDtypeStruct((n_tok, GLA_WIDTH), BF16)]
    aliased = [out_prev]
    emit_state = states_out is not None
    if emit_state:
        for s in states_out:
            out_specs.append(pl.BlockSpec(st_block, lambda b, h: (b, layer, h, 0, 0)))
            out_shape.append(jax.ShapeDtypeStruct((n_batch, DEPTH, GLA_HEADS, GLA_DK, GLA_DV), F32))
            aliased.append(s)
    aliases = {}
    for oi, buf in enumerate(aliased):
        if buf is not None:
            aliases[len(args)] = oi
            in_specs.append(pl.BlockSpec(memory_space=pl.ANY))
            args.append(buf)
    n = S // C
    return pl.pallas_call(
        functools.partial(_gla_kernel, S=S, C=C, hps=hps, has_state=has_state, emit_state=emit_state,
                          n_alias=len(aliases)),
        grid=(n_batch, GLA_HEADS // hps),
        in_specs=in_specs,
        out_specs=out_specs,
        out_shape=out_shape,
        input_output_aliases=aliases,
        scratch_shapes=[pltpu.VMEM((hps, n, GLA_DV, C), BF16), pltpu.VMEM((2, S, kw), F32),
                        pltpu.VMEM((2, S, vw), F32), pltpu.VMEM((2, hps, GLA_DV, GLA_DK), F32)],
        compiler_params=_cparams(("arbitrary", "arbitrary")),
        name="gla_latent" if has_state else "gla_context",
    )(*args)


CONV_PAD = SUBLANES
CONV_ROWS = 256


def _conv_silu(dst_ref, src_ref, w_ref, b_ref, xp_ref, S):
    W = src_ref.shape[1]
    left = (SSM_CONV - 1) // 2
    xp_ref[0:CONV_PAD, 0:W] = jnp.zeros((CONV_PAD, W), F32)
    xp_ref[CONV_PAD + S:CONV_PAD + S + CONV_PAD, 0:W] = jnp.zeros((CONV_PAD, W), F32)
    xp_ref[CONV_PAD:CONV_PAD + S, 0:W] = src_ref[...]
    for r in range(0, S, CONV_ROWS):
        acc = b_ref[...] + xp_ref[CONV_PAD - left + r:CONV_PAD - left + r + CONV_ROWS, 0:W] * w_ref[0:1, :]
        for j in range(1, SSM_CONV):
            r0 = CONV_PAD - left + j + r
            acc = acc + xp_ref[r0:r0 + CONV_ROWS, 0:W] * w_ref[j:j + 1, :]
        dst_ref[r:r + CONV_ROWS, :] = _silu(acc)
def _ssd_kernel(*refs, S, C, has_state, emit_state, n_alias):
    refs = _drop_alias_refs(refs, 15 + (2 if has_state else 0) + n_alias, n_alias)
    it = iter(refs)
    xin_ref = next(it)
    bin_ref = next(it)
    cin_ref = next(it)
    cw_refs = (next(it), next(it), next(it))
    cb_refs = (next(it), next(it), next(it))
    z_ref = next(it)
    dtc_ref = next(it)
    dtr_ref = next(it)
    pc_ref = next(it)
    pr_ref = next(it)
    d_ref = next(it)
    s0_refs = (next(it), next(it)) if has_state else None
    y_ref = next(it)
    so_refs = (next(it), next(it)) if emit_state else None
    bt_ref = next(it)
    acc_ref = next(it)
    st_ref = next(it)
    xp_ref = next(it)
    xh_ref = next(it)
    x_ref = next(it)
    b_ref = next(it)
    c_ref = next(it)

    for dst, src, w, b in zip((x_ref, b_ref, c_ref), (xin_ref, bin_ref, cin_ref), cw_refs, cb_refs):
        _conv_silu(dst, src, w, b, xp_ref, S)

    n = S // C
    P = SSM_HEADDIM
    for r in range(0, S, CONV_ROWS):
        xr = x_ref[r:r + CONV_ROWS, :]
        for j in range(SSM_HPG):
            xh_ref[j, r:r + CONV_ROWS, :] = xr[:, j * P:(j + 1) * P].astype(BF16)
    for c in range(n):
        bt_ref[c] = b_ref[c * C:(c + 1) * C, :].T
    for d in range(2):
        if has_state:
            s0_t = jnp.concatenate([s0_refs[d][j] for j in range(SSM_HPG)], axis=0).T
            for j in range(SSM_HPG):
                st_ref[d, j] = s0_t[:, j * P:(j + 1) * P]
        else:
            st_ref[d] = jnp.zeros((SSM_HPG, SSM_STATE, P), F32)

    row = lax.broadcasted_iota(jnp.int32, (C, C), 0)
    col = lax.broadcasted_iota(jnp.int32, (C, C), 1)
    lower = row >= col
    upper = row <= col
    tri_l = jnp.where(lower, 1.0, 0.0).astype(BF16)
    tri_u = jnp.where(upper, 1.0, 0.0).astype(BF16)

    def chunk(c, d):
        r0 = pl.multiple_of(c * C, C)
        bm = b_ref[pl.ds(r0, C), :].astype(BF16)
        cm = c_ref[pl.ds(r0, C), :]
        bt = bt_ref[c]
        a_c = -jnp.exp(pc_ref[d, 1])
        a_r = -jnp.exp(pr_ref[d, 1])
        dt_c = _softplus(dtc_ref[d, pl.ds(r0, C), :] + pc_ref[d, 0])
        dt_r = _softplus(dtr_ref[d, c] + pr_ref[d, 0])
        if d == 0:
            cum_c = _tri_left(tri_l, dt_c * a_c)
            cum_r = _tri_right(dt_r * a_r, tri_u)
        else:
            cum_c = _tri_left(tri_u, dt_c * a_c)
            cum_r = _tri_right(dt_r * a_r, tri_l)
        mask = lower if d == 0 else upper
        last_r = cum_r[:, C - 1:C] if d == 0 else cum_r[:, 0:1]
        u_r = dt_r * jnp.exp(last_r - cum_r)
        e_last = jnp.exp(last_r)
        cb = _dot_nt(cm.astype(BF16), bm)
        cq2 = cum_c * LOG2_E
        cr2 = cum_r * LOG2_E - jnp.log2(dt_r)
        for j in range(SSM_HPG):
            cum_q = jnp.broadcast_to(cq2[:, j:j + 1], (C, C))
            dec = jnp.where(mask, jnp.exp2(cum_q - cr2[j:j + 1, :]), 0.0)
            lhs = jnp.concatenate([(cb * dec).astype(BF16),
                                   (cm * jnp.exp2(cum_q)).astype(BF16)], axis=1)
            x_j = xh_ref[j, pl.ds(r0, C), :]
            st_j = st_ref[d, j]
            acc_ref[d, j, pl.ds(r0, C), :] = _dot(lhs, jnp.concatenate([x_j, st_j.astype(BF16)], axis=0))
            st_ref[d, j] = st_j * e_last[j:j + 1, :] + _dot((bt * u_r[j:j + 1, :]).astype(BF16), x_j)

    def both(i, carry):
        chunk(i, 0)
        chunk(n - 1 - i, 1)
        return carry

    lax.fori_loop(0, n, both, 0)

    for r in range(0, S, CONV_ROWS):
        rows = slice(r, r + CONV_ROWS)
        y = jnp.concatenate([acc_ref[0, j, rows, :] + acc_ref[1, j, rows, :] for j in range(SSM_HPG)], axis=-1)
        y = y + x_ref[rows, :] * d_ref[...]
        y_ref[rows, :] = y * _silu(z_ref[rows, :])
    if emit_state:
        for d in range(2):
            s_t = jnp.concatenate([st_ref[d, j] for j in range(SSM_HPG)], axis=1).T
            for j in range(SSM_HPG):
                so_refs[d][j] = s_t[j * P:(j + 1) * P, :]


def _ssd(P, cw, cb, dt_col, dt_row, p_col, p_row, d_lane, *, layer, n_tok, n_batch, S, row0,
         states_in=None, states_out=None, out_prev=None):
    C = SSM_CHUNK
    sb = row0 // S
    z_reg, z_col = SEG["z"]
    x_reg, x_col = SEG["xbc"]
    zj = z_col // SSM_GW
    xj = x_col // SSM_GW
    bj = SSM_WIDTH // SSM_STATE
    cj = (SSM_WIDTH + SSM_BC) // SSM_STATE
    xbj = x_col // SSM_STATE

    def taps(width, j0):
        return pl.BlockSpec((None, SSM_CONV, width), lambda b, g: (layer, 0, j0 + g))

    def bias(width, j0):
        return pl.BlockSpec((None, 1, width), lambda b, g: (layer, 0, j0 + g))

    in_specs = [
        pl.BlockSpec((S, SSM_GW), lambda b, g: (sb + b, xj + g)),
        pl.BlockSpec((S, SSM_STATE), lambda b, g: (sb + b, xbj + bj + g)),
        pl.BlockSpec((S, SSM_STATE), lambda b, g: (sb + b, xbj + cj + g)),
        taps(SSM_GW, 0), taps(SSM_STATE, bj), taps(SSM_STATE, cj),
        bias(SSM_GW, 0), bias(SSM_STATE, bj), bias(SSM_STATE, cj),
        pl.BlockSpec((S, SSM_GW), lambda b, g: (sb + b, zj + g)),
        pl.BlockSpec((2, None, S, SSM_HPG), lambda b, g: (0, g, sb + b, 0)),
        pl.BlockSpec((2, None, S // C, SSM_HPG, C), lambda b, g: (0, g, sb + b, 0, 0)),
        pl.BlockSpec((None, 2, 2, None, 1, SSM_HPG), lambda b, g: (layer, 0, 0, g, 0, 0)),
        pl.BlockSpec((None, 2, 2, None, SSM_HPG, 1), lambda b, g: (layer, 0, 0, g, 0, 0)),
        pl.BlockSpec((None, 1, SSM_GW), lambda b, g: (layer, 0, g)),
    ]
    cb3 = cb.reshape(DEPTH, 1, SSM_CONV_CH)
    args = [P[x_reg], P[x_reg], P[x_reg], cw, cw, cw, cb3, cb3, cb3, P[z_reg], dt_col, dt_row, p_col, p_row,
            d_lane]
    has_state = states_in is not None
    st_block = (None, None, SSM_HPG, SSM_HEADDIM, SSM_STATE)
    if has_state:
        for s in states_in:
            in_specs.append(pl.BlockSpec(st_block, lambda b, g: (b, layer, g, 0, 0)))
            args.append(s)
    out_specs = [pl.BlockSpec((S, SSM_GW), lambda b, g: (sb + b, g))]
    out_shape = [jax.ShapeDtypeStruct((n_tok, SSM_WIDTH), F32)]
    aliased = [out_prev]
    emit_state = states_out is not None
    if emit_state:
        for s in states_out:
            out_specs.append(pl.BlockSpec(st_block, lambda b, g: (b, layer, g, 0, 0)))
            out_shape.append(jax.ShapeDtypeStruct((n_batch, DEPTH, SSM_HEADS, SSM_HEADDIM, SSM_STATE), F32))
            aliased.append(s)
    aliases = {}
    for oi, buf in enumerate(aliased):
        if buf is not None:
            aliases[len(args)] = oi
            in_specs.append(pl.BlockSpec(memory_space=pl.ANY))
            args.append(buf)
    n = S // C
    return pl.pallas_call(
        functools.partial(_ssd_kernel, S=S, C=C, has_state=has_state, emit_state=emit_state,
                          n_alias=len(aliases)),
        grid=(n_batch, SSM_GROUPS),
        in_specs=in_specs,
        out_specs=out_specs,
        out_shape=out_shape,
        input_output_aliases=aliases,
        scratch_shapes=[pltpu.VMEM((n, SSM_STATE, C), F32), pltpu.VMEM((2, SSM_HPG, S, SSM_HEADDIM), F32),
                        pltpu.VMEM((2, SSM_HPG, SSM_STATE, SSM_HEADDIM), F32),
                        pltpu.VMEM((S + 2 * CONV_PAD, SSM_GW), F32),
                        pltpu.VMEM((SSM_HPG, S, SSM_HEADDIM), BF16), pltpu.VMEM((S, SSM_GW), F32),
                        pltpu.VMEM((S, SSM_STATE), F32), pltpu.VMEM((S, SSM_STATE), F32)],
        compiler_params=_cparams(("arbitrary", "arbitrary")),
        name="ssd_latent" if has_state else "ssd_context",
    )(*args)


def _gather_small_w_in(w_in_t):
    names = REGIONS["s"]
    rows = [w_in_t[:, _IN_OFF[n]:_IN_OFF[n] + _IN_SIZE[n], :] for n in names]
    used = sum(_IN_SIZE[n] for n in names)
    rows.append(jnp.zeros((DEPTH, REGION_W["s"] - used, D_MODEL), w_in_t.dtype))
    return jnp.concatenate(rows, axis=1)


def kernel(x_prompt, x_sample, cache_mla_ckv, cache_mla_krope, state_gla_fwd, state_gla_bwd,
           state_ssm_fwd, state_ssm_bwd, c, c_ctx, w_mod, b_mod, norm_w, w_in, q_a_norm, w_q_b,
           kv_a_norm, w_kv_b, gla_w_gate2, gla_b_gate, gla_norm, ssm_conv_w, ssm_conv_b,
           ssm_dt_bias, ssm_a_log, ssm_d, ssm_norm, w_br_mla, w_br_gla, w_br_ssm, w_out, final_norm):
    nb_p, s_p, _ = x_prompt.shape
    nb_l, s_l, _ = x_sample.shape
    n_p = nb_p * s_p
    n_l = nb_l * s_l
    n_tok = n_p + n_l
    assert s_p == SEQ_BLOCK and s_l % SEQ_BLOCK == 0 and n_p % s_l == 0
    t_c = cache_mla_ckv.shape[2]
    assert t_c == SEQ_BLOCK

    x = jnp.concatenate([x_prompt.reshape(n_p, D_MODEL), x_sample.reshape(n_l, D_MODEL)], axis=0)

    w_in_t = jnp.swapaxes(w_in, 1, 2)
    w_in_s = _gather_small_w_in(w_in_t)
    wq = w_q_b.reshape(DEPTH, Q_LORA, MLA_HEADS, QK_NOPE + QK_ROPE)
    wq_nope = wq[..., :QK_NOPE].reshape(DEPTH, Q_LORA, MLA_HEADS * QK_NOPE)
    wq_rope = wq[..., QK_NOPE:].reshape(DEPTH, Q_LORA, MLA_HEADS * QK_ROPE)
    tables = _rope_tables(s_l)
    p_ssm = jnp.stack([ssm_dt_bias, ssm_a_log], axis=2).reshape(DEPTH, 2, 2, SSM_GROUPS, SSM_HPG)
    p_col = p_ssm[:, :, :, :, None, :]
    p_row = p_ssm[:, :, :, :, :, None]
    d_lane = jnp.repeat(ssm_d, SSM_HEADDIM, axis=-1).reshape(DEPTH, 1, SSM_WIDTH)
    blk_mod = np.concatenate([np.zeros(n_p // SEQ_BLOCK, np.int32),
                              1 + np.repeat(np.arange(nb_l, dtype=np.int32), s_l // SEQ_BLOCK)])
    cond = jnp.concatenate([c_ctx[None, :], c], axis=0)
    cond = _silu(cond)
    cond = jnp.concatenate([cond, jnp.zeros((8 - cond.shape[0] % 8, D_MODEL), F32)], axis=0)

    gla_f = gla_b = ssm_f = ssm_b = None
    ckv_l, kr_l = [], []
    for l in range(DEPTH):
        mod = _matmul(cond, w_mod, name="mm_mod", layer=l, n_out=3 * D_MODEL, tm=cond.shape[0], tn=1536,
                      out_dtype=F32)
        mod = mod[:1 + nb_l] + b_mod[l][None, :]
        mod_rows = mod[blk_mod][:, None, :]
        gate_rows = mod_rows[:, :, 2 * D_MODEL:]

        h = _prenorm(x, norm_w[l], mod_rows)
        P = {}
        for r in ("a", "b", "c"):
            first = REGIONS[r][0]
            P[r] = _matmul(h, w_in_t, name="mm_in_" + r, layer=l, w_t=True, w_col0=_IN_OFF[first],
                           n_out=REGION_W[r], tm=2048, tn=IN_TN, out_dtype=F32, w_single_buffer=True)
        P["s"] = _matmul(h, w_in_s, name="mm_in_s", layer=l, w_t=True, n_out=REGION_W["s"], tm=1536, tn=IN_TN,
                         out_dtype=F32)
        small = P["s"][:, SMALL_COL0:SMALL_COL0 + SMALL_W]

        qa_col, kva_col = SEG["q_a"][1], SEG["kv_a"][1]
        qn = _matmul(P["s"], wq_nope, name="mm_q_nope", x_col0=qa_col, k=Q_LORA, layer=l,
                     n_out=MLA_HEADS * QK_NOPE, tm=1024, tn=1024, out_dtype=BF16, gain=q_a_norm[l])
        qr = _matmul(P["s"], wq_rope, name="mm_q_rope", x_col0=qa_col, k=Q_LORA, layer=l,
                     n_out=MLA_HEADS * QK_ROPE, tm=1024, tn=1024, out_dtype=F32, gain=q_a_norm[l])
        ckv = _rmsnorm(P["s"], kv_a_norm[l], x_col0=kva_col, width=KV_LORA)
        ckv_all = jnp.concatenate([ckv, cache_mla_ckv[:, l].reshape(nb_l * t_c, KV_LORA)], axis=0)
        kv = _matmul(ckv_all, w_kv_b, name="mm_kv", layer=l, n_out=MLA_HEADS * (QK_NOPE + V_HEAD),
                     tm=ckv_all.shape[0] // 4, tn=1024, out_dtype=BF16)
        o_mla = _mla(qn, qr, P, kv, n_tok=n_tok, n_batch=nb_p, S=s_p, row0=0, latent=False)
        o_mla = _mla(qn, qr, P, kv, n_tok=n_tok, n_batch=nb_l, S=s_l, row0=n_p, latent=True,
                     kv_ctx_row0=n_tok, krope_ctx=cache_mla_krope[:, l], tables=tables, out_prev=o_mla)
        ckv_l.append(ckv[:n_p].reshape(nb_p, s_p, KV_LORA))
        kr_l.append(small[:n_p, SM["k_r"]:SM["k_r"] + QK_ROPE].reshape(nb_p, s_p, QK_ROPE))

        o_gla, gla_f, gla_b = _gla(P, gla_w_gate2, gla_b_gate, gla_norm, layer=l, n_tok=n_tok, n_batch=nb_p,
                                   S=s_p, row0=0, states_out=(gla_f, gla_b))
        (o_gla,) = _gla(P, gla_w_gate2, gla_b_gate, gla_norm, layer=l, n_tok=n_tok, n_batch=nb_l, S=s_l,
                        row0=n_p, states_in=(state_gla_fwd, state_gla_bwd), out_prev=o_gla)

        dt = small[:, SM["dt_f"]:SM["dt_f"] + 2 * SSM_HEADS].reshape(n_tok, 2, SSM_GROUPS, SSM_HPG)
        dt_col = jnp.transpose(dt, (1, 2, 0, 3))
        dt_row = jnp.transpose(dt.reshape(n_tok // SSM_CHUNK, SSM_CHUNK, 2, SSM_GROUPS, SSM_HPG),
                               (2, 3, 0, 4, 1))
        yz, ssm_f, ssm_b = _ssd(P, ssm_conv_w, ssm_conv_b, dt_col, dt_row, p_col, p_row, d_lane, layer=l,
                                n_tok=n_tok, n_batch=nb_p, S=s_p, row0=0, states_out=(ssm_f, ssm_b))
        (yz,) = _ssd(P, ssm_conv_w, ssm_conv_b, dt_col, dt_row, p_col, p_row, d_lane, layer=l, n_tok=n_tok,
                     n_batch=nb_l, S=s_l, row0=n_p, states_in=(state_ssm_fwd, state_ssm_bwd), out_prev=yz)

        mg = _matmul(o_mla, w_br_mla, name="mm_br_mla", layer=l, n_out=D_MODEL, tm=1024, tn=1024, out_dtype=F32,
                     w_single_buffer=True, ep="sig", m=P["c"], m_col0=SEG["m_mla"][1])
        mg = _matmul(o_gla, w_br_gla, name="mm_br_gla", layer=l, n_out=D_MODEL, tm=1024, tn=1024, out_dtype=F32,
                     w_single_buffer=True, ep="sigadd", m=P["c"], m_col0=SEG["m_gla"][1], prev=mg)
        mg = _matmul(yz, w_br_ssm, name="mm_br_ssm", layer=l, n_out=D_MODEL, tm=256, tn=1024, out_dtype=BF16,
                     gain=ssm_norm[l], ep="sigadd", m=P["c"], m_col0=SEG["m_ssm"][1], prev=mg,
                     w_single_buffer=True)
        x = _matmul(mg, w_out, name="mm_out", layer=l, n_out=D_MODEL, tm=1024, tn=1024, out_dtype=F32,
                    w_single_buffer=True, ep="resid", res=x, gate_rows=gate_rows)

    y = _rmsnorm(x, final_norm)
    y_prompt = y[:n_p].reshape(nb_p, s_p, D_MODEL)
    y_sample = y[n_p:].reshape(nb_l, s_l, D_MODEL)
    return (y_prompt, y_sample, jnp.stack(ckv_l, axis=1), jnp.stack(kr_l, axis=1),
            gla_f, gla_b, ssm_f, ssm_b)
```
